```python
import math
import jax
import jax.numpy as jnp
from jax import lax
import numpy as np

D_MODEL = 2048
BATCH = 2
SEQ = 16384
DEPTH = 4

N_MIXERS = 3
F32 = jnp.float32
RWKV_HEAD = 64
RWKV_HEADS = D_MODEL // RWKV_HEAD
RWKV_DECAY_LORA = 96
RWKV_AAA_LORA = 96
RWKV_MV_LORA = 64
RWKV_GATE_LORA = 256
RWKV_GN_EPS = 64e-5
DSA_HEADS = 16
DSA_HEAD_DIM = 128
DSA_ROPE_DIM = DSA_HEAD_DIM // 4
DSA_NOPE_DIM = DSA_HEAD_DIM - DSA_ROPE_DIM
DSA_V_DIM = 128
DSA_KV_RANK = 512
IDX_HEADS = 16
IDX_DIM = 64
IDX_ROPE_DIM = IDX_DIM // 4
TOPK_MAX = 256
Q_BLOCK = 128
S5_GROUP = 16
S5_GROUPS = D_MODEL // S5_GROUP
S5_STATE = 64
S5_GROUPS_PER_STEP = 8
N_EXPERTS = 32
N_GROUPS = 4
EXPERTS_PER_GROUP = N_EXPERTS // N_GROUPS
TOP_K = 2
D_EXPERT = 512
MOE_BLOCK = 128
ROPE_THETA = 500000.0
LN_EPS = 1e-5
DEEPNORM_ALPHA = (2 * DEPTH) ** 0.25
DEEPNORM_BETA = (8 * DEPTH) ** -0.25

kernel_name = 'hybrid_rwkv7_dsa_s5_moe_trunk'


def _n_layers_of(kind):
    return len(range(kind, DEPTH, N_MIXERS))


def layer_norm(x, g, b, eps=LN_EPS):
    xf = x.astype(F32)
    mu = jnp.mean(xf, -1, keepdims=True)
    var = jnp.mean(jnp.square(xf - mu), -1, keepdims=True)
    return ((xf - mu) * lax.rsqrt(var + eps)).astype(x.dtype) * g + b


def rms_norm(x, g, eps=1e-6):
    xf = x.astype(F32)
    return (xf * lax.rsqrt(jnp.mean(xf * xf, -1, keepdims=True) + eps)).astype(x.dtype) * g


def apply_partial_rope(x, positions, rot_dim):
    half = rot_dim // 2
    inv_freq = ROPE_THETA ** (-jnp.arange(half, dtype=F32) * (2.0 / rot_dim))
    ang = positions.astype(F32)[:, :, None, None] * inv_freq
    cos, sin = jnp.cos(ang), jnp.sin(ang)
    xr = x[..., :rot_dim].astype(F32)
    x1, x2 = xr[..., :half], xr[..., half:]
    rot = jnp.concatenate([x1 * cos - x2 * sin, x2 * cos + x1 * sin], axis=-1).astype(x.dtype)
    return jnp.concatenate([rot, x[..., rot_dim:]], axis=-1)


def rwkv7_time_mix(h, v_first, vres, mu, w_rkv, w_o, w0, w1, w2, a0, a1, a2, g1, g2,
                   k_k, k_a, r_k, gn_g, gn_b):
    B, T, D = h.shape
    H, N = RWKV_HEADS, RWKV_HEAD
    dx = jnp.pad(h, ((0, 0), (1, 0), (0, 0)))[:, :-1] - h
    xr, xw, xk, xv, xa, xg = (h + dx * mu[m] for m in range(6))
    r = xr @ w_rkv[0]
    k = xk @ w_rkv[1]
    v = xv @ w_rkv[2]
    log_w = -jax.nn.softplus(-(w0 + jnp.tanh(xw @ w1) @ w2)) - 0.5
    decay = jnp.exp(-jnp.exp(log_w.astype(F32)))
    a = jax.nn.sigmoid(a0 + (xa @ a1) @ a2)
    g = jax.nn.sigmoid(xg @ g1) @ g2
    kk = (k * k_k).astype(F32).reshape(B, T, H, N)
    kk = kk / jnp.maximum(jnp.sqrt(jnp.sum(kk * kk, -1, keepdims=True)), 1e-12)
    k = k * (1.0 + (a - 1.0) * k_a)
    if vres is None:
        v_first = v
    else:
        v0, v1, v2 = vres
        v = v + (v_first - v) * jax.nn.sigmoid(v0 + (xv @ v1) @ v2)

    def heads_tm(z):
        return jnp.moveaxis(z.astype(F32).reshape(B, T, H, N), 1, 0)

    xs = (heads_tm(r), heads_tm(decay), heads_tm(k), heads_tm(v), jnp.moveaxis(kk, 1, 0), heads_tm(a))

    def step(S, inp):
        r_t, w_t, k_t, v_t, kk_t, a_t = inp
        sa = jnp.einsum('bhvk,bhk->bhv', S, -kk_t)
        S = (S * w_t[:, :, None, :] + sa[..., None] * (kk_t * a_t)[:, :, None, :]
             + v_t[..., None] * k_t[:, :, None, :])
        return S, jnp.einsum('bhvk,bhk->bhv', S, r_t)

    _, y = lax.scan(step, jnp.zeros((B, H, N, N), F32), xs)
    y = jnp.moveaxis(y, 0, 1)
    m_y = jnp.mean(y, -1, keepdims=True)
    v_y = jnp.mean(jnp.square(y - m_y), -1, keepdims=True)
    y = ((y - m_y) * lax.rsqrt(v_y + RWKV_GN_EPS)).reshape(B, T, D) * gn_g.astype(F32) + gn_b.astype(F32)
    bonus = (jnp.sum((r * k * r_k).astype(F32).reshape(B, T, H, N), -1, keepdims=True)
             * v.astype(F32).reshape(B, T, H, N))
    y = (y + bonus.reshape(B, T, D)).astype(h.dtype)
    return (y * g) @ w_o, v_first


def dsa_attention(h, positions, w_in, kv_norm_g, w_uk, w_uv, idx_ln_g, idx_ln_b, w_o):
    B, T, D = h.shape
    H, Hi = DSA_HEADS, IDX_HEADS
    topk = min(TOPK_MAX, T // 4)
    sizes = [H * DSA_HEAD_DIM, DSA_KV_RANK, DSA_ROPE_DIM, Hi * IDX_DIM, IDX_DIM, Hi]
    cuts = [int(s) for s in np.cumsum(sizes)[:-1]]
    q, c_kv, k_rope, q_idx, k_idx, w_idx = jnp.split(h @ w_in, cuts, axis=-1)
    q = apply_partial_rope(q.reshape(B, T, H, DSA_HEAD_DIM), positions, DSA_ROPE_DIM)
    q_rope, q_nope = q[..., :DSA_ROPE_DIM], q[..., DSA_ROPE_DIM:]
    c_kv = rms_norm(c_kv, kv_norm_g)
    k_rope = apply_partial_rope(k_rope[:, :, None, :], positions, DSA_ROPE_DIM)[:, :, 0]
    q_idx = apply_partial_rope(q_idx.reshape(B, T, Hi, IDX_DIM), positions, IDX_ROPE_DIM)
    k_idx = apply_partial_rope(layer_norm(k_idx, idx_ln_g, idx_ln_b)[:, :, None, :], positions,
                               IDX_ROPE_DIM)[:, :, 0]
    w_idx = w_idx * (Hi ** -0.5 * IDX_DIM ** -0.5)
    key_pos = jnp.arange(T)
    gather = jax.vmap(lambda src, idx: src[idx])
    scale = DSA_HEAD_DIM ** -0.5

    def block(j):
        s0 = j * Q_BLOCK
        sl = lambda z: lax.dynamic_slice_in_dim(z, s0, Q_BLOCK, axis=1)
        qi, wi, qr, qn = sl(q_idx), sl(w_idx), sl(q_rope), sl(q_nope)
        t = s0 + jnp.arange(Q_BLOCK)
        causal = key_pos[None, :] <= t[:, None]
        isc = jnp.einsum('bqh,bqhs->bqs', wi, jax.nn.relu(jnp.einsum('bqhd,bsd->bqhs', qi, k_idx)))
        isc = jnp.where(causal[None], isc.astype(F32), -jnp.inf)
        _, sel = lax.top_k(isc, topk)
        valid = sel <= t[None, :, None]
        c_sel = gather(c_kv, sel)
        kr_sel = gather(k_rope, sel)
        q_abs = jnp.einsum('bqhn,hnr->bqhr', qn, w_uk)
        s = jnp.einsum('bqhr,bqkr->bqhk', q_abs, c_sel) + jnp.einsum('bqhd,bqkd->bqhk', qr, kr_sel)
        s = jnp.where(valid[:, :, None, :], s.astype(F32) * scale, -jnp.inf)
        p = jax.nn.softmax(s, axis=-1).astype(h.dtype)
        o_lat = jnp.einsum('bqhk,bqkr->bqhr', p, c_sel)
        return jnp.einsum('bqhr,hrv->bqhv', o_lat, w_uv).reshape(B, Q_BLOCK, H * DSA_V_DIM)

    out = lax.map(block, jnp.arange(T // Q_BLOCK))
    out = jnp.moveaxis(out, 0, 1).reshape(B, T, H * DSA_V_DIM)
    return out @ w_o


def _complex_linear_combine(e1, e2):
    a1r, a1i, b1r, b1i = e1
    a2r, a2i, b2r, b2i = e2
    return (a2r * a1r - a2i * a1i, a2r * a1i + a2i * a1r,
            a2r * b1r - a2i * b1i + b2r, a2r * b1i + a2i * b1r + b2i)


def s5_ssm(h, lam_re, lam_im, log_dt, b_re, b_im, c_re, c_im, d_skip, w_glu):
    B, T, D = h.shape
    G, P, Hg = S5_GROUPS, S5_STATE, S5_GROUP
    n_steps, gps = G // S5_GROUPS_PER_STEP, S5_GROUPS_PER_STEP
    dt = jnp.exp(log_dt.astype(F32))[:, None]
    lr, li = lam_re.astype(F32), lam_im.astype(F32)
    mag = jnp.exp(lr * dt)
    ab_re, ab_im = mag * jnp.cos(li * dt), mag * jnp.sin(li * dt)
    den = lr * lr + li * li
    nr, ni = ab_re - 1.0, ab_im
    coef_re = (nr * lr + ni * li) / den
    coef_im = (ni * lr - nr * li) / den
    br, bi = b_re.astype(F32), b_im.astype(F32)
    bb_re = coef_re[..., None] * br - coef_im[..., None] * bi
    bb_im = coef_re[..., None] * bi + coef_im[..., None] * br
    u = h.astype(F32)
    u_g = jnp.moveaxis(u.reshape(B, T, n_steps, gps, Hg), 2, 0)
    chunk = lambda z: z.reshape((n_steps, gps) + z.shape[1:])

    def group_block(args):
        u_c, a_r, a_i, bb_r, bb_i, cr, ci = args
        bu_r = jnp.einsum('btgh,gph->btgp', u_c, bb_r)
        bu_i = jnp.einsum('btgh,gph->btgp', u_c, bb_i)
        ar = jnp.broadcast_to(a_r, bu_r.shape)
        ai = jnp.broadcast_to(a_i, bu_r.shape)
        _, _, xr, xi = lax.associative_scan(_complex_linear_combine, (ar, ai, bu_r, bu_i), axis=1)
        return jnp.einsum('btgp,ghp->btgh', xr, cr) - jnp.einsum('btgp,ghp->btgh', xi, ci)

    y = lax.map(group_block, (u_g, chunk(ab_re), chunk(ab_im), chunk(bb_re), chunk(bb_im),
                              chunk(c_re.astype(F32)), chunk(c_im.astype(F32))))
    y = jnp.moveaxis(y, 0, 2).reshape(B, T, D) + d_skip.astype(F32) * u
    z = jax.nn.gelu(y.astype(h.dtype))
    zz = z @ w_glu
    return zz[..., :D] * jax.nn.sigmoid(zz[..., D:])


def route(hf, router_w, router_bias):
    n = hf.shape[0]
    scores = jax.nn.sigmoid((hf @ router_w).astype(F32))
    biased = (scores + router_bias.astype(F32)).reshape(n, N_GROUPS, EXPERTS_PER_GROUP)
    group_score = jnp.sum(lax.top_k(biased, TOP_K)[0], -1)
    g_sel = jnp.argmax(group_score, -1)
    in_group = biased[jnp.arange(n), g_sel]
    _, local = lax.top_k(in_group, TOP_K)
    expert_idx = g_sel[:, None] * EXPERTS_PER_GROUP + local
    gate = jnp.take_along_axis(scores, expert_idx, axis=1)
    return expert_idx, gate / jnp.sum(gate, -1, keepdims=True)


def moe_ffn(h, router_w, router_bias, w_gate, w_up, w_down):
    B, T, D = h.shape
    n_tok = B * T
    hf = h.reshape(n_tok, D)
    expert_idx, gate = route(hf, router_w, router_bias)
    n_slots = n_tok * TOP_K
    flat_e = expert_idx.reshape(-1)
    flat_tok = jnp.repeat(jnp.arange(n_tok, dtype=jnp.int32), TOP_K)
    flat_gate = gate.reshape(-1).astype(h.dtype)
    order = jnp.argsort(flat_e)
    sorted_e = flat_e[order]
    counts = jnp.bincount(flat_e, length=N_EXPERTS)
    start = jnp.cumsum(counts) - counts
    padded = (counts + MOE_BLOCK - 1) // MOE_BLOCK * MOE_BLOCK
    padded_end = jnp.cumsum(padded)
    padded_start = padded_end - padded
    dest = padded_start[sorted_e] + jnp.arange(n_slots) - start[sorted_e]
    n_blocks = -(-n_slots // MOE_BLOCK) + N_EXPERTS
    n_pad = n_blocks * MOE_BLOCK
    slot_tok = jnp.zeros((n_pad,), jnp.int32).at[dest].set(flat_tok[order])
    slot_gate = jnp.zeros((n_pad,), h.dtype).at[dest].set(flat_gate[order])
    block_expert = jnp.minimum(
        jnp.searchsorted(padded_end, jnp.arange(n_blocks) * MOE_BLOCK, side='right'), N_EXPERTS - 1)

    def expert_block(args):
        tok, e = args
        xb = hf[tok]
        return (jax.nn.silu(xb @ w_gate[e]) * (xb @ w_up[e])) @ w_down[e]

    yb = lax.map(expert_block, (slot_tok.reshape(n_blocks, MOE_BLOCK), block_expert))
    yb = yb.reshape(n_pad, D) * slot_gate[:, None]
    return jax.ops.segment_sum(yb, slot_tok, num_segments=n_tok).reshape(B, T, D)


def setup_inputs(seed: int = 0) -> dict:
    key = jax.random.key(seed)
    keys = iter(jax.random.split(key, 80))

    def nrm(shape, std):
        return std * jax.random.normal(next(keys), shape, F32)

    def uni(shape, lo, hi):
        return jax.random.uniform(next(keys), shape, F32, lo, hi)

    D, beta = D_MODEL, DEEPNORM_BETA
    n_a, n_b, n_c = _n_layers_of(0), _n_layers_of(1), _n_layers_of(2)
    n_v = max(n_a - 1, 0)
    H, R, Hi = DSA_HEADS, DSA_KV_RANK, IDX_HEADS
    G, P, Hg = S5_GROUPS, S5_STATE, S5_GROUP
    d_in = H * DSA_HEAD_DIM + R + DSA_ROPE_DIM + Hi * IDX_DIM + IDX_DIM + Hi
    positions = (jnp.arange(SEQ, dtype=jnp.int32)[None, :]
                 + jax.random.randint(next(keys), (BATCH, 1), 0, 4096, jnp.int32))
    return {
        'x': nrm((BATCH, SEQ, D), 1.0),
        'c': nrm((BATCH, D), 1.0),
        'positions': positions,
        'ada_w': nrm((DEPTH, D, 6 * D), 0.1 * D ** -0.5),
        'ada_b': nrm((DEPTH, 6 * D), 0.01),
        'ln_g': 1.0 + nrm((DEPTH, 2, D), 0.01),
        'ln_b': nrm((DEPTH, 2, D), 0.01),
        'router_w': nrm((D, N_EXPERTS), D ** -0.5),
        'router_bias': nrm((N_EXPERTS,), 0.01),
        'moe_w_gate': nrm((DEPTH, N_EXPERTS, D, D_EXPERT), D ** -0.5),
        'moe_w_up': nrm((DEPTH, N_EXPERTS, D, D_EXPERT), D ** -0.5),
        'moe_w_down': nrm((DEPTH, N_EXPERTS, D_EXPERT, D), beta * D_EXPERT ** -0.5),
        'rwkv_mu': uni((n_a, 6, D), 0.0, 1.0),
        'rwkv_w_rkv': nrm((n_a, 3, D, D), D ** -0.5),
        'rwkv_w_o': nrm((n_a, D, D), beta * D ** -0.5),
        'rwkv_w0': uni((n_a, D), -6.0, -1.0),
        'rwkv_w1': nrm((n_a, D, RWKV_DECAY_LORA), 0.1 * D ** -0.5),
        'rwkv_w2': nrm((n_a, RWKV_DECAY_LORA, D), 0.1 * RWKV_DECAY_LORA ** -0.5),
        'rwkv_a0': nrm((n_a, D), 0.1),
        'rwkv_a1': nrm((n_a, D, RWKV_AAA_LORA), 0.1 * D ** -0.5),
        'rwkv_a2': nrm((n_a, RWKV_AAA_LORA, D), RWKV_AAA_LORA ** -0.5),
        'rwkv_g1': nrm((n_a, D, RWKV_GATE_LORA), D ** -0.5),
        'rwkv_g2': nrm((n_a, RWKV_GATE_LORA, D), RWKV_GATE_LORA ** -0.5),
        'rwkv_k_k': 0.85 + nrm((n_a, D), 0.02),
        'rwkv_k_a': 1.0 + nrm((n_a, D), 0.02),
        'rwkv_r_k': nrm((n_a, D), 0.1),
        'rwkv_gn_g': 1.0 + nrm((n_a, D), 0.01),
        'rwkv_gn_b': nrm((n_a, D), 0.01),
        'rwkv_v0': 1.0 + nrm((n_v, D), 0.1),
        'rwkv_v1': nrm((n_v, D, RWKV_MV_LORA), 0.1 * D ** -0.5),
        'rwkv_v2': nrm((n_v, RWKV_MV_LORA, D), RWKV_MV_LORA ** -0.5),
        'dsa_w_in': nrm((n_b, D, d_in), D ** -0.5),
        'dsa_kv_norm': 1.0 + nrm((n_b, R), 0.01),
        'dsa_w_uk': nrm((n_b, H, DSA_NOPE_DIM, R), R ** -0.5),
        'dsa_w_uv': nrm((n_b, H, R, DSA_V_DIM), R ** -0.5),
        'dsa_idx_ln_g': 1.0 + nrm((n_b, IDX_DIM), 0.01),
        'dsa_idx_ln_b': nrm((n_b, IDX_DIM), 0.01),
        'dsa_w_o': nrm((n_b, H * DSA_V_DIM, D), beta * (H * DSA_V_DIM) ** -0.5),
        's5_lam_re': -0.5 + nrm((n_c, G, P), 1e-3),
        's5_lam_im': math.pi * jnp.arange(P, dtype=F32) + nrm((n_c, G, P), 1e-3),
        's5_log_dt': uni((n_c, G), math.log(1e-3), math.log(1e-1)),
        's5_b_re': nrm((n_c, G, P, Hg), (2 * Hg) ** -0.5),
        's5_b_im': nrm((n_c, G, P, Hg), (2 * Hg) ** -0.5),
        's5_c_re': nrm((n_c, G, Hg, P), 2.0 * (2 * P) ** -0.5),
        's5_c_im': nrm((n_c, G, Hg, P), 2.0 * (2 * P) ** -0.5),
        's5_d': nrm((n_c, D), 1.0),
        's5_w_glu': jnp.concatenate([nrm((n_c, D, D), beta * D ** -0.5), nrm((n_c, D, D), D ** -0.5)], axis=-1),
    }


def reference(x, c, positions, ada_w, ada_b, ln_g, ln_b, router_w, router_bias,
              moe_w_gate, moe_w_up, moe_w_down,
              rwkv_mu, rwkv_w_rkv, rwkv_w_o, rwkv_w0, rwkv_w1, rwkv_w2, rwkv_a0, rwkv_a1, rwkv_a2,
              rwkv_g1, rwkv_g2, rwkv_k_k, rwkv_k_a, rwkv_r_k, rwkv_gn_g, rwkv_gn_b,
              rwkv_v0, rwkv_v1, rwkv_v2,
              dsa_w_in, dsa_kv_norm, dsa_w_uk, dsa_w_uv, dsa_idx_ln_g, dsa_idx_ln_b, dsa_w_o,
              s5_lam_re, s5_lam_im, s5_log_dt, s5_b_re, s5_b_im, s5_c_re, s5_c_im, s5_d, s5_w_glu):
    v_first = None
    for i in range(DEPTH):
        kind, j = i % N_MIXERS, i // N_MIXERS
        mod = (c @ ada_w[i] + ada_b[i])[:, None, :]
        sh1, sc1, g1, sh2, sc2, g2 = jnp.split(mod, 6, axis=-1)
        hmix = x * (1.0 + sc1) + sh1
        if kind == 0:
            vres = None if j == 0 else (rwkv_v0[j - 1], rwkv_v1[j - 1], rwkv_v2[j - 1])
            y, v_first = rwkv7_time_mix(hmix, v_first, vres, rwkv_mu[j], rwkv_w_rkv[j], rwkv_w_o[j],
                                        rwkv_w0[j], rwkv_w1[j], rwkv_w2[j], rwkv_a0[j], rwkv_a1[j],
                                        rwkv_a2[j], rwkv_g1[j], rwkv_g2[j], rwkv_k_k[j], rwkv_k_a[j],
                                        rwkv_r_k[j], rwkv_gn_g[j], rwkv_gn_b[j])
        elif kind == 1:
            y = dsa_attention(hmix, positions, dsa_w_in[j], dsa_kv_norm[j], dsa_w_uk[j], dsa_w_uv[j],
                              dsa_idx_ln_g[j], dsa_idx_ln_b[j], dsa_w_o[j])
        else:
            y = s5_ssm(hmix, s5_lam_re[j], s5_lam_im[j], s5_log_dt[j], s5_b_re[j], s5_b_im[j],
                       s5_c_re[j], s5_c_im[j], s5_d[j], s5_w_glu[j])
        x = layer_norm(DEEPNORM_ALPHA * x + (1.0 + g1) * y, ln_g[i, 0], ln_b[i, 0])
        hffn = x * (1.0 + sc2) + sh2
        y = moe_ffn(hffn, router_w, router_bias, moe_w_gate[i], moe_w_up[i], moe_w_down[i])
        x = layer_norm(DEEPNORM_ALPHA * x + (1.0 + g2) * y, ln_g[i, 1], ln_b[i, 1])
    return x
```

```python
import functools
import math

import numpy as np
import jax
import jax.numpy as jnp
from jax import lax
from jax.experimental import pallas as pl
from jax.experimental.pallas import tpu as pltpu

F32, BF16 = jnp.float32, jnp.bfloat16

DEPTH = 4
N_MIXERS = 3
RWKV_HEAD = 64
RWKV_GN_EPS = 64e-5
DSA_HEADS = 16
DSA_HEAD_DIM = 128
DSA_ROPE_DIM = 32
DSA_NOPE_DIM = 96
DSA_V_DIM = 128
DSA_KV_RANK = 512
IDX_HEADS = 16
IDX_DIM = 64
IDX_ROPE_DIM = 16
TOPK_MAX = 256
Q_BLOCK = 128
S5_GROUP = 16
S5_STATE = 64
S5_GROUPS_PER_STEP = 8
N_EXPERTS = 32
N_GROUPS = 4
EXPERTS_PER_GROUP = 8
TOP_K = 2
ROPE_THETA = 500000.0
LN_EPS = 1e-5
DEEPNORM_ALPHA = (2 * DEPTH) ** 0.25

LANES = 128
VMEM_LIMIT = 48 * 1024 * 1024
MOE_ROWS = 256
WKV_CHUNK = 64
WKV_HEADS_PER_STEP = 4
WKV_ROWS_PER_STEP = 512


def _cparams(*sem):
    return pltpu.CompilerParams(dimension_semantics=sem, vmem_limit_bytes=VMEM_LIMIT)


def _pick(n, cands):
    for c in cands:
        if n % c == 0:
            return c
    return n


def _mm_body(a_ref, w_ref, o_ref, *, act):
    acc = jnp.dot(a_ref[...].astype(BF16), w_ref[...], preferred_element_type=F32)
    if act == "tanh":
        acc = jnp.tanh(acc)
    elif act == "sigmoid":
        acc = jax.nn.sigmoid(acc)
    o_ref[...] = acc.astype(o_ref.dtype)


def mm(a, w, *, out_dtype=F32, act=None, tm=512):
    m, k = a.shape
    n = w.shape[1]
    tm = _pick(m, (tm, 256, 128, 64, 32, 16, 8))
    tn = _pick(n, (512, 256, 128))
    return pl.pallas_call(
        functools.partial(_mm_body, act=act),
        grid=(m // tm, n // tn),
        in_specs=[pl.BlockSpec((tm, k), lambda i, j: (i, 0)), pl.BlockSpec((k, tn), lambda i, j: (0, j))],
        out_specs=pl.BlockSpec((tm, tn), lambda i, j: (i, j)),
        out_shape=jax.ShapeDtypeStruct((m, n), out_dtype),
        compiler_params=_cparams("parallel", "parallel"),
        name="mm",
    )(a, w)


def _wpad(w):
    n = w.shape[1]
    npad = -(-n // LANES) * LANES
    w = w.astype(BF16)
    return w if npad == n else jnp.pad(w, ((0, 0), (0, npad - n)))


def _kpad(w):
    k = w.shape[0]
    kpad = -(-k // LANES) * LANES
    w = w.astype(BF16)
    return w if kpad == k else jnp.pad(w, ((0, kpad - k), (0, 0)))


def rowwise(fn, rows, perbatch, consts, outs, *, seq, tm, name):
    n = rows[0].shape[0]
    tm = min(tm, seq)
    tpb = seq // tm
    nr, nb, nc = len(rows), len(perbatch), len(consts)

    def body(*refs):
        vals = [r[...] for r in refs[:nr]]
        vals += [r[0] for r in refs[nr:nr + nb]]
        vals += [r[...] for r in refs[nr + nb:nr + nb + nc]]
        res = fn(*vals)
        for o, v in zip(refs[nr + nb + nc:], res):
            o[...] = v.astype(o.dtype)

    in_specs = [pl.BlockSpec((tm, r.shape[1]), lambda i: (i, 0)) for r in rows]
    in_specs += [pl.BlockSpec((1, 1, p.shape[-1]), lambda i: (i // tpb, 0, 0)) for p in perbatch]
    in_specs += [pl.BlockSpec(c.shape, lambda i: (0, 0)) for c in consts]
    res = pl.pallas_call(
        body,
        grid=(n // tm,),
        in_specs=in_specs,
        out_specs=[pl.BlockSpec((tm, w), lambda i: (i, 0)) for w, _ in outs],
        out_shape=[jax.ShapeDtypeStruct((n, w), dt) for w, dt in outs],
        compiler_params=_cparams("parallel"),
        name=name,
    )(*rows, *perbatch, *consts)
    return res


def _layer_norm(z, g, b):
    mu = jnp.mean(z, -1, keepdims=True)
    zc = z - mu
    var = jnp.mean(zc * zc, -1, keepdims=True)
    return zc * lax.rsqrt(var + LN_EPS) * g + b


def _split_dot(x, m):
    xh = x.astype(BF16)
    xl = (x - xh.astype(F32)).astype(BF16)
    return jnp.dot(xh, m, preferred_element_type=F32) + jnp.dot(xl, m, preferred_element_type=F32)


def _seg_sum(x, ones_bd):
    parts = [_split_dot(x[:, j:j + LANES], ones_bd) for j in range(0, x.shape[1], LANES)]
    return jnp.concatenate(parts, -1)


def _head_ones():
    i = np.arange(LANES)
    return jnp.asarray((i[:, None] // RWKV_HEAD) == (i[None, :] // RWKV_HEAD), BF16)


def _ada_body(c_ref, w_ref, b_ref, o_ref):
    acc = jnp.dot(c_ref[...], w_ref[0], preferred_element_type=F32, precision=lax.Precision.HIGHEST)
    o_ref[0] = acc + b_ref[0]


def ada_mod(c, ada_w, ada_b):
    depth, d, n = ada_w.shape
    b = c.shape[0]
    cp = jnp.pad(c, ((0, 8 - b), (0, 0)))
    tn = 1024
    return pl.pallas_call(
        _ada_body,
        grid=(depth, n // tn),
        in_specs=[pl.BlockSpec((8, d), lambda i, j: (0, 0)),
                  pl.BlockSpec((1, d, tn), lambda i, j: (i, 0, j)),
                  pl.BlockSpec((1, 1, tn), lambda i, j: (i, 0, j))],
        out_specs=pl.BlockSpec((1, 8, tn), lambda i, j: (i, 0, j)),
        out_shape=jax.ShapeDtypeStruct((depth, 8, n), F32),
        compiler_params=_cparams("parallel", "parallel"),
        name="ada_mod",
    )(cp, ada_w, ada_b.reshape(depth, 1, n))


def post_norm_mix(x, y, gate, sc, sh, lng, lnb, *, seq):
    d = x.shape[1]

    def fn(xv, yv, g, s, h, lg, lb):
        xn = _layer_norm(DEEPNORM_ALPHA * xv + (1.0 + g) * yv, lg, lb)
        return xn, xn * (1.0 + s) + h

    return rowwise(fn, [x, y], [gate, sc, sh], [lng, lnb], [(d, F32), (d, BF16)], seq=seq, tm=256, name="post_norm_mix")


def post_norm_moe(x, y0, y1, gates, gate, lng, lnb, *, seq):
    d = x.shape[1]

    def fn(xv, a, b, gt, g, lg, lb):
        y = a * gt[:, 0:1] + b * gt[:, 1:2]
        return (_layer_norm(DEEPNORM_ALPHA * xv + (1.0 + g) * y, lg, lb),)

    return rowwise(fn, [x, y0, y1, gates], [gate], [lng, lnb], [(d, F32)], seq=seq, tm=256, name="post_norm_moe")[0]


def _router_fn(xv, s, h, rw, rb):
    hf = xv * (1.0 + s) + h
    logits = jnp.dot(hf, rw, preferred_element_type=F32, precision=lax.Precision.HIGHEST)
    scores = jax.nn.sigmoid(logits)
    biased = scores + rb
    lane = lax.broadcasted_iota(jnp.int32, biased.shape, 1)
    neg = jnp.float32(-jnp.inf)
    big = jnp.int32(1 << 20)
    best = bi1 = bi2 = None
    for g in range(N_GROUPS):
        ing = (lane >= g * EXPERTS_PER_GROUP) & (lane < (g + 1) * EXPERTS_PER_GROUP)
        v = jnp.where(ing, biased, neg)
        m1 = jnp.max(v, -1, keepdims=True)
        i1 = jnp.min(jnp.where(v == m1, lane, big), -1, keepdims=True)
        v2 = jnp.where(lane == i1, neg, v)
        m2 = jnp.max(v2, -1, keepdims=True)
        i2 = jnp.min(jnp.where(v2 == m2, lane, big), -1, keepdims=True)
        gs = m1 + m2
        if g == 0:
            best, bi1, bi2 = gs, i1, i2
        else:
            better = gs > best
            best = jnp.where(better, gs, best)
            bi1 = jnp.where(better, i1, bi1)
            bi2 = jnp.where(better, i2, bi2)
    g1 = jnp.sum(jnp.where(lane == bi1, scores, 0.0), -1, keepdims=True)
    g2 = jnp.sum(jnp.where(lane == bi2, scores, 0.0), -1, keepdims=True)
    tot = g1 + g2
    two = lax.broadcasted_iota(jnp.int32, (biased.shape[0], TOP_K), 1)
    return jnp.where(two == 0, bi1, bi2), jnp.where(two == 0, g1 / tot, g2 / tot)


def _expert_body(be_ref, nb_ref, x_ref, wg_ref, wu_ref, wd_ref, o_ref):
    del be_ref

    @pl.when(pl.program_id(0) < nb_ref[0])
    def _():
        x = x_ref[...]
        g = jnp.dot(x, wg_ref[0], preferred_element_type=F32)
        u = jnp.dot(x, wu_ref[0], preferred_element_type=F32)
        hid = (g * jax.nn.sigmoid(g) * u).astype(BF16)
        o_ref[...] = jnp.dot(hid, wd_ref[0], preferred_element_type=F32)

    @pl.when(pl.program_id(0) >= nb_ref[0])
    def _():
        o_ref[...] = jnp.zeros_like(o_ref)


def moe_ffn(x, hbf, sc, sh, router_w, router_bias, wg, wu, wd, *, seq):
    n, d = x.shape
    e = router_w.shape[1]
    idx, gates = rowwise(_router_fn, [x], [sc, sh], [router_w, router_bias.reshape(1, e)],
                         [(TOP_K, jnp.int32), (TOP_K, F32)], seq=seq, tm=256, name="router")
    bm = MOE_ROWS
    n_slots = n * TOP_K
    flat_e = idx.reshape(-1)
    onehot = (flat_e[:, None] == jnp.arange(e, dtype=jnp.int32)[None, :]).astype(jnp.int32)
    csum = jnp.cumsum(onehot, axis=0)
    pos = jnp.sum(onehot * csum, axis=1) - 1
    counts = csum[-1]
    padded = (counts + bm - 1) // bm * bm
    pend = jnp.cumsum(padded)
    pstart = pend - padded
    dest = pstart[flat_e] + pos
    n_blocks = n_slots // bm + e
    n_pad = n_blocks * bm
    slot_tok = jnp.zeros((n_pad,), jnp.int32).at[dest].set(jnp.arange(n_slots, dtype=jnp.int32) // TOP_K)
    block_expert = jnp.minimum(
        jnp.searchsorted(pend, jnp.arange(n_blocks, dtype=jnp.int32) * bm, side="right"), e - 1).astype(jnp.int32)
    used_blocks = (pend[-1] // bm).astype(jnp.int32).reshape(1)
    xs = jnp.take(hbf, slot_tok, axis=0)
    de = wg.shape[2]
    yb = pl.pallas_call(
        _expert_body,
        grid_spec=pltpu.PrefetchScalarGridSpec(
            num_scalar_prefetch=2,
            grid=(n_blocks,),
            in_specs=[pl.BlockSpec((bm, d), lambda i, be, nb: (i, 0)),
                      pl.BlockSpec((1, d, de), lambda i, be, nb: (be[i], 0, 0)),
                      pl.BlockSpec((1, d, de), lambda i, be, nb: (be[i], 0, 0)),
                      pl.BlockSpec((1, de, d), lambda i, be, nb: (be[i], 0, 0))],
            out_specs=pl.BlockSpec((bm, d), lambda i, be, nb: (i, 0)),
        ),
        out_shape=jax.ShapeDtypeStruct((n_pad, d), F32),
        compiler_params=_cparams("arbitrary"),
        name="moe_experts",
    )(block_expert, used_blocks, xs, wg, wu, wd)
    dest2 = dest.reshape(n, TOP_K)
    return jnp.take(yb, dest2[:, 0], axis=0), jnp.take(yb, dest2[:, 1], axis=0), gates


def _rwkv_prep_body(x_ref, xp_ref, sc_ref, sh_ref, mu_ref, *outs, tpb):
    sc = 1.0 + sc_ref[0]
    sh = sh_ref[0]
    h = x_ref[...] * sc + sh
    prev_row = xp_ref[7:8, :] * sc + sh
    prev_row = jnp.where(pl.program_id(0) % tpb == 0, 0.0, prev_row)
    rowid = lax.broadcasted_iota(jnp.int32, h.shape, 0)
    hprev = jnp.where(rowid == 0, prev_row, pltpu.roll(h, 1, 0))
    dx = hprev - h
    for m, o in enumerate(outs):
        o[...] = (h + dx * mu_ref[m:m + 1, :]).astype(o.dtype)


def rwkv_prep(x, sc, sh, mu, *, seq):
    n, d = x.shape
    tm = min(256, seq)
    tpb = seq // tm
    return pl.pallas_call(
        functools.partial(_rwkv_prep_body, tpb=tpb),
        grid=(n // tm,),
        in_specs=[pl.BlockSpec((tm, d), lambda i: (i, 0)),
                  pl.BlockSpec((8, d), lambda i: (jnp.maximum(i * (tm // 8) - 1, 0), 0)),
                  pl.BlockSpec((1, 1, d), lambda i: (i // tpb, 0, 0)),
                  pl.BlockSpec((1, 1, d), lambda i: (i // tpb, 0, 0)),
                  pl.BlockSpec((6, d), lambda i: (0, 0))],
        out_specs=[pl.BlockSpec((tm, d), lambda i: (i, 0))] * 6,
        out_shape=[jax.ShapeDtypeStruct((n, d), BF16)] * 6,
        compiler_params=_cparams("parallel"),
        name="rwkv_prep",
    )(x, x, sc, sh, mu)


def _softplus(z):
    return jnp.maximum(z, 0.0) + jnp.log(1.0 + jnp.exp(-jnp.abs(z)))


def _rwkv_gates_fn(k0, lw, al, *rest, has_vres):
    if has_vres:
        v, vfirst, vl, prm, ones_bd = rest
    else:
        prm, ones_bd = rest
    w0, a0, k_k, k_a = prm[0:1], prm[1:2], prm[2:3], prm[3:4]
    log_w = -_softplus(-(w0 + lw)) - 0.5
    logdecay = -jnp.exp(log_w)
    a = jax.nn.sigmoid(a0 + al)
    kk = k0 * k_k
    nrm = jnp.maximum(jnp.sqrt(_seg_sum(kk * kk, ones_bd)), 1e-12)
    kk = kk / nrm
    k = k0 * (1.0 + (a - 1.0) * k_a)
    res = [logdecay, k, kk, kk * a]
    if has_vres:
        res.append(v + (vfirst - v) * jax.nn.sigmoid(prm[4:5] + vl))
    return res


def _wkv_body(r_ref, lw_ref, k_ref, v_ref, kk_ref, b_ref, y_ref, s_ref, *, chunk, heads):
    hd = RWKV_HEAD
    rows = r_ref.shape[0]

    @pl.when(pl.program_id(2) == 0)
    def _():
        s_ref[...] = jnp.zeros_like(s_ref)

    ti = lax.broadcasted_iota(jnp.int32, (chunk, chunk), 0)
    si = lax.broadcasted_iota(jnp.int32, (chunk, chunk), 1)
    incl = ti >= si
    strict = ti > si
    tri = incl.astype(BF16)
    eye = (ti == si).astype(F32)

    def step(c, carry):
        sl = pl.ds(pl.multiple_of(c * chunk, chunk), chunk)
        lw = lw_ref[sl, :]
        cs = _split_dot_left(tri, lw)
        ctot = cs[chunk - 1:chunk, :]
        g_inv = jnp.exp(-cs)
        rg = r_ref[sl, :] * jnp.exp(cs)
        kk = kk_ref[sl, :]
        ag = -kk * jnp.exp(cs - lw)
        kv = k_ref[sl, :]
        bv = b_ref[sl, :]
        kg = kv * g_inv
        bg = bv * g_inv
        g_end = jnp.exp(ctot - cs)
        kend = kv * g_end
        bend = bv * g_end
        g_tot = jnp.exp(ctot)
        vv = v_ref[sl, :]
        ys = []
        for h in range(heads):
            hs = slice(h * hd, (h + 1) * hd)
            s_old = s_ref[h]
            ar = jnp.concatenate([ag[:, hs], rg[:, hs]], 0).astype(BF16)
            bk = jnp.concatenate([bg[:, hs], kg[:, hs]], 0).astype(BF16)
            m = lax.dot_general(ar, bk, (((1,), (1,)), ((), ())), preferred_element_type=F32)
            a_ab = jnp.where(strict, m[:chunk, :chunk], 0.0)
            a_ak = jnp.where(strict, m[:chunk, chunk:], 0.0)
            a_rb = jnp.where(incl, m[chunk:, :chunk], 0.0)
            a_rk = jnp.where(incl, m[chunk:, chunk:], 0.0)
            ars = lax.dot_general(ar, s_old.astype(BF16), (((1,), (1,)), ((), ())), preferred_element_type=F32)
            vh = vv[:, hs]
            vb = vh.astype(BF16)
            rhs = ars[:chunk] + jnp.dot(a_ak.astype(BF16), vb, preferred_element_type=F32)
            inv = eye + a_ab
            apow = a_ab.astype(BF16)
            for _ in range(int(math.log2(chunk)) - 1):
                ap = jnp.dot(apow, apow, preferred_element_type=F32)
                apow = ap.astype(BF16)
                inv = inv + jnp.dot(inv.astype(BF16), apow, preferred_element_type=F32)
            u = jnp.dot(inv.astype(BF16), rhs.astype(BF16), preferred_element_type=F32)
            uv = jnp.concatenate([u.astype(BF16), vb], 0)
            y = ars[chunk:] + jnp.dot(jnp.concatenate([a_rb, a_rk], 1).astype(BF16), uv, preferred_element_type=F32)
            ys.append(y)
            ends = jnp.concatenate([bend[:, hs], kend[:, hs]], 0).astype(BF16)
            s_ref[h] = s_old * g_tot[:, hs] + lax.dot_general(
                uv, ends, (((0,), (0,)), ((), ())), preferred_element_type=F32)
        y_ref[sl, :] = jnp.concatenate(ys, -1)
        return carry

    lax.fori_loop(0, rows // chunk, step, 0)


def _split_dot_left(m, x):
    xh = x.astype(BF16)
    xl = (x - xh.astype(F32)).astype(BF16)
    return jnp.dot(m, xh, preferred_element_type=F32) + jnp.dot(m, xl, preferred_element_type=F32)


def wkv7(r, logdecay, k, v, kk, b, *, batch, seq):
    n, d = r.shape
    chunk = min(WKV_CHUNK, seq)
    rows = min(WKV_ROWS_PER_STEP, seq)
    heads = WKV_HEADS_PER_STEP
    width = heads * RWKV_HEAD
    spb = seq // rows
    spec = pl.BlockSpec((rows, width), lambda bi, hi, ti: (bi * spb + ti, hi))
    return pl.pallas_call(
        functools.partial(_wkv_body, chunk=chunk, heads=heads),
        grid=(batch, d // width, spb),
        in_specs=[spec] * 6,
        out_specs=spec,
        out_shape=jax.ShapeDtypeStruct((n, d), F32),
        scratch_shapes=[pltpu.VMEM((heads, RWKV_HEAD, RWKV_HEAD), F32)],
        compiler_params=_cparams("parallel", "parallel", "arbitrary"),
        name="wkv7",
    )(r, logdecay, k, v, kk, b)


def _rwkv_out_fn(y, r, k, v, g, prm, ones_bd):
    r_k, gn_g, gn_b = prm[0:1], prm[1:2], prm[2:3]
    inv_n = 1.0 / RWKV_HEAD
    m_y = _seg_sum(y, ones_bd) * inv_n
    yc = y - m_y
    v_y = _seg_sum(yc * yc, ones_bd) * inv_n
    yn = yc * lax.rsqrt(v_y + RWKV_GN_EPS) * gn_g + gn_b
    bonus = _seg_sum(r * k * r_k, ones_bd) * v
    return ((yn + bonus) * g,)


def rwkv_layer(x, sc, sh, v_first, p, *, batch, seq):
    n, d = x.shape
    xr, xw, xk, xv, xa, xg = rwkv_prep(x, sc, sh, p["mu"], seq=seq)
    r = mm(xr, p["w_r"])
    k0 = mm(xk, p["w_k"])
    v = mm(xv, p["w_v"])
    lw = mm(mm(xw, p["w1"], out_dtype=BF16, act="tanh"), p["w2"])
    al = mm(mm(xa, p["a1"], out_dtype=BF16), p["a2"])
    g = mm(mm(xg, p["g1"], out_dtype=BF16, act="sigmoid"), p["g2"])
    ones_bd = _head_ones()
    has_vres = p["v0"] is not None
    outs = [(d, F32)] * (5 if has_vres else 4)
    if has_vres:
        vl = mm(mm(xv, p["v1"], out_dtype=BF16), p["v2"])
        prm = jnp.stack([p["w0"], p["a0"], p["k_k"], p["k_a"], p["v0"]])
        rows = [k0, lw, al, v, v_first, vl]
    else:
        prm = jnp.stack([p["w0"], p["a0"], p["k_k"], p["k_a"]])
        rows = [k0, lw, al]
    res = rowwise(functools.partial(_rwkv_gates_fn, has_vres=has_vres), rows, [], [prm, ones_bd], outs,
                  seq=seq, tm=128, name="rwkv_gates")
    logdecay, k, kk, b = res[:4]
    if has_vres:
        v = res[4]
    else:
        v_first = v
    y = wkv7(r, logdecay, k, v, kk, b, batch=batch, seq=seq)
    prm2 = jnp.stack([p["r_k"], p["gn_g"], p["gn_b"]])
    yg = rowwise(_rwkv_out_fn, [y, r, k, v, g], [], [prm2, ones_bd], [(d, BF16)], seq=seq, tm=128, name="rwkv_out")[0]
    return mm(yg, p["w_o"]), v_first


def _layer_norm_j(x, g, b, eps=LN_EPS):
    mu = jnp.mean(x, -1, keepdims=True)
    var = jnp.mean(jnp.square(x - mu), -1, keepdims=True)
    return (x - mu) * lax.rsqrt(var + eps) * g + b


def _rms_norm_j(x, g, eps=1e-6):
    return x * lax.rsqrt(jnp.mean(x * x, -1, keepdims=True) + eps) * g


def _partial_rope_j(x, positions, rot_dim):
    half = rot_dim // 2
    inv_freq = ROPE_THETA ** (-jnp.arange(half, dtype=F32) * (2.0 / rot_dim))
    ang = positions.astype(F32)[:, :, None, None] * inv_freq
    cos, sin = jnp.cos(ang), jnp.sin(ang)
    xr = x[..., :rot_dim]
    x1, x2 = xr[..., :half], xr[..., half:]
    rot = jnp.concatenate([x1 * cos - x2 * sin, x2 * cos + x1 * sin], axis=-1)
    return jnp.concatenate([rot, x[..., rot_dim:]], axis=-1)


def dsa_layer(hbf, positions, p, *, batch, seq):
    B, T = batch, seq
    H, Hi = DSA_HEADS, IDX_HEADS
    topk = min(TOPK_MAX, T // 4)
    q = mm(hbf, p["w_q"]).reshape(B, T, H, DSA_HEAD_DIM)
    c_kv = mm(hbf, p["w_ckv"]).reshape(B, T, DSA_KV_RANK)
    q_idx = mm(hbf, p["w_qidx"]).reshape(B, T, Hi, IDX_DIM)
    small = mm(hbf, p["w_small"]).reshape(B, T, LANES)
    k_rope = small[..., :DSA_ROPE_DIM]
    k_idx = small[..., DSA_ROPE_DIM:DSA_ROPE_DIM + IDX_DIM]
    w_idx = small[..., DSA_ROPE_DIM + IDX_DIM:DSA_ROPE_DIM + IDX_DIM + Hi]
    q = _partial_rope_j(q, positions, DSA_ROPE_DIM)
    q_rope, q_nope = q[..., :DSA_ROPE_DIM], q[..., DSA_ROPE_DIM:]
    c_kv = _rms_norm_j(c_kv, p["kv_norm"])
    k_rope = _partial_rope_j(k_rope[:, :, None, :], positions, DSA_ROPE_DIM)[:, :, 0]
    q_idx = _partial_rope_j(q_idx, positions, IDX_ROPE_DIM)
    k_idx = _partial_rope_j(_layer_norm_j(k_idx, p["idx_ln_g"], p["idx_ln_b"])[:, :, None, :], positions,
                            IDX_ROPE_DIM)[:, :, 0]
    w_idx = w_idx * (Hi ** -0.5 * IDX_DIM ** -0.5)
    key_pos = jnp.arange(T)
    gather = jax.vmap(lambda src, idx: src[idx])
    scale = DSA_HEAD_DIM ** -0.5
    w_uk, w_uv = p["w_uk"], p["w_uv"]

    def block(j):
        s0 = j * Q_BLOCK
        sl = lambda z: lax.dynamic_slice_in_dim(z, s0, Q_BLOCK, axis=1)
        qi, wi, qr, qn = sl(q_idx), sl(w_idx), sl(q_rope), sl(q_nope)
        t = s0 + jnp.arange(Q_BLOCK)
        causal = key_pos[None, :] <= t[:, None]
        isc = jnp.einsum('bqh,bqhs->bqs', wi, jax.nn.relu(jnp.einsum('bqhd,bsd->bqhs', qi, k_idx)))
        isc = jnp.where(causal[None], isc, -jnp.inf)
        _, sel = lax.top_k(isc, topk)
        valid = sel <= t[None, :, None]
        c_sel = gather(c_kv, sel)
        kr_sel = gather(k_rope, sel)
        q_abs = jnp.einsum('bqhn,hnr->bqhr', qn, w_uk)
        s = jnp.einsum('bqhr,bqkr->bqhk', q_abs, c_sel) + jnp.einsum('bqhd,bqkd->bqhk', qr, kr_sel)
        s = jnp.where(valid[:, :, None, :], s * scale, -jnp.inf)
        pr = jax.nn.softmax(s, axis=-1)
        o_lat = jnp.einsum('bqhk,bqkr->bqhr', pr, c_sel)
        return jnp.einsum('bqhr,hrv->bqhv', o_lat, w_uv).reshape(B, Q_BLOCK, H * DSA_V_DIM)

    out = lax.map(block, jnp.arange(T // Q_BLOCK))
    out = jnp.moveaxis(out, 0, 1).reshape(B * T, H * DSA_V_DIM)
    return mm(out, p["w_o"])


def _complex_combine(e1, e2):
    a1r, a1i, b1r, b1i = e1
    a2r, a2i, b2r, b2i = e2
    return (a2r * a1r - a2i * a1i, a2r * a1i + a2i * a1r,
            a2r * b1r - a2i * b1i + b2r, a2r * b1i + a2i * b1r + b2i)


def s5_layer(h, p, *, batch, seq):
    B, T = batch, seq
    D = h.shape[-1]
    Hg, P = S5_GROUP, S5_STATE
    G = D // Hg
    n_steps, gps = G // S5_GROUPS_PER_STEP, S5_GROUPS_PER_STEP
    dt = jnp.exp(p["log_dt"])[:, None]
    lr, li = p["lam_re"], p["lam_im"]
    mag = jnp.exp(lr * dt)
    ab_re, ab_im = mag * jnp.cos(li * dt), mag * jnp.sin(li * dt)
    den = lr * lr + li * li
    nr, ni = ab_re - 1.0, ab_im
    coef_re = (nr * lr + ni * li) / den
    coef_im = (ni * lr - nr * li) / den
    br, bi = p["b_re"], p["b_im"]
    bb_re = coef_re[..., None] * br - coef_im[..., None] * bi
    bb_im = coef_re[..., None] * bi + coef_im[..., None] * br
    u = h.reshape(B, T, D)
    u_g = jnp.moveaxis(u.reshape(B, T, n_steps, gps, Hg), 2, 0)
    chunk = lambda z: z.reshape((n_steps, gps) + z.shape[1:])

    def group_block(args):
        u_c, a_r, a_i, bb_r, bb_i, cr, ci = args
        bu_r = jnp.einsum('btgh,gph->btgp', u_c, bb_r)
        bu_i = jnp.einsum('btgh,gph->btgp', u_c, bb_i)
        ar = jnp.broadcast_to(a_r, bu_r.shape)
        ai = jnp.broadcast_to(a_i, bu_r.shape)
        _, _, xr, xi = lax.associative_scan(_complex_combine, (ar, ai, bu_r, bu_i), axis=1)
        return jnp.einsum('btgp,ghp->btgh', xr, cr) - jnp.einsum('btgp,ghp->btgh', xi, ci)

    y = lax.map(group_block, (u_g, chunk(ab_re), chunk(ab_im), chunk(bb_re), chunk(bb_im),
                              chunk(p["c_re"]), chunk(p["c_im"])))
    y = jnp.moveaxis(y, 0, 2).reshape(B, T, D) + p["d"] * u
    z = jax.nn.gelu(y).reshape(B * T, D)
    zz = mm(z, p["w_glu"])
    return rowwise(lambda a: (a[:, :D] * jax.nn.sigmoid(a[:, D:]),), [zz], [], [], [(D, F32)], seq=seq, tm=256,
                   name="s5_glu")[0]


def _modulate(x, sc, sh, dtype, *, seq):
    d = x.shape[1]
    return rowwise(lambda xv, s, h: (xv * (1.0 + s) + h,), [x], [sc, sh], [], [(d, dtype)], seq=seq, tm=256,
                   name="modulate")[0]


def kernel(x, c, positions, ada_w, ada_b, ln_g, ln_b, router_w, router_bias, moe_w_gate, moe_w_up, moe_w_down,
           rwkv_mu, rwkv_w_rkv, rwkv_w_o, rwkv_w0, rwkv_w1, rwkv_w2, rwkv_a0, rwkv_a1, rwkv_a2, rwkv_g1, rwkv_g2,
           rwkv_k_k, rwkv_k_a, rwkv_r_k, rwkv_gn_g, rwkv_gn_b, rwkv_v0, rwkv_v1, rwkv_v2, dsa_w_in, dsa_kv_norm,
           dsa_w_uk, dsa_w_uv, dsa_idx_ln_g, dsa_idx_ln_b, dsa_w_o, s5_lam_re, s5_lam_im, s5_log_dt, s5_b_re,
           s5_b_im, s5_c_re, s5_c_im, s5_d, s5_w_glu):
    batch, seq, d = x.shape
    depth = ada_w.shape[0]
    n = batch * seq
    xf = x.reshape(n, d)
    mod = ada_mod(c, ada_w, ada_b)[:, :batch].reshape(depth, batch, 1, 6, d)
    v_first = None
    for i in range(depth):
        kind, j = i % N_MIXERS, i // N_MIXERS
        sh1, sc1, g1, sh2, sc2, g2 = (mod[i, :, :, m] for m in range(6))
        if kind == 0:
            p = dict(mu=rwkv_mu[j], w_r=rwkv_w_rkv[j, 0].astype(BF16), w_k=rwkv_w_rkv[j, 1].astype(BF16),
                     w_v=rwkv_w_rkv[j, 2].astype(BF16), w_o=rwkv_w_o[j].astype(BF16),
                     w0=rwkv_w0[j], w1=_wpad(rwkv_w1[j]), w2=_kpad(rwkv_w2[j]),
                     a0=rwkv_a0[j], a1=_wpad(rwkv_a1[j]), a2=_kpad(rwkv_a2[j]),
                     g1=_wpad(rwkv_g1[j]), g2=_kpad(rwkv_g2[j]),
                     k_k=rwkv_k_k[j], k_a=rwkv_k_a[j], r_k=rwkv_r_k[j], gn_g=rwkv_gn_g[j], gn_b=rwkv_gn_b[j],
                     v0=None)
            if j > 0:
                p.update(v0=rwkv_v0[j - 1], v1=_wpad(rwkv_v1[j - 1]), v2=_kpad(rwkv_v2[j - 1]))
            y, v_first = rwkv_layer(xf, sc1, sh1, v_first, p, batch=batch, seq=seq)
        elif kind == 1:
            w_in = dsa_w_in[j]
            hq = DSA_HEADS * DSA_HEAD_DIM
            o1 = hq + DSA_KV_RANK
            o2 = o1 + DSA_ROPE_DIM
            o3 = o2 + IDX_HEADS * IDX_DIM
            p = dict(w_q=w_in[:, :hq].astype(BF16), w_ckv=w_in[:, hq:o1].astype(BF16),
                     w_qidx=w_in[:, o2:o3].astype(BF16),
                     w_small=_wpad(jnp.concatenate([w_in[:, o1:o2], w_in[:, o3:]], axis=1)),
                     kv_norm=dsa_kv_norm[j], w_uk=dsa_w_uk[j], w_uv=dsa_w_uv[j],
                     idx_ln_g=dsa_idx_ln_g[j], idx_ln_b=dsa_idx_ln_b[j], w_o=dsa_w_o[j].astype(BF16))
            hbf = _modulate(xf, sc1, sh1, BF16, seq=seq)
            y = dsa_layer(hbf, positions, p, batch=batch, seq=seq)
        else:
            p = dict(lam_re=s5_lam_re[j], lam_im=s5_lam_im[j], log_dt=s5_log_dt[j], b_re=s5_b_re[j],
                     b_im=s5_b_im[j], c_re=s5_c_re[j], c_im=s5_c_im[j], d=s5_d[j], w_glu=s5_w_glu[j].astype(BF16))
            hmix = _modulate(xf, sc1, sh1, F32, seq=seq)
            y = s5_layer(hmix, p, batch=batch, seq=seq)
        xf, hbf = post_norm_mix(xf, y, g1, sc2, sh2, ln_g[i, 0:1], ln_b[i, 0:1], seq=seq)
        y0, y1, gates = moe_ffn(xf, hbf, sc2, sh2, router_w, router_bias, moe_w_gate[i].astype(BF16),
                                moe_w_up[i].astype(BF16), moe_w_down[i].astype(BF16), seq=seq)
        xf = post_norm_moe(xf, y0, y1, gates, g2, ln_g[i, 1:2], ln_b[i, 1:2], seq=seq)
    return xf.reshape(batch, seq, d)
```

```python
import functools
import math

import numpy as np
import jax
import jax.numpy as jnp
from jax import lax
from jax.experimental import pallas as pl
from jax.experimental.pallas import tpu as pltpu

F32, BF16 = jnp.float32, jnp.bfloat16

DEPTH = 4
N_MIXERS = 3
RWKV_HEAD = 64
RWKV_GN_EPS = 64e-5
DSA_HEADS = 16
DSA_HEAD_DIM = 128
DSA_ROPE_DIM = 32
DSA_NOPE_DIM = 96
DSA_V_DIM = 128
DSA_KV_RANK = 512
IDX_HEADS = 16
IDX_DIM = 64
IDX_ROPE_DIM = 16
TOPK_MAX = 256
Q_BLOCK = 128
S5_GROUP = 16
S5_STATE = 64
S5_CHUNK = 16
S5_PAIRS_PER_STEP = 4
N_EXPERTS = 32
N_GROUPS = 4
EXPERTS_PER_GROUP = 8
TOP_K = 2
ROPE_THETA = 500000.0
LN_EPS = 1e-5
DEEPNORM_ALPHA = (2 * DEPTH) ** 0.25

LANES = 128
VMEM_LIMIT = 48 * 1024 * 1024
MOE_ROWS = 256
WKV_CHUNK = 64
WKV_HEADS_PER_STEP = 4
WKV_ROWS_PER_STEP = 512


def _cparams(*sem):
    return pltpu.CompilerParams(dimension_semantics=sem, vmem_limit_bytes=VMEM_LIMIT)


def _pick(n, cands):
    for c in cands:
        if n % c == 0:
            return c
    return n


def _mm_body(a_ref, w_ref, o_ref, *, act):
    acc = jnp.dot(a_ref[...].astype(BF16), w_ref[...], preferred_element_type=F32)
    if act == "tanh":
        acc = jnp.tanh(acc)
    elif act == "sigmoid":
        acc = jax.nn.sigmoid(acc)
    o_ref[...] = acc.astype(o_ref.dtype)


def mm(a, w, *, out_dtype=F32, act=None, tm=512):
    m, k = a.shape
    n = w.shape[1]
    tm = _pick(m, (tm, 256, 128, 64, 32, 16, 8))
    tn = _pick(n, (512, 256, 128))
    return pl.pallas_call(
        functools.partial(_mm_body, act=act),
        grid=(m // tm, n // tn),
        in_specs=[pl.BlockSpec((tm, k), lambda i, j: (i, 0)), pl.BlockSpec((k, tn), lambda i, j: (0, j))],
        out_specs=pl.BlockSpec((tm, tn), lambda i, j: (i, j)),
        out_shape=jax.ShapeDtypeStruct((m, n), out_dtype),
        compiler_params=_cparams("parallel", "parallel"),
        name="mm",
    )(a, w)


def _wpad(w):
    n = w.shape[1]
    npad = -(-n // LANES) * LANES
    w = w.astype(BF16)
    return w if npad == n else jnp.pad(w, ((0, 0), (0, npad - n)))


def _kpad(w):
    k = w.shape[0]
    kpad = -(-k // LANES) * LANES
    w = w.astype(BF16)
    return w if kpad == k else jnp.pad(w, ((0, kpad - k), (0, 0)))


def rowwise(fn, rows, perbatch, consts, outs, *, seq, tm, name):
    n = rows[0].shape[0]
    tm = min(tm, seq)
    tpb = seq // tm
    nr, nb, nc = len(rows), len(perbatch), len(consts)

    def body(*refs):
        vals = [r[...] for r in refs[:nr]]
        vals += [r[0] for r in refs[nr:nr + nb]]
        vals += [r[...] for r in refs[nr + nb:nr + nb + nc]]
        res = fn(*vals)
        for o, v in zip(refs[nr + nb + nc:], res):
            o[...] = v.astype(o.dtype)

    in_specs = [pl.BlockSpec((tm, r.shape[1]), lambda i: (i, 0)) for r in rows]
    in_specs += [pl.BlockSpec((1, 1, p.shape[-1]), lambda i: (i // tpb, 0, 0)) for p in perbatch]
    in_specs += [pl.BlockSpec(c.shape, lambda i: (0, 0)) for c in consts]
    res = pl.pallas_call(
        body,
        grid=(n // tm,),
        in_specs=in_specs,
        out_specs=[pl.BlockSpec((tm, w), lambda i: (i, 0)) for w, _ in outs],
        out_shape=[jax.ShapeDtypeStruct((n, w), dt) for w, dt in outs],
        compiler_params=_cparams("parallel"),
        name=name,
    )(*rows, *perbatch, *consts)
    return res


def _layer_norm(z, g, b):
    mu = jnp.mean(z, -1, keepdims=True)
    zc = z - mu
    var = jnp.mean(zc * zc, -1, keepdims=True)
    return zc * lax.rsqrt(var + LN_EPS) * g + b


def _split_dot(x, m):
    xh = x.astype(BF16)
    xl = (x - xh.astype(F32)).astype(BF16)
    return jnp.dot(xh, m, preferred_element_type=F32) + jnp.dot(xl, m, preferred_element_type=F32)


def _seg_sum(x, ones_bd):
    parts = [_split_dot(x[:, j:j + LANES], ones_bd) for j in range(0, x.shape[1], LANES)]
    return jnp.concatenate(parts, -1)


def _head_ones():
    i = np.arange(LANES)
    return jnp.asarray((i[:, None] // RWKV_HEAD) == (i[None, :] // RWKV_HEAD), BF16)


def _ada_body(c_ref, w_ref, b_ref, o_ref):
    acc = jnp.dot(c_ref[...], w_ref[0], preferred_element_type=F32, precision=lax.Precision.HIGHEST)
    o_ref[0] = acc + b_ref[0]


def ada_mod(c, ada_w, ada_b):
    depth, d, n = ada_w.shape
    b = c.shape[0]
    cp = jnp.pad(c, ((0, 8 - b), (0, 0)))
    tn = 1024
    return pl.pallas_call(
        _ada_body,
        grid=(depth, n // tn),
        in_specs=[pl.BlockSpec((8, d), lambda i, j: (0, 0)),
                  pl.BlockSpec((1, d, tn), lambda i, j: (i, 0, j)),
                  pl.BlockSpec((1, 1, tn), lambda i, j: (i, 0, j))],
        out_specs=pl.BlockSpec((1, 8, tn), lambda i, j: (i, 0, j)),
        out_shape=jax.ShapeDtypeStruct((depth, 8, n), F32),
        compiler_params=_cparams("parallel", "parallel"),
        name="ada_mod",
    )(cp, ada_w, ada_b.reshape(depth, 1, n))


def post_norm_mix(x, y, gate, sc, sh, lng, lnb, *, seq):
    d = x.shape[1]

    def fn(xv, yv, g, s, h, lg, lb):
        xn = _layer_norm(DEEPNORM_ALPHA * xv + (1.0 + g) * yv, lg, lb)
        return xn, xn * (1.0 + s) + h

    return rowwise(fn, [x, y], [gate, sc, sh], [lng, lnb], [(d, F32), (d, BF16)], seq=seq, tm=256, name="post_norm_mix")


def post_norm_moe(x, y0, y1, gates, gate, lng, lnb, *, seq):
    d = x.shape[1]

    def fn(xv, a, b, gt, g, lg, lb):
        y = a * gt[:, 0:1] + b * gt[:, 1:2]
        return (_layer_norm(DEEPNORM_ALPHA * xv + (1.0 + g) * y, lg, lb),)

    return rowwise(fn, [x, y0, y1, gates], [gate], [lng, lnb], [(d, F32)], seq=seq, tm=256, name="post_norm_moe")[0]


def _router_fn(xv, s, h, rw, rb):
    hf = xv * (1.0 + s) + h
    logits = jnp.dot(hf, rw, preferred_element_type=F32, precision=lax.Precision.HIGHEST)
    scores = jax.nn.sigmoid(logits)
    biased = scores + rb
    lane = lax.broadcasted_iota(jnp.int32, biased.shape, 1)
    neg = jnp.float32(-jnp.inf)
    big = jnp.int32(1 << 20)
    best = bi1 = bi2 = None
    for g in range(N_GROUPS):
        ing = (lane >= g * EXPERTS_PER_GROUP) & (lane < (g + 1) * EXPERTS_PER_GROUP)
        v = jnp.where(ing, biased, neg)
        m1 = jnp.max(v, -1, keepdims=True)
        i1 = jnp.min(jnp.where(v == m1, lane, big), -1, keepdims=True)
        v2 = jnp.where(lane == i1, neg, v)
        m2 = jnp.max(v2, -1, keepdims=True)
        i2 = jnp.min(jnp.where(v2 == m2, lane, big), -1, keepdims=True)
        gs = m1 + m2
        if g == 0:
            best, bi1, bi2 = gs, i1, i2
        else:
            better = gs > best
            best = jnp.where(better, gs, best)
            bi1 = jnp.where(better, i1, bi1)
            bi2 = jnp.where(better, i2, bi2)
    g1 = jnp.sum(jnp.where(lane == bi1, scores, 0.0), -1, keepdims=True)
    g2 = jnp.sum(jnp.where(lane == bi2, scores, 0.0), -1, keepdims=True)
    tot = g1 + g2
    two = lax.broadcasted_iota(jnp.int32, (biased.shape[0], TOP_K), 1)
    return jnp.where(two == 0, bi1, bi2), jnp.where(two == 0, g1 / tot, g2 / tot)


def _expert_body(be_ref, nb_ref, x_ref, wg_ref, wu_ref, wd_ref, o_ref):
    del be_ref

    @pl.when(pl.program_id(0) < nb_ref[0])
    def _():
        x = x_ref[...]
        g = jnp.dot(x, wg_ref[0], preferred_element_type=F32)
        u = jnp.dot(x, wu_ref[0], preferred_element_type=F32)
        hid = (g * jax.nn.sigmoid(g) * u).astype(BF16)
        o_ref[...] = jnp.dot(hid, wd_ref[0], preferred_element_type=F32)

    @pl.when(pl.program_id(0) >= nb_ref[0])
    def _():
        o_ref[...] = jnp.zeros_like(o_ref)


def moe_ffn(x, hbf, sc, sh, router_w, router_bias, wg, wu, wd, *, seq):
    n, d = x.shape
    e = router_w.shape[1]
    idx, gates = rowwise(_router_fn, [x], [sc, sh], [router_w, router_bias.reshape(1, e)],
                         [(TOP_K, jnp.int32), (TOP_K, F32)], seq=seq, tm=256, name="router")
    bm = MOE_ROWS
    n_slots = n * TOP_K
    flat_e = idx.reshape(-1)
    onehot = (flat_e[:, None] == jnp.arange(e, dtype=jnp.int32)[None, :]).astype(jnp.int32)
    csum = jnp.cumsum(onehot, axis=0)
    pos = jnp.sum(onehot * csum, axis=1) - 1
    counts = csum[-1]
    padded = (counts + bm - 1) // bm * bm
    pend = jnp.cumsum(padded)
    pstart = pend - padded
    dest = pstart[flat_e] + pos
    n_blocks = n_slots // bm + e
    n_pad = n_blocks * bm
    slot_tok = jnp.zeros((n_pad,), jnp.int32).at[dest].set(jnp.arange(n_slots, dtype=jnp.int32) // TOP_K)
    block_expert = jnp.minimum(
        jnp.searchsorted(pend, jnp.arange(n_blocks, dtype=jnp.int32) * bm, side="right"), e - 1).astype(jnp.int32)
    used_blocks = (pend[-1] // bm).astype(jnp.int32).reshape(1)
    xs = jnp.take(hbf, slot_tok, axis=0)
    de = wg.shape[2]
    yb = pl.pallas_call(
        _expert_body,
        grid_spec=pltpu.PrefetchScalarGridSpec(
            num_scalar_prefetch=2,
            grid=(n_blocks,),
            in_specs=[pl.BlockSpec((bm, d), lambda i, be, nb: (i, 0)),
                      pl.BlockSpec((1, d, de), lambda i, be, nb: (be[i], 0, 0)),
                      pl.BlockSpec((1, d, de), lambda i, be, nb: (be[i], 0, 0)),
                      pl.BlockSpec((1, de, d), lambda i, be, nb: (be[i], 0, 0))],
            out_specs=pl.BlockSpec((bm, d), lambda i, be, nb: (i, 0)),
        ),
        out_shape=jax.ShapeDtypeStruct((n_pad, d), F32),
        compiler_params=_cparams("arbitrary"),
        name="moe_experts",
    )(block_expert, used_blocks, xs, wg, wu, wd)
    dest2 = dest.reshape(n, TOP_K)
    return jnp.take(yb, dest2[:, 0], axis=0), jnp.take(yb, dest2[:, 1], axis=0), gates


def _rwkv_prep_body(x_ref, xp_ref, sc_ref, sh_ref, mu_ref, *outs, tpb):
    sc = 1.0 + sc_ref[0]
    sh = sh_ref[0]
    h = x_ref[...] * sc + sh
    prev_row = xp_ref[7:8, :] * sc + sh
    prev_row = jnp.where(pl.program_id(0) % tpb == 0, 0.0, prev_row)
    rowid = lax.broadcasted_iota(jnp.int32, h.shape, 0)
    hprev = jnp.where(rowid == 0, prev_row, pltpu.roll(h, 1, 0))
    dx = hprev - h
    for m, o in enumerate(outs):
        o[...] = (h + dx * mu_ref[m:m + 1, :]).astype(o.dtype)


def rwkv_prep(x, sc, sh, mu, *, seq):
    n, d = x.shape
    tm = min(256, seq)
    tpb = seq // tm
    return pl.pallas_call(
        functools.partial(_rwkv_prep_body, tpb=tpb),
        grid=(n // tm,),
        in_specs=[pl.BlockSpec((tm, d), lambda i: (i, 0)),
                  pl.BlockSpec((8, d), lambda i: (jnp.maximum(i * (tm // 8) - 1, 0), 0)),
                  pl.BlockSpec((1, 1, d), lambda i: (i // tpb, 0, 0)),
                  pl.BlockSpec((1, 1, d), lambda i: (i // tpb, 0, 0)),
                  pl.BlockSpec((6, d), lambda i: (0, 0))],
        out_specs=[pl.BlockSpec((tm, d), lambda i: (i, 0))] * 6,
        out_shape=[jax.ShapeDtypeStruct((n, d), BF16)] * 6,
        compiler_params=_cparams("parallel"),
        name="rwkv_prep",
    )(x, x, sc, sh, mu)


def _softplus(z):
    return jnp.maximum(z, 0.0) + jnp.log(1.0 + jnp.exp(-jnp.abs(z)))


def _rwkv_gates_fn(k0, lw, al, *rest, has_vres):
    if has_vres:
        v, vfirst, vl, prm, ones_bd = rest
    else:
        prm, ones_bd = rest
    w0, a0, k_k, k_a = prm[0:1], prm[1:2], prm[2:3], prm[3:4]
    log_w = -_softplus(-(w0 + lw)) - 0.5
    logdecay = -jnp.exp(log_w)
    a = jax.nn.sigmoid(a0 + al)
    kk = k0 * k_k
    nrm = jnp.maximum(jnp.sqrt(_seg_sum(kk * kk, ones_bd)), 1e-12)
    kk = kk / nrm
    k = k0 * (1.0 + (a - 1.0) * k_a)
    res = [logdecay, k, kk, kk * a]
    if has_vres:
        res.append(v + (vfirst - v) * jax.nn.sigmoid(prm[4:5] + vl))
    return res


def _wkv_body(r_ref, lw_ref, k_ref, v_ref, kk_ref, b_ref, y_ref, s_ref, *, chunk, heads):
    hd = RWKV_HEAD
    rows = r_ref.shape[0]

    @pl.when(pl.program_id(2) == 0)
    def _():
        s_ref[...] = jnp.zeros_like(s_ref)

    ti = lax.broadcasted_iota(jnp.int32, (chunk, chunk), 0)
    si = lax.broadcasted_iota(jnp.int32, (chunk, chunk), 1)
    incl = ti >= si
    strict = ti > si
    tri = incl.astype(BF16)
    eye = (ti == si).astype(F32)

    def step(c, carry):
        sl = pl.ds(pl.multiple_of(c * chunk, chunk), chunk)
        lw = lw_ref[sl, :]
        cs = _split_dot_left(tri, lw)
        ctot = cs[chunk - 1:chunk, :]
        g_inv = jnp.exp(-cs)
        rg = r_ref[sl, :] * jnp.exp(cs)
        kk = kk_ref[sl, :]
        ag = -kk * jnp.exp(cs - lw)
        kv = k_ref[sl, :]
        bv = b_ref[sl, :]
        kg = kv * g_inv
        bg = bv * g_inv
        g_end = jnp.exp(ctot - cs)
        kend = kv * g_end
        bend = bv * g_end
        g_tot = jnp.exp(ctot)
        vv = v_ref[sl, :]
        ys = []
        for h in range(heads):
            hs = slice(h * hd, (h + 1) * hd)
            s_old = s_ref[h]
            ar = jnp.concatenate([ag[:, hs], rg[:, hs]], 0).astype(BF16)
            bk = jnp.concatenate([bg[:, hs], kg[:, hs]], 0).astype(BF16)
            m = lax.dot_general(ar, bk, (((1,), (1,)), ((), ())), preferred_element_type=F32)
            a_ab = jnp.where(strict, m[:chunk, :chunk], 0.0)
            a_ak = jnp.where(strict, m[:chunk, chunk:], 0.0)
            a_rb = jnp.where(incl, m[chunk:, :chunk], 0.0)
            a_rk = jnp.where(incl, m[chunk:, chunk:], 0.0)
            ars = lax.dot_general(ar, s_old.astype(BF16), (((1,), (1,)), ((), ())), preferred_element_type=F32)
            vh = vv[:, hs]
            vb = vh.astype(BF16)
            rhs = ars[:chunk] + jnp.dot(a_ak.astype(BF16), vb, preferred_element_type=F32)
            inv = eye + a_ab
            apow = a_ab.astype(BF16)
            for _ in range(int(math.log2(chunk)) - 1):
                ap = jnp.dot(apow, apow, preferred_element_type=F32)
                apow = ap.astype(BF16)
                inv = inv + jnp.dot(inv.astype(BF16), apow, preferred_element_type=F32)
            u = jnp.dot(inv.astype(BF16), rhs.astype(BF16), preferred_element_type=F32)
            uv = jnp.concatenate([u.astype(BF16), vb], 0)
            y = ars[chunk:] + jnp.dot(jnp.concatenate([a_rb, a_rk], 1).astype(BF16), uv, preferred_element_type=F32)
            ys.append(y)
            ends = jnp.concatenate([bend[:, hs], kend[:, hs]], 0).astype(BF16)
            s_ref[h] = s_old * g_tot[:, hs] + lax.dot_general(
                uv, ends, (((0,), (0,)), ((), ())), preferred_element_type=F32)
        y_ref[sl, :] = jnp.concatenate(ys, -1)
        return carry

    lax.fori_loop(0, rows // chunk, step, 0)


def _split_dot_left(m, x):
    xh = x.astype(BF16)
    xl = (x - xh.astype(F32)).astype(BF16)
    return jnp.dot(m, xh, preferred_element_type=F32) + jnp.dot(m, xl, preferred_element_type=F32)


def wkv7(r, logdecay, k, v, kk, b, *, batch, seq):
    n, d = r.shape
    chunk = min(WKV_CHUNK, seq)
    rows = min(WKV_ROWS_PER_STEP, seq)
    heads = WKV_HEADS_PER_STEP
    width = heads * RWKV_HEAD
    spb = seq // rows
    spec = pl.BlockSpec((rows, width), lambda bi, hi, ti: (bi * spb + ti, hi))
    return pl.pallas_call(
        functools.partial(_wkv_body, chunk=chunk, heads=heads),
        grid=(batch, d // width, spb),
        in_specs=[spec] * 6,
        out_specs=spec,
        out_shape=jax.ShapeDtypeStruct((n, d), F32),
        scratch_shapes=[pltpu.VMEM((heads, RWKV_HEAD, RWKV_HEAD), F32)],
        compiler_params=_cparams("parallel", "parallel", "arbitrary"),
        name="wkv7",
    )(r, logdecay, k, v, kk, b)


def _rwkv_out_fn(y, r, k, v, g, prm, ones_bd):
    r_k, gn_g, gn_b = prm[0:1], prm[1:2], prm[2:3]
    inv_n = 1.0 / RWKV_HEAD
    m_y = _seg_sum(y, ones_bd) * inv_n
    yc = y - m_y
    v_y = _seg_sum(yc * yc, ones_bd) * inv_n
    yn = yc * lax.rsqrt(v_y + RWKV_GN_EPS) * gn_g + gn_b
    bonus = _seg_sum(r * k * r_k, ones_bd) * v
    return ((yn + bonus) * g,)


def rwkv_layer(x, sc, sh, v_first, p, *, batch, seq):
    n, d = x.shape
    xr, xw, xk, xv, xa, xg = rwkv_prep(x, sc, sh, p["mu"], seq=seq)
    r = mm(xr, p["w_r"])
    k0 = mm(xk, p["w_k"])
    v = mm(xv, p["w_v"])
    lw = mm(mm(xw, p["w1"], out_dtype=BF16, act="tanh"), p["w2"])
    al = mm(mm(xa, p["a1"], out_dtype=BF16), p["a2"])
    g = mm(mm(xg, p["g1"], out_dtype=BF16, act="sigmoid"), p["g2"])
    ones_bd = _head_ones()
    has_vres = p["v0"] is not None
    outs = [(d, F32)] * (5 if has_vres else 4)
    if has_vres:
        vl = mm(mm(xv, p["v1"], out_dtype=BF16), p["v2"])
        prm = jnp.stack([p["w0"], p["a0"], p["k_k"], p["k_a"], p["v0"]])
        rows = [k0, lw, al, v, v_first, vl]
    else:
        prm = jnp.stack([p["w0"], p["a0"], p["k_k"], p["k_a"]])
        rows = [k0, lw, al]
    res = rowwise(functools.partial(_rwkv_gates_fn, has_vres=has_vres), rows, [], [prm, ones_bd], outs,
                  seq=seq, tm=128, name="rwkv_gates")
    logdecay, k, kk, b = res[:4]
    if has_vres:
        v = res[4]
    else:
        v_first = v
    y = wkv7(r, logdecay, k, v, kk, b, batch=batch, seq=seq)
    prm2 = jnp.stack([p["r_k"], p["gn_g"], p["gn_b"]])
    yg = rowwise(_rwkv_out_fn, [y, r, k, v, g], [], [prm2, ones_bd], [(d, BF16)], seq=seq, tm=128, name="rwkv_out")[0]
    return mm(yg, p["w_o"]), v_first


def _layer_norm_j(x, g, b, eps=LN_EPS):
    mu = jnp.mean(x, -1, keepdims=True)
    var = jnp.mean(jnp.square(x - mu), -1, keepdims=True)
    return (x - mu) * lax.rsqrt(var + eps) * g + b


def _rms_norm_j(x, g, eps=1e-6):
    return x * lax.rsqrt(jnp.mean(x * x, -1, keepdims=True) + eps) * g


def _partial_rope_j(x, positions, rot_dim):
    half = rot_dim // 2
    inv_freq = ROPE_THETA ** (-jnp.arange(half, dtype=F32) * (2.0 / rot_dim))
    ang = positions.astype(F32)[:, :, None, None] * inv_freq
    cos, sin = jnp.cos(ang), jnp.sin(ang)
    xr = x[..., :rot_dim]
    x1, x2 = xr[..., :half], xr[..., half:]
    rot = jnp.concatenate([x1 * cos - x2 * sin, x2 * cos + x1 * sin], axis=-1)
    return jnp.concatenate([rot, x[..., rot_dim:]], axis=-1)


def dsa_layer(hbf, positions, p, *, batch, seq):
    B, T = batch, seq
    H, Hi = DSA_HEADS, IDX_HEADS
    topk = min(TOPK_MAX, T // 4)
    q = mm(hbf, p["w_q"]).reshape(B, T, H, DSA_HEAD_DIM)
    c_kv = mm(hbf, p["w_ckv"]).reshape(B, T, DSA_KV_RANK)
    q_idx = mm(hbf, p["w_qidx"]).reshape(B, T, Hi, IDX_DIM)
    small = mm(hbf, p["w_small"]).reshape(B, T, LANES)
    k_rope = small[..., :DSA_ROPE_DIM]
    k_idx = small[..., DSA_ROPE_DIM:DSA_ROPE_DIM + IDX_DIM]
    w_idx = small[..., DSA_ROPE_DIM + IDX_DIM:DSA_ROPE_DIM + IDX_DIM + Hi]
    q = _partial_rope_j(q, positions, DSA_ROPE_DIM)
    q_rope, q_nope = q[..., :DSA_ROPE_DIM], q[..., DSA_ROPE_DIM:]
    c_kv = _rms_norm_j(c_kv, p["kv_norm"])
    k_rope = _partial_rope_j(k_rope[:, :, None, :], positions, DSA_ROPE_DIM)[:, :, 0]
    q_idx = _partial_rope_j(q_idx, positions, IDX_ROPE_DIM)
    k_idx = _partial_rope_j(_layer_norm_j(k_idx, p["idx_ln_g"], p["idx_ln_b"])[:, :, None, :], positions,
                            IDX_ROPE_DIM)[:, :, 0]
    w_idx = w_idx * (Hi ** -0.5 * IDX_DIM ** -0.5)
    key_pos = jnp.arange(T)
    gather = jax.vmap(lambda src, idx: src[idx])
    scale = DSA_HEAD_DIM ** -0.5
    w_uk, w_uv = p["w_uk"], p["w_uv"]

    def block(j):
        s0 = j * Q_BLOCK
        sl = lambda z: lax.dynamic_slice_in_dim(z, s0, Q_BLOCK, axis=1)
        qi, wi, qr, qn = sl(q_idx), sl(w_idx), sl(q_rope), sl(q_nope)
        t = s0 + jnp.arange(Q_BLOCK)
        causal = key_pos[None, :] <= t[:, None]
        isc = jnp.einsum('bqh,bqhs->bqs', wi, jax.nn.relu(jnp.einsum('bqhd,bsd->bqhs', qi, k_idx)))
        isc = jnp.where(causal[None], isc, -jnp.inf)
        _, sel = lax.top_k(isc, topk)
        valid = sel <= t[None, :, None]
        c_sel = gather(c_kv, sel)
        kr_sel = gather(k_rope, sel)
        q_abs = jnp.einsum('bqhn,hnr->bqhr', qn, w_uk)
        s = jnp.einsum('bqhr,bqkr->bqhk', q_abs, c_sel) + jnp.einsum('bqhd,bqkd->bqhk', qr, kr_sel)
        s = jnp.where(valid[:, :, None, :], s * scale, -jnp.inf)
        pr = jax.nn.softmax(s, axis=-1)
        o_lat = jnp.einsum('bqhk,bqkr->bqhr', pr, c_sel)
        return jnp.einsum('bqhr,hrv->bqhv', o_lat, w_uv).reshape(B, Q_BLOCK, H * DSA_V_DIM)

    out = lax.map(block, jnp.arange(T // Q_BLOCK))
    out = jnp.moveaxis(out, 0, 1).reshape(B * T, H * DSA_V_DIM)
    return mm(out, p["w_o"])


def _s5_tables(p):
    hp = lax.Precision.HIGHEST
    lc, hg = S5_CHUNK, S5_GROUP
    lr, li = p["lam_re"], p["lam_im"]
    g, ps = lr.shape
    dt = jnp.exp(p["log_dt"])[:, None]
    mag = jnp.exp(lr * dt)
    ab_re, ab_im = mag * jnp.cos(li * dt), mag * jnp.sin(li * dt)
    den = lr * lr + li * li
    nr, ni = ab_re - 1.0, ab_im
    coef_re = (nr * lr + ni * li) / den
    coef_im = (ni * lr - nr * li) / den
    bb_re = coef_re[..., None] * p["b_re"] - coef_im[..., None] * p["b_im"]
    bb_im = coef_re[..., None] * p["b_im"] + coef_im[..., None] * p["b_re"]
    tau = jnp.arange(lc + 1, dtype=F32)[:, None, None]
    pmag = jnp.exp(lr * dt * tau)
    pr, pi = pmag * jnp.cos(li * dt * tau), pmag * jnp.sin(li * dt * tau)
    cr, ci = p["c_re"][None], p["c_im"][None]
    car = cr * pr[:, :, None, :] - ci * pi[:, :, None, :]
    cai = cr * pi[:, :, None, :] + ci * pr[:, :, None, :]
    kern = (jnp.einsum('tghp,gpk->tghk', car[:lc], bb_re, precision=hp)
            - jnp.einsum('tghp,gpk->tghk', cai[:lc], bb_im, precision=hp))
    kz = jnp.concatenate([kern, jnp.zeros((1,) + kern.shape[1:], F32)], 0)
    s_i = np.arange(lc)[:, None]
    t_i = np.arange(lc)[None, :]
    m = kz[np.where(t_i >= s_i, t_i - s_i, lc)]
    m = m.transpose(2, 0, 4, 1, 3).reshape(g, lc * hg, lc * hg)
    skip = jnp.tile(p["d"].reshape(g, 1, hg), (1, lc, 1)).reshape(g, 1, lc * hg)
    m = m + skip * jnp.eye(lc * hg, dtype=F32)[None]
    prs, pis = pr[lc - 1 - np.arange(lc)], pi[lc - 1 - np.arange(lc)]
    wre = prs[..., None] * bb_re[None] - pis[..., None] * bb_im[None]
    wim = prs[..., None] * bb_im[None] + pis[..., None] * bb_re[None]
    wre = wre.transpose(1, 0, 3, 2).reshape(g, lc * hg, ps)
    wim = wim.transpose(1, 0, 3, 2).reshape(g, lc * hg, ps)
    vre = car[1:].transpose(1, 3, 0, 2).reshape(g, ps, lc * hg)
    vim = (-cai[1:]).transpose(1, 3, 0, 2).reshape(g, ps, lc * hg)
    eye2 = jnp.eye(2, dtype=F32)[None, :, None, :, None]

    def pair(z):
        r, c = z.shape[1:]
        return (z.reshape(g // 2, 2, r, 1, c) * eye2).reshape(g // 2, 2 * r, 2 * c).astype(BF16)

    are = pr[lc].reshape(g // 2, 1, 2 * ps)
    aim = pi[lc].reshape(g // 2, 1, 2 * ps)
    return pair(m), pair(wre), pair(wim), pair(vre), pair(vim), are, aim


def _gelu_tanh(y):
    return 0.5 * y * (1.0 + jnp.tanh(math.sqrt(2.0 / math.pi) * (y + 0.044715 * (y * y * y))))


def _s5_body(u_ref, m_ref, wre_ref, wim_ref, vre_ref, vim_ref, are_ref, aim_ref, z_ref, xre, xim, sre, sim, *, pairs):
    nc = u_ref.shape[1]
    for q in range(pairs):
        u = u_ref[q]
        xre[q] = jnp.dot(u, wre_ref[q], preferred_element_type=F32)
        xim[q] = jnp.dot(u, wim_ref[q], preferred_element_type=F32)
    ar = [are_ref[q] for q in range(pairs)]
    ai = [aim_ref[q] for q in range(pairs)]

    def step(c, carry):
        new = []
        row = pl.ds(c, 1)
        for q in range(pairs):
            re, im = carry[2 * q], carry[2 * q + 1]
            sre[q, row, :] = re
            sim[q, row, :] = im
            new.append(ar[q] * re - ai[q] * im + xre[q, row, :])
            new.append(ar[q] * im + ai[q] * re + xim[q, row, :])
        return tuple(new)

    zero = jnp.zeros((1, are_ref.shape[-1]), F32)
    lax.fori_loop(0, nc, step, (zero,) * (2 * pairs), unroll=8)
    for q in range(pairs):
        y = (jnp.dot(u_ref[q], m_ref[q], preferred_element_type=F32)
             + jnp.dot(sre[q].astype(BF16), vre_ref[q], preferred_element_type=F32)
             + jnp.dot(sim[q].astype(BF16), vim_ref[q], preferred_element_type=F32))
        z_ref[q] = _gelu_tanh(y).astype(z_ref.dtype)


def s5_layer(hbf, p, *, batch, seq):
    n, D = hbf.shape
    lc, hg = S5_CHUNK, S5_GROUP
    g2 = D // hg // 2
    pw = 2 * lc * hg
    tabs = _s5_tables(p)
    nc = seq // lc
    u = hbf.reshape(n // lc, lc, g2, 2, hg).transpose(2, 0, 3, 1, 4).reshape(g2, n // lc, pw)
    pairs = S5_PAIRS_PER_STEP
    sw = 2 * S5_STATE
    wspec = lambda r, c: pl.BlockSpec((pairs, r, c), lambda i, b: (i, 0, 0))
    z = pl.pallas_call(
        functools.partial(_s5_body, pairs=pairs),
        grid=(g2 // pairs, batch),
        in_specs=[pl.BlockSpec((pairs, nc, pw), lambda i, b: (i, b, 0)),
                  wspec(pw, pw), wspec(pw, sw), wspec(pw, sw), wspec(sw, pw), wspec(sw, pw),
                  wspec(1, sw), wspec(1, sw)],
        out_specs=pl.BlockSpec((pairs, nc, pw), lambda i, b: (i, b, 0)),
        out_shape=jax.ShapeDtypeStruct((g2, n // lc, pw), BF16),
        scratch_shapes=[pltpu.VMEM((pairs, nc, sw), F32)] * 4,
        compiler_params=_cparams("parallel", "parallel"),
        name="s5_scan",
    )(u, *tabs)
    z = z.reshape(g2, n // lc, 2, lc, hg).transpose(1, 3, 0, 2, 4).reshape(n, D)
    zz = mm(z, p["w_glu"])
    return rowwise(lambda a: (a[:, :D] * jax.nn.sigmoid(a[:, D:]),), [zz], [], [], [(D, F32)], seq=seq, tm=256,
                   name="s5_glu")[0]


def _modulate(x, sc, sh, dtype, *, seq):
    d = x.shape[1]
    return rowwise(lambda xv, s, h: (xv * (1.0 + s) + h,), [x], [sc, sh], [], [(d, dtype)], seq=seq, tm=256,
                   name="modulate")[0]


def kernel(x, c, positions, ada_w, ada_b, ln_g, ln_b, router_w, router_bias, moe_w_gate, moe_w_up, moe_w_down,
           rwkv_mu, rwkv_w_rkv, rwkv_w_o, rwkv_w0, rwkv_w1, rwkv_w2, rwkv_a0, rwkv_a1, rwkv_a2, rwkv_g1, rwkv_g2,
           rwkv_k_k, rwkv_k_a, rwkv_r_k, rwkv_gn_g, rwkv_gn_b, rwkv_v0, rwkv_v1, rwkv_v2, dsa_w_in, dsa_kv_norm,
           dsa_w_uk, dsa_w_uv, dsa_idx_ln_g, dsa_idx_ln_b, dsa_w_o, s5_lam_re, s5_lam_im, s5_log_dt, s5_b_re,
           s5_b_im, s5_c_re, s5_c_im, s5_d, s5_w_glu):
    batch, seq, d = x.shape
    depth = ada_w.shape[0]
    n = batch * seq
    xf = x.reshape(n, d)
    mod = ada_mod(c, ada_w, ada_b)[:, :batch].reshape(depth, batch, 1, 6, d)
    v_first = None
    for i in range(depth):
        kind, j = i % N_MIXERS, i // N_MIXERS
        sh1, sc1, g1, sh2, sc2, g2 = (mod[i, :, :, m] for m in range(6))
        if kind == 0:
            p = dict(mu=rwkv_mu[j], w_r=rwkv_w_rkv[j, 0].astype(BF16), w_k=rwkv_w_rkv[j, 1].astype(BF16),
                     w_v=rwkv_w_rkv[j, 2].astype(BF16), w_o=rwkv_w_o[j].astype(BF16),
                     w0=rwkv_w0[j], w1=_wpad(rwkv_w1[j]), w2=_kpad(rwkv_w2[j]),
                     a0=rwkv_a0[j], a1=_wpad(rwkv_a1[j]), a2=_kpad(rwkv_a2[j]),
                     g1=_wpad(rwkv_g1[j]), g2=_kpad(rwkv_g2[j]),
                     k_k=rwkv_k_k[j], k_a=rwkv_k_a[j], r_k=rwkv_r_k[j], gn_g=rwkv_gn_g[j], gn_b=rwkv_gn_b[j],
                     v0=None)
            if j > 0:
                p.update(v0=rwkv_v0[j - 1], v1=_wpad(rwkv_v1[j - 1]), v2=_kpad(rwkv_v2[j - 1]))
            y, v_first = rwkv_layer(xf, sc1, sh1, v_first, p, batch=batch, seq=seq)
        elif kind == 1:
            w_in = dsa_w_in[j]
            hq = DSA_HEADS * DSA_HEAD_DIM
            o1 = hq + DSA_KV_RANK
            o2 = o1 + DSA_ROPE_DIM
            o3 = o2 + IDX_HEADS * IDX_DIM
            p = dict(w_q=w_in[:, :hq].astype(BF16), w_ckv=w_in[:, hq:o1].astype(BF16),
                     w_qidx=w_in[:, o2:o3].astype(BF16),
                     w_small=_wpad(jnp.concatenate([w_in[:, o1:o2], w_in[:, o3:]], axis=1)),
                     kv_norm=dsa_kv_norm[j], w_uk=dsa_w_uk[j], w_uv=dsa_w_uv[j],
                     idx_ln_g=dsa_idx_ln_g[j], idx_ln_b=dsa_idx_ln_b[j], w_o=dsa_w_o[j].astype(BF16))
            hbf = _modulate(xf, sc1, sh1, BF16, seq=seq)
            y = dsa_layer(hbf, positions, p, batch=batch, seq=seq)
        else:
            p = dict(lam_re=s5_lam_re[j], lam_im=s5_lam_im[j], log_dt=s5_log_dt[j], b_re=s5_b_re[j],
                     b_im=s5_b_im[j], c_re=s5_c_re[j], c_im=s5_c_im[j], d=s5_d[j], w_glu=s5_w_glu[j].astype(BF16))
            hbf = _modulate(xf, sc1, sh1, BF16, seq=seq)
            y = s5_layer(hbf, p, batch=batch, seq=seq)
        xf, hbf = post_norm_mix(xf, y, g1, sc2, sh2, ln_g[i, 0:1], ln_b[i, 0:1], seq=seq)
        y0, y1, gates = moe_ffn(xf, hbf, sc2, sh2, router_w, router_bias, moe_w_gate[i].astype(BF16),
                                moe_w_up[i].astype(BF16), moe_w_down[i].astype(BF16), seq=seq)
        xf = post_norm_moe(xf, y0, y1, gates, g2, ln_g[i, 1:2], ln_b[i, 1:2], seq=seq)
    return xf.reshape(batch, seq, d)
```

```python
import functools
import math

import numpy as np
import jax
import jax.numpy as jnp
from jax import lax
from jax.experimental import pallas as pl
from jax.experimental.pallas import tpu as pltpu

F32, BF16 = jnp.float32, jnp.bfloat16

DEPTH = 4
N_MIXERS = 3
RWKV_HEAD = 64
RWKV_GN_EPS = 64e-5
DSA_HEADS = 16
DSA_HEAD_DIM = 128
DSA_ROPE_DIM = 32
DSA_NOPE_DIM = 96
DSA_V_DIM = 128
DSA_KV_RANK = 512
IDX_HEADS = 16
IDX_DIM = 64
IDX_ROPE_DIM = 16
TOPK_MAX = 256
Q_BLOCK = 128
S5_GROUP = 16
S5_STATE = 64
S5_CHUNK = 16
S5_PAIRS_PER_STEP = 4
N_EXPERTS = 32
N_GROUPS = 4
EXPERTS_PER_GROUP = 8
TOP_K = 2
ROPE_THETA = 500000.0
LN_EPS = 1e-5
DEEPNORM_ALPHA = (2 * DEPTH) ** 0.25

LANES = 128
VMEM_LIMIT = 48 * 1024 * 1024
MOE_ROWS = 256
WKV_CHUNK = 64
WKV_HEADS_PER_STEP = 4
WKV_ROWS_PER_STEP = 512


def _cparams(*sem):
    return pltpu.CompilerParams(dimension_semantics=sem, vmem_limit_bytes=VMEM_LIMIT)


def _pick(n, cands):
    for c in cands:
        if n % c == 0:
            return c
    return n


def _mm_body(a_ref, w_ref, o_ref, *, act):
    acc = jnp.dot(a_ref[...].astype(BF16), w_ref[...], preferred_element_type=F32)
    if act == "tanh":
        acc = jnp.tanh(acc)
    elif act == "sigmoid":
        acc = jax.nn.sigmoid(acc)
    o_ref[...] = acc.astype(o_ref.dtype)


def mm(a, w, *, out_dtype=F32, act=None, tm=512):
    m, k = a.shape
    n = w.shape[1]
    tm = _pick(m, (tm, 256, 128, 64, 32, 16, 8))
    tn = _pick(n, (512, 256, 128))
    return pl.pallas_call(
        functools.partial(_mm_body, act=act),
        grid=(m // tm, n // tn),
        in_specs=[pl.BlockSpec((tm, k), lambda i, j: (i, 0)), pl.BlockSpec((k, tn), lambda i, j: (0, j))],
        out_specs=pl.BlockSpec((tm, tn), lambda i, j: (i, j)),
        out_shape=jax.ShapeDtypeStruct((m, n), out_dtype),
        compiler_params=_cparams("parallel", "parallel"),
        name="mm",
    )(a, w)


def _wpad(w):
    n = w.shape[1]
    npad = -(-n // LANES) * LANES
    w = w.astype(BF16)
    return w if npad == n else jnp.pad(w, ((0, 0), (0, npad - n)))


def _kpad(w):
    k = w.shape[0]
    kpad = -(-k // LANES) * LANES
    w = w.astype(BF16)
    return w if kpad == k else jnp.pad(w, ((0, kpad - k), (0, 0)))


def rowwise(fn, rows, perbatch, consts, outs, *, seq, tm, name):
    n = rows[0].shape[0]
    tm = min(tm, seq)
    tpb = seq // tm
    nr, nb, nc = len(rows), len(perbatch), len(consts)

    def body(*refs):
        vals = [r[...] for r in refs[:nr]]
        vals += [r[0] for r in refs[nr:nr + nb]]
        vals += [r[...] for r in refs[nr + nb:nr + nb + nc]]
        res = fn(*vals)
        for o, v in zip(refs[nr + nb + nc:], res):
            o[...] = v.astype(o.dtype)

    in_specs = [pl.BlockSpec((tm, r.shape[1]), lambda i: (i, 0)) for r in rows]
    in_specs += [pl.BlockSpec((1, 1, p.shape[-1]), lambda i: (i // tpb, 0, 0)) for p in perbatch]
    in_specs += [pl.BlockSpec(c.shape, lambda i: (0, 0)) for c in consts]
    res = pl.pallas_call(
        body,
        grid=(n // tm,),
        in_specs=in_specs,
        out_specs=[pl.BlockSpec((tm, w), lambda i: (i, 0)) for w, _ in outs],
        out_shape=[jax.ShapeDtypeStruct((n, w), dt) for w, dt in outs],
        compiler_params=_cparams("parallel"),
        name=name,
    )(*rows, *perbatch, *consts)
    return res


def _layer_norm(z, g, b):
    mu = jnp.mean(z, -1, keepdims=True)
    zc = z - mu
    var = jnp.mean(zc * zc, -1, keepdims=True)
    return zc * lax.rsqrt(var + LN_EPS) * g + b


def _split_dot(x, m):
    xh = x.astype(BF16)
    xl = (x - xh.astype(F32)).astype(BF16)
    return jnp.dot(xh, m, preferred_element_type=F32) + jnp.dot(xl, m, preferred_element_type=F32)


def _seg_sum(x, ones_bd):
    parts = [_split_dot(x[:, j:j + LANES], ones_bd) for j in range(0, x.shape[1], LANES)]
    return jnp.concatenate(parts, -1)


def _head_ones():
    i = np.arange(LANES)
    return jnp.asarray((i[:, None] // RWKV_HEAD) == (i[None, :] // RWKV_HEAD), BF16)


def _ada_body(c_ref, w_ref, b_ref, o_ref):
    acc = jnp.dot(c_ref[...], w_ref[0], preferred_element_type=F32, precision=lax.Precision.HIGHEST)
    o_ref[0] = acc + b_ref[0]


def ada_mod(c, ada_w, ada_b):
    depth, d, n = ada_w.shape
    b = c.shape[0]
    cp = jnp.pad(c, ((0, 8 - b), (0, 0)))
    tn = 1024
    return pl.pallas_call(
        _ada_body,
        grid=(depth, n // tn),
        in_specs=[pl.BlockSpec((8, d), lambda i, j: (0, 0)),
                  pl.BlockSpec((1, d, tn), lambda i, j: (i, 0, j)),
                  pl.BlockSpec((1, 1, tn), lambda i, j: (i, 0, j))],
        out_specs=pl.BlockSpec((1, 8, tn), lambda i, j: (i, 0, j)),
        out_shape=jax.ShapeDtypeStruct((depth, 8, n), F32),
        compiler_params=_cparams("parallel", "parallel"),
        name="ada_mod",
    )(cp, ada_w, ada_b.reshape(depth, 1, n))


def post_norm_mix(x, y, gate, sc, sh, lng, lnb, *, seq):
    d = x.shape[1]

    def fn(xv, yv, g, s, h, lg, lb):
        xn = _layer_norm(DEEPNORM_ALPHA * xv + (1.0 + g) * yv, lg, lb)
        return xn, xn * (1.0 + s) + h

    return rowwise(fn, [x, y], [gate, sc, sh], [lng, lnb], [(d, F32), (d, BF16)], seq=seq, tm=256, name="post_norm_mix")


def post_norm_moe(x, y0, y1, gates, gate, lng, lnb, *, seq):
    d = x.shape[1]

    def fn(xv, a, b, gt, g, lg, lb):
        y = a * gt[:, 0:1] + b * gt[:, 1:2]
        return (_layer_norm(DEEPNORM_ALPHA * xv + (1.0 + g) * y, lg, lb),)

    return rowwise(fn, [x, y0, y1, gates], [gate], [lng, lnb], [(d, F32)], seq=seq, tm=256, name="post_norm_moe")[0]


def _router_fn(xv, s, h, rw, rb):
    hf = xv * (1.0 + s) + h
    logits = jnp.dot(hf, rw, preferred_element_type=F32, precision=lax.Precision.HIGHEST)
    scores = jax.nn.sigmoid(logits)
    biased = scores + rb
    lane = lax.broadcasted_iota(jnp.int32, biased.shape, 1)
    neg = jnp.float32(-jnp.inf)
    big = jnp.int32(1 << 20)
    best = bi1 = bi2 = None
    for g in range(N_GROUPS):
        ing = (lane >= g * EXPERTS_PER_GROUP) & (lane < (g + 1) * EXPERTS_PER_GROUP)
        v = jnp.where(ing, biased, neg)
        m1 = jnp.max(v, -1, keepdims=True)
        i1 = jnp.min(jnp.where(v == m1, lane, big), -1, keepdims=True)
        v2 = jnp.where(lane == i1, neg, v)
        m2 = jnp.max(v2, -1, keepdims=True)
        i2 = jnp.min(jnp.where(v2 == m2, lane, big), -1, keepdims=True)
        gs = m1 + m2
        if g == 0:
            best, bi1, bi2 = gs, i1, i2
        else:
            better = gs > best
            best = jnp.where(better, gs, best)
            bi1 = jnp.where(better, i1, bi1)
            bi2 = jnp.where(better, i2, bi2)
    g1 = jnp.sum(jnp.where(lane == bi1, scores, 0.0), -1, keepdims=True)
    g2 = jnp.sum(jnp.where(lane == bi2, scores, 0.0), -1, keepdims=True)
    tot = g1 + g2
    two = lax.broadcasted_iota(jnp.int32, (biased.shape[0], TOP_K), 1)
    return jnp.where(two == 0, bi1, bi2), jnp.where(two == 0, g1 / tot, g2 / tot)


def _expert_body(be_ref, nb_ref, x_ref, wg_ref, wu_ref, wd_ref, o_ref):
    del be_ref

    @pl.when(pl.program_id(0) < nb_ref[0])
    def _():
        x = x_ref[...]
        g = jnp.dot(x, wg_ref[0], preferred_element_type=F32)
        u = jnp.dot(x, wu_ref[0], preferred_element_type=F32)
        hid = (g * jax.nn.sigmoid(g) * u).astype(BF16)
        o_ref[...] = jnp.dot(hid, wd_ref[0], preferred_element_type=F32)

    @pl.when(pl.program_id(0) >= nb_ref[0])
    def _():
        o_ref[...] = jnp.zeros_like(o_ref)


def moe_ffn(x, hbf, sc, sh, router_w, router_bias, wg, wu, wd, *, seq):
    n, d = x.shape
    e = router_w.shape[1]
    idx, gates = rowwise(_router_fn, [x], [sc, sh], [router_w, router_bias.reshape(1, e)],
                         [(TOP_K, jnp.int32), (TOP_K, F32)], seq=seq, tm=256, name="router")
    bm = MOE_ROWS
    n_slots = n * TOP_K
    flat_e = idx.reshape(-1)
    onehot = (flat_e[:, None] == jnp.arange(e, dtype=jnp.int32)[None, :]).astype(jnp.int32)
    csum = jnp.cumsum(onehot, axis=0)
    pos = jnp.sum(onehot * csum, axis=1) - 1
    counts = csum[-1]
    padded = (counts + bm - 1) // bm * bm
    pend = jnp.cumsum(padded)
    pstart = pend - padded
    dest = pstart[flat_e] + pos
    n_blocks = n_slots // bm + e
    n_pad = n_blocks * bm
    slot_tok = jnp.zeros((n_pad,), jnp.int32).at[dest].set(jnp.arange(n_slots, dtype=jnp.int32) // TOP_K)
    block_expert = jnp.minimum(
        jnp.searchsorted(pend, jnp.arange(n_blocks, dtype=jnp.int32) * bm, side="right"), e - 1).astype(jnp.int32)
    used_blocks = (pend[-1] // bm).astype(jnp.int32).reshape(1)
    xs = jnp.take(hbf, slot_tok, axis=0)
    de = wg.shape[2]
    yb = pl.pallas_call(
        _expert_body,
        grid_spec=pltpu.PrefetchScalarGridSpec(
            num_scalar_prefetch=2,
            grid=(n_blocks,),
            in_specs=[pl.BlockSpec((bm, d), lambda i, be, nb: (i, 0)),
                      pl.BlockSpec((1, d, de), lambda i, be, nb: (be[i], 0, 0)),
                      pl.BlockSpec((1, d, de), lambda i, be, nb: (be[i], 0, 0)),
                      pl.BlockSpec((1, de, d), lambda i, be, nb: (be[i], 0, 0))],
            out_specs=pl.BlockSpec((bm, d), lambda i, be, nb: (i, 0)),
        ),
        out_shape=jax.ShapeDtypeStruct((n_pad, d), F32),
        compiler_params=_cparams("arbitrary"),
        name="moe_experts",
    )(block_expert, used_blocks, xs, wg, wu, wd)
    dest2 = dest.reshape(n, TOP_K)
    return jnp.take(yb, dest2[:, 0], axis=0), jnp.take(yb, dest2[:, 1], axis=0), gates


def _rwkv_prep_body(x_ref, xp_ref, sc_ref, sh_ref, mu_ref, *outs, tpb):
    sc = 1.0 + sc_ref[0]
    sh = sh_ref[0]
    h = x_ref[...] * sc + sh
    prev_row = xp_ref[7:8, :] * sc + sh
    prev_row = jnp.where(pl.program_id(0) % tpb == 0, 0.0, prev_row)
    rowid = lax.broadcasted_iota(jnp.int32, h.shape, 0)
    hprev = jnp.where(rowid == 0, prev_row, pltpu.roll(h, 1, 0))
    dx = hprev - h
    for m, o in enumerate(outs):
        o[...] = (h + dx * mu_ref[m:m + 1, :]).astype(o.dtype)


def rwkv_prep(x, sc, sh, mu, *, seq):
    n, d = x.shape
    tm = min(256, seq)
    tpb = seq // tm
    return pl.pallas_call(
        functools.partial(_rwkv_prep_body, tpb=tpb),
        grid=(n // tm,),
        in_specs=[pl.BlockSpec((tm, d), lambda i: (i, 0)),
                  pl.BlockSpec((8, d), lambda i: (jnp.maximum(i * (tm // 8) - 1, 0), 0)),
                  pl.BlockSpec((1, 1, d), lambda i: (i // tpb, 0, 0)),
                  pl.BlockSpec((1, 1, d), lambda i: (i // tpb, 0, 0)),
                  pl.BlockSpec((6, d), lambda i: (0, 0))],
        out_specs=[pl.BlockSpec((tm, d), lambda i: (i, 0))] * 6,
        out_shape=[jax.ShapeDtypeStruct((n, d), BF16)] * 6,
        compiler_params=_cparams("parallel"),
        name="rwkv_prep",
    )(x, x, sc, sh, mu)


def _softplus(z):
    return jnp.maximum(z, 0.0) + jnp.log(1.0 + jnp.exp(-jnp.abs(z)))


def _rwkv_gates_fn(k0, lw, al, *rest, has_vres):
    if has_vres:
        v, vfirst, vl, prm, ones_bd = rest
    else:
        prm, ones_bd = rest
    w0, a0, k_k, k_a = prm[0:1], prm[1:2], prm[2:3], prm[3:4]
    log_w = -_softplus(-(w0 + lw)) - 0.5
    logdecay = -jnp.exp(log_w)
    a = jax.nn.sigmoid(a0 + al)
    kk = k0 * k_k
    nrm = jnp.maximum(jnp.sqrt(_seg_sum(kk * kk, ones_bd)), 1e-12)
    kk = kk / nrm
    k = k0 * (1.0 + (a - 1.0) * k_a)
    res = [logdecay, k, kk, kk * a]
    if has_vres:
        res.append(v + (vfirst - v) * jax.nn.sigmoid(prm[4:5] + vl))
    return res


def _wkv_body(r_ref, lw_ref, k_ref, v_ref, kk_ref, b_ref, y_ref, s_ref, *, chunk, heads):
    hd = RWKV_HEAD
    rows = r_ref.shape[0]

    @pl.when(pl.program_id(2) == 0)
    def _():
        s_ref[...] = jnp.zeros_like(s_ref)

    ti = lax.broadcasted_iota(jnp.int32, (chunk, chunk), 0)
    si = lax.broadcasted_iota(jnp.int32, (chunk, chunk), 1)
    incl = ti >= si
    strict = ti > si
    tri = incl.astype(BF16)
    eye = (ti == si).astype(F32)

    def step(c, carry):
        sl = pl.ds(pl.multiple_of(c * chunk, chunk), chunk)
        lw = lw_ref[sl, :]
        cs = _split_dot_left(tri, lw)
        ctot = cs[chunk - 1:chunk, :]
        g_inv = jnp.exp(-cs)
        rg = r_ref[sl, :] * jnp.exp(cs)
        kk = kk_ref[sl, :]
        ag = -kk * jnp.exp(cs - lw)
        kv = k_ref[sl, :]
        bv = b_ref[sl, :]
        kg = kv * g_inv
        bg = bv * g_inv
        g_end = jnp.exp(ctot - cs)
        kend = kv * g_end
        bend = bv * g_end
        g_tot = jnp.exp(ctot)
        vv = v_ref[sl, :]
        ys = []
        for h in range(heads):
            hs = slice(h * hd, (h + 1) * hd)
            s_old = s_ref[h]
            ar = jnp.concatenate([ag[:, hs], rg[:, hs]], 0).astype(BF16)
            bk = jnp.concatenate([bg[:, hs], kg[:, hs]], 0).astype(BF16)
            m = lax.dot_general(ar, bk, (((1,), (1,)), ((), ())), preferred_element_type=F32)
            a_ab = jnp.where(strict, m[:chunk, :chunk], 0.0)
            a_ak = jnp.where(strict, m[:chunk, chunk:], 0.0)
            a_rb = jnp.where(incl, m[chunk:, :chunk], 0.0)
            a_rk = jnp.where(incl, m[chunk:, chunk:], 0.0)
            ars = lax.dot_general(ar, s_old.astype(BF16), (((1,), (1,)), ((), ())), preferred_element_type=F32)
            vh = vv[:, hs]
            vb = vh.astype(BF16)
            rhs = ars[:chunk] + jnp.dot(a_ak.astype(BF16), vb, preferred_element_type=F32)
            inv = eye + a_ab
            apow = a_ab.astype(BF16)
            for _ in range(int(math.log2(chunk)) - 1):
                ap = jnp.dot(apow, apow, preferred_element_type=F32)
                apow = ap.astype(BF16)
                inv = inv + jnp.dot(inv.astype(BF16), apow, preferred_element_type=F32)
            u = jnp.dot(inv.astype(BF16), rhs.astype(BF16), preferred_element_type=F32)
            uv = jnp.concatenate([u.astype(BF16), vb], 0)
            y = ars[chunk:] + jnp.dot(jnp.concatenate([a_rb, a_rk], 1).astype(BF16), uv, preferred_element_type=F32)
            ys.append(y)
            ends = jnp.concatenate([bend[:, hs], kend[:, hs]], 0).astype(BF16)
            s_ref[h] = s_old * g_tot[:, hs] + lax.dot_general(
                uv, ends, (((0,), (0,)), ((), ())), preferred_element_type=F32)
        y_ref[sl, :] = jnp.concatenate(ys, -1)
        return carry

    lax.fori_loop(0, rows // chunk, step, 0)


def _split_dot_left(m, x):
    xh = x.astype(BF16)
    xl = (x - xh.astype(F32)).astype(BF16)
    return jnp.dot(m, xh, preferred_element_type=F32) + jnp.dot(m, xl, preferred_element_type=F32)


def wkv7(r, logdecay, k, v, kk, b, *, batch, seq):
    n, d = r.shape
    chunk = min(WKV_CHUNK, seq)
    rows = min(WKV_ROWS_PER_STEP, seq)
    heads = WKV_HEADS_PER_STEP
    width = heads * RWKV_HEAD
    spb = seq // rows
    spec = pl.BlockSpec((rows, width), lambda bi, hi, ti: (bi * spb + ti, hi))
    return pl.pallas_call(
        functools.partial(_wkv_body, chunk=chunk, heads=heads),
        grid=(batch, d // width, spb),
        in_specs=[spec] * 6,
        out_specs=spec,
        out_shape=jax.ShapeDtypeStruct((n, d), F32),
        scratch_shapes=[pltpu.VMEM((heads, RWKV_HEAD, RWKV_HEAD), F32)],
        compiler_params=_cparams("parallel", "parallel", "arbitrary"),
        name="wkv7",
    )(r, logdecay, k, v, kk, b)


def _rwkv_out_fn(y, r, k, v, g, prm, ones_bd):
    r_k, gn_g, gn_b = prm[0:1], prm[1:2], prm[2:3]
    inv_n = 1.0 / RWKV_HEAD
    m_y = _seg_sum(y, ones_bd) * inv_n
    yc = y - m_y
    v_y = _seg_sum(yc * yc, ones_bd) * inv_n
    yn = yc * lax.rsqrt(v_y + RWKV_GN_EPS) * gn_g + gn_b
    bonus = _seg_sum(r * k * r_k, ones_bd) * v
    return ((yn + bonus) * g,)


def rwkv_layer(x, sc, sh, v_first, p, *, batch, seq):
    n, d = x.shape
    xr, xw, xk, xv, xa, xg = rwkv_prep(x, sc, sh, p["mu"], seq=seq)
    r = mm(xr, p["w_r"])
    k0 = mm(xk, p["w_k"])
    v = mm(xv, p["w_v"])
    lw = mm(mm(xw, p["w1"], out_dtype=BF16, act="tanh"), p["w2"])
    al = mm(mm(xa, p["a1"], out_dtype=BF16), p["a2"])
    g = mm(mm(xg, p["g1"], out_dtype=BF16, act="sigmoid"), p["g2"])
    ones_bd = _head_ones()
    has_vres = p["v0"] is not None
    outs = [(d, F32)] * (5 if has_vres else 4)
    if has_vres:
        vl = mm(mm(xv, p["v1"], out_dtype=BF16), p["v2"])
        prm = jnp.stack([p["w0"], p["a0"], p["k_k"], p["k_a"], p["v0"]])
        rows = [k0, lw, al, v, v_first, vl]
    else:
        prm = jnp.stack([p["w0"], p["a0"], p["k_k"], p["k_a"]])
        rows = [k0, lw, al]
    res = rowwise(functools.partial(_rwkv_gates_fn, has_vres=has_vres), rows, [], [prm, ones_bd], outs,
                  seq=seq, tm=128, name="rwkv_gates")
    logdecay, k, kk, b = res[:4]
    if has_vres:
        v = res[4]
    else:
        v_first = v
    y = wkv7(r, logdecay, k, v, kk, b, batch=batch, seq=seq)
    prm2 = jnp.stack([p["r_k"], p["gn_g"], p["gn_b"]])
    yg = rowwise(_rwkv_out_fn, [y, r, k, v, g], [], [prm2, ones_bd], [(d, BF16)], seq=seq, tm=128, name="rwkv_out")[0]
    return mm(yg, p["w_o"]), v_first


SM_W0 = IDX_DIM
SM_R0 = LANES - DSA_ROPE_DIM
LAT_W = 768
LAT_PACK = LAT_W // 2


def _dsa_patterns():
    fq = ROPE_THETA ** (-np.arange(DSA_ROPE_DIM // 2, dtype=np.float32) * np.float32(2.0 / DSA_ROPE_DIM))
    fi = ROPE_THETA ** (-np.arange(IDX_ROPE_DIM // 2, dtype=np.float32) * np.float32(2.0 / IDX_ROPE_DIM))
    hq, hi = DSA_ROPE_DIM // 2, IDX_ROPE_DIM // 2
    pat = np.zeros((16, LANES), np.float32)
    lane = np.arange(LANES)
    pat[0, :hq] = fq; pat[0, hq:2 * hq] = fq
    pat[1, :hq] = -1.0; pat[2, hq:2 * hq] = 1.0
    l64 = lane % IDX_DIM
    pat[3] = np.where(l64 < hi, fi[np.minimum(l64, hi - 1)], np.where(l64 < 2 * hi, fi[np.clip(l64 - hi, 0, hi - 1)], 0.0))
    pat[4] = np.where(l64 < hi, -1.0, 0.0); pat[5] = np.where((l64 >= hi) & (l64 < 2 * hi), 1.0, 0.0)
    pat[6, :hi] = fi; pat[6, hi:2 * hi] = fi
    pat[6, SM_R0:SM_R0 + hq] = fq; pat[6, SM_R0 + hq:] = fq
    pat[7, :hi] = -1.0; pat[8, hi:2 * hi] = 1.0
    pat[9, SM_R0:SM_R0 + hq] = -1.0; pat[10, SM_R0 + hq:] = 1.0
    pat[11, :IDX_DIM] = 1.0
    pat[12, SM_W0:SM_W0 + IDX_HEADS] = IDX_HEADS ** -0.5 * IDX_DIM ** -0.5
    pat[13, SM_R0:] = 1.0
    return jnp.asarray(pat)


def _rot(x, shift):
    return pltpu.roll(x, shift % x.shape[1], 1)


def _tile_lanes(v, reps):
    return jnp.concatenate([v] * reps, 1) if reps > 1 else v


def _dsa_prep_fn(q, ckv, qidx, small, pos, kvn, lng, lnb, pat):
    posf = pos.astype(F32)
    hq, hi = DSA_ROPE_DIM // 2, IDX_ROPE_DIM // 2
    ang = posf * pat[0:1]
    c, s = jnp.cos(ang), jnp.sin(ang)
    reps = q.shape[1] // LANES
    qr = (q * _tile_lanes(c, reps) + _rot(q, -hq) * _tile_lanes(s * pat[1:2], reps)
          + _rot(q, hq) * _tile_lanes(s * pat[2:3], reps))
    ang = posf * pat[3:4]
    c, s = jnp.cos(ang), jnp.sin(ang)
    reps = qidx.shape[1] // LANES
    qi = (qidx * _tile_lanes(c, reps) + _rot(qidx, -hi) * _tile_lanes(s * pat[4:5], reps)
          + _rot(qidx, hi) * _tile_lanes(s * pat[5:6], reps))
    ckn = ckv * lax.rsqrt(jnp.mean(ckv * ckv, -1, keepdims=True) + 1e-6) * kvn
    mk = pat[11:12]
    inv = 1.0 / IDX_DIM
    mu = jnp.sum(small * mk, -1, keepdims=True) * inv
    dv = (small - mu) * mk
    var = jnp.sum(dv * dv, -1, keepdims=True) * inv
    y = (dv * lax.rsqrt(var + LN_EPS) * lng + lnb) * mk + small * pat[12:13] + small * pat[13:14]
    ang = posf * pat[6:7]
    c, s = jnp.cos(ang), jnp.sin(ang)
    so = (y * c + _rot(y, -hi) * (s * pat[7:8]) + _rot(y, hi) * (s * pat[8:9])
          + _rot(y, -hq) * (s * pat[9:10]) + _rot(y, hq) * (s * pat[10:11]))
    lat = jnp.concatenate([ckn, so * pat[13:14], jnp.zeros_like(so)], 1)
    lo = lax.bitcast_convert_type(lat[:, :LAT_PACK].astype(BF16).astype(F32), jnp.uint32) >> 16
    hi_b = lax.bitcast_convert_type(lat[:, LAT_PACK:].astype(BF16).astype(F32), jnp.uint32) & jnp.uint32(0xFFFF0000)
    return qr, lo | hi_b, qi, so, so


def _sortable(x):
    b = lax.bitcast_convert_type(x, jnp.int32)
    return jnp.where(b < 0, b ^ jnp.int32(0x7FFFFFFF), b)


def _dsa_select_body(qi_ref, sm_ref, kx_ref, sel_ref, key_ref, ib_ref, *, tq, ts, topk, seq):
    sub = 32
    t0 = pl.program_id(1) * tq
    nchunk = (t0 + tq + ts - 1) // ts
    ntile = nchunk * (ts // LANES)
    qpos = t0 + lax.broadcasted_iota(jnp.int32, (tq, 1), 0)
    w = sm_ref[:, SM_W0:SM_W0 + IDX_HEADS]
    neg = jnp.float32(-jnp.inf)

    def score_chunk(c, carry):
        off = pl.multiple_of(c * ts, ts)
        ks = kx_ref[pl.ds(off, ts), :][:, :IDX_DIM]
        acc = jnp.zeros((tq, ts), F32)
        for h in range(IDX_HEADS):
            s = lax.dot_general(qi_ref[:, h * IDX_DIM:(h + 1) * IDX_DIM], ks, (((1,), (1,)), ((), ())),
                                preferred_element_type=F32)
            acc = acc + w[:, h:h + 1] * jnp.maximum(s, 0.0)
        kpos = off + lax.broadcasted_iota(jnp.int32, (1, ts), 1)
        key_ref[:, pl.ds(off, ts)] = _sortable(jnp.where(kpos <= qpos, acc, neg))
        return carry

    lax.fori_loop(0, nchunk, score_chunk, 0)

    def count(pred):
        def body(j, acc):
            off = pl.multiple_of(j * LANES, LANES)
            kpos = off + lax.broadcasted_iota(jnp.int32, (1, LANES), 1)
            return acc + pred(key_ref[:, pl.ds(off, LANES)], kpos).astype(jnp.int32)
        acc = lax.fori_loop(0, ntile, body, jnp.zeros((tq, LANES), jnp.int32))
        return jnp.sum(acc, -1, keepdims=True)

    sign = jnp.int32(-2 ** 31)
    thr_u = jnp.zeros((tq, 1), jnp.int32)
    for bit in range(31, -1, -1):
        cand_u = thr_u | jnp.int32(np.uint32(1 << bit).astype(np.int32))
        cand = cand_u ^ sign
        cnt = count(lambda k, kp, cand=cand: k >= cand)
        thr_u = jnp.where(cnt >= topk, cand_u, thr_u)
    thr = thr_u ^ sign
    need = topk - count(lambda k, kp: k > thr)
    n_eq = count(lambda k, kp: k == thr)

    ib_ref[...] = jnp.full((tq, 1), seq, jnp.int32)

    @pl.when(jnp.max(n_eq - need) > 0)
    def _():
        ib = jnp.zeros((tq, 1), jnp.int32)
        for bit in range(int(math.log2(seq)) - 1, -1, -1):
            cand = ib | jnp.int32(1 << bit)
            cnt = count(lambda k, kp, cand=cand: (k == thr) & (kp < cand))
            ib = jnp.where(cnt < need, cand, ib)
        ib_ref[...] = ib

    ib = ib_ref[...]

    def mask_tile(j, acc):
        off = pl.multiple_of(j * LANES, LANES)
        kpos = off + lax.broadcasted_iota(jnp.int32, (1, LANES), 1)
        k = key_ref[:, pl.ds(off, LANES)]
        m = (((k > thr) | ((k == thr) & (kpos <= ib))) & (kpos <= qpos)).astype(jnp.int32)
        key_ref[:, pl.ds(off, LANES)] = m
        return acc + m

    colcnt = lax.fori_loop(0, ntile, mask_tile, jnp.zeros((tq, LANES), jnp.int32))
    li = lax.broadcasted_iota(jnp.int32, (LANES, LANES), 0)
    lj = lax.broadcasted_iota(jnp.int32, (LANES, LANES), 1)
    cin = jnp.dot(colcnt.astype(BF16), (li <= lj).astype(BF16), preferred_element_type=F32).astype(jnp.int32)
    total = cin[:, LANES - 1:LANES]
    slot = lax.broadcasted_iota(jnp.int32, (tq, topk), 1)
    lane_of = jnp.zeros((tq, topk), jnp.int32)
    base = jnp.zeros((tq, topk), jnp.int32)
    for l in range(LANES):
        c = cin[:, l:l + 1]
        le = c <= slot
        lane_of = lane_of + le.astype(jnp.int32)
        base = jnp.maximum(base, jnp.where(le, c, 0))
    lane_of = jnp.minimum(lane_of, LANES - 1)
    rank = slot - base
    nhalf = topk // LANES
    for r0 in range(0, tq, sub):
        rows = slice(r0, r0 + sub)
        lanes_h = [lane_of[rows, i * LANES:(i + 1) * LANES] for i in range(nhalf)]
        ranks_h = [rank[rows, i * LANES:(i + 1) * LANES] for i in range(nhalf)]

        def tile_body(j, carry):
            pre = carry[0] + key_ref[r0:r0 + sub, pl.ds(pl.multiple_of(j * LANES, LANES), LANES)]
            out = [pre]
            for i in range(nhalf):
                got = jnp.take_along_axis(pre, lanes_h[i], axis=1)
                out.append(carry[1 + i] + (got <= ranks_h[i]).astype(jnp.int32))
            return tuple(out)

        res = lax.fori_loop(0, ntile, tile_body, (jnp.zeros((sub, LANES), jnp.int32),) * (1 + nhalf))
        pos = jnp.concatenate([res[1 + i] * LANES + lanes_h[i] for i in range(nhalf)], 1)
        slot_sub = lax.broadcasted_iota(jnp.int32, (sub, topk), 1)
        sel_ref[rows, :] = jnp.where(slot_sub < total[rows], pos, -1)


def dsa_select(qi, small, kx, *, batch, seq):
    n = qi.shape[0]
    topk = min(TOPK_MAX, seq // 4)
    tq = min(128, seq)
    ts = min(512, seq)
    qpb = seq // tq
    return pl.pallas_call(
        functools.partial(_dsa_select_body, tq=tq, ts=ts, topk=topk, seq=seq),
        grid=(batch, qpb),
        in_specs=[pl.BlockSpec((tq, qi.shape[1]), lambda b, i: (b * qpb + i, 0)),
                  pl.BlockSpec((tq, LANES), lambda b, i: (b * qpb + i, 0)),
                  pl.BlockSpec((seq, LANES), lambda b, i: (b, 0))],
        out_specs=pl.BlockSpec((tq, topk), lambda b, i: (b * qpb + i, 0)),
        out_shape=jax.ShapeDtypeStruct((n, topk), jnp.int32),
        scratch_shapes=[pltpu.VMEM((tq, seq), jnp.int32), pltpu.VMEM((tq, 1), jnp.int32)],
        compiler_params=_cparams("parallel", "arbitrary"),
        name="dsa_select",
    )(qi, small, kx)


def _dsa_attn_body(sel_s, selv_ref, q2_ref, tbl_ref, o_ref, stage, *, tqa, topk):
    heads = DSA_HEADS
    scale = DSA_HEAD_DIM ** -0.5

    def per_query(i, carry):
        def fetch(j, c2):
            r = jnp.maximum(sel_s[i, j], 0)
            stage[pl.ds(j, 1), :] = tbl_ref[pl.ds(r, 1), :]
            return c2

        lax.fori_loop(0, topk, fetch, 0, unroll=8)
        pk = stage[...]
        lo = lax.bitcast_convert_type(pk << 16, F32)
        hi = lax.bitcast_convert_type(pk & jnp.uint32(0xFFFF0000), F32)
        kv = jnp.concatenate([lo, hi], 1).astype(BF16)
        q = q2_ref[pl.ds(pl.multiple_of(i * heads, heads), heads), :]
        s = lax.dot_general(q, kv, (((1,), (1,)), ((), ())), preferred_element_type=F32) * scale
        s = jnp.where(selv_ref[pl.ds(i, 1), :] >= 0, s, -jnp.inf)
        e = jnp.exp(s - jnp.max(s, -1, keepdims=True))
        p = e / jnp.sum(e, -1, keepdims=True)
        o = jnp.dot(p.astype(BF16), kv[:, :DSA_KV_RANK], preferred_element_type=F32)
        o_ref[pl.ds(pl.multiple_of(i * heads, heads), heads), :] = o.astype(o_ref.dtype)
        return carry

    lax.fori_loop(0, tqa, per_query, 0)


def dsa_attend(sel, q2, table, *, batch, seq):
    n, topk = sel.shape
    tqa = min(64, seq)
    qpb = seq // tqa
    heads = DSA_HEADS
    return pl.pallas_call(
        functools.partial(_dsa_attn_body, tqa=tqa, topk=topk),
        grid=(batch, qpb),
        in_specs=[pl.BlockSpec((tqa, topk), lambda b, i: (b * qpb + i, 0), memory_space=pltpu.SMEM),
                  pl.BlockSpec((tqa, topk), lambda b, i: (b * qpb + i, 0)),
                  pl.BlockSpec((tqa * heads, LAT_W), lambda b, i: (b * qpb + i, 0)),
                  pl.BlockSpec((seq, LAT_PACK), lambda b, i: (b, 0), pipeline_mode=pl.Buffered(1))],
        out_specs=pl.BlockSpec((tqa * heads, DSA_KV_RANK), lambda b, i: (b * qpb + i, 0)),
        out_shape=jax.ShapeDtypeStruct((n * heads, DSA_KV_RANK), BF16),
        scratch_shapes=[pltpu.VMEM((topk, LAT_PACK), jnp.uint32)],
        compiler_params=_cparams("parallel", "arbitrary"),
        name="dsa_attend",
    )(sel, sel, q2, table)


def _head_mm_body(a_ref, w_ref, o_ref):
    o_ref[0] = jnp.dot(a_ref[...].astype(BF16), w_ref[0], preferred_element_type=F32).astype(o_ref.dtype)


def head_mm_out(a, w, *, tm=512):
    n = a.shape[0]
    heads, k, m = w.shape
    tm = min(tm, n)
    return pl.pallas_call(
        _head_mm_body,
        grid=(n // tm, heads),
        in_specs=[pl.BlockSpec((tm, k), lambda i, h: (i, h)), pl.BlockSpec((1, k, m), lambda i, h: (h, 0, 0))],
        out_specs=pl.BlockSpec((1, tm, m), lambda i, h: (h, i, 0)),
        out_shape=jax.ShapeDtypeStruct((heads, n, m), BF16),
        compiler_params=_cparams("parallel", "parallel"),
        name="head_mm_out",
    )(a, w)


def _head_mm_in_body(a_ref, w_ref, o_ref):
    o_ref[...] = jnp.dot(a_ref[0], w_ref[0], preferred_element_type=F32).astype(o_ref.dtype)


def head_mm_in(a, w, *, tm=512):
    heads, n, k = a.shape
    m = w.shape[2]
    tm = min(tm, n)
    return pl.pallas_call(
        _head_mm_in_body,
        grid=(n // tm, heads),
        in_specs=[pl.BlockSpec((1, tm, k), lambda i, h: (h, i, 0)), pl.BlockSpec((1, k, m), lambda i, h: (h, 0, 0))],
        out_specs=pl.BlockSpec((tm, m), lambda i, h: (i, h)),
        out_shape=jax.ShapeDtypeStruct((n, heads * m), BF16),
        compiler_params=_cparams("parallel", "parallel"),
        name="head_mm_in",
    )(a, w)


def dsa_layer(hbf, positions, p, *, batch, seq):
    n = hbf.shape[0]
    heads = DSA_HEADS
    q = mm(hbf, p["w_q"])
    ckv = mm(hbf, p["w_ckv"])
    qidx = mm(hbf, p["w_qidx"])
    small = mm(hbf, p["w_small"])
    qr, table, qi, small_o, kx = rowwise(
        _dsa_prep_fn, [q, ckv, qidx, small, positions.reshape(n, 1)], [],
        [p["kv_norm"], p["idx_ln_g"], p["idx_ln_b"], _dsa_patterns()],
        [(q.shape[1], BF16), (LAT_PACK, jnp.uint32), (qidx.shape[1], BF16), (LANES, F32), (LANES, BF16)],
        seq=seq, tm=128, name="dsa_prep")
    sel = dsa_select(qi, small_o, kx, batch=batch, seq=seq)
    q2 = head_mm_out(qr, p["w_q2"])
    q2 = q2.transpose(1, 0, 2).reshape(n * heads, LAT_W)
    o_lat = dsa_attend(sel, q2, table, batch=batch, seq=seq)
    o_lat = o_lat.reshape(n, heads, DSA_KV_RANK).transpose(1, 0, 2)
    out = head_mm_in(o_lat, p["w_uv"])
    return mm(out, p["w_o"])


def _s5_tables(p):
    hp = lax.Precision.HIGHEST
    lc, hg = S5_CHUNK, S5_GROUP
    lr, li = p["lam_re"], p["lam_im"]
    g, ps = lr.shape
    dt = jnp.exp(p["log_dt"])[:, None]
    mag = jnp.exp(lr * dt)
    ab_re, ab_im = mag * jnp.cos(li * dt), mag * jnp.sin(li * dt)
    den = lr * lr + li * li
    nr, ni = ab_re - 1.0, ab_im
    coef_re = (nr * lr + ni * li) / den
    coef_im = (ni * lr - nr * li) / den
    bb_re = coef_re[..., None] * p["b_re"] - coef_im[..., None] * p["b_im"]
    bb_im = coef_re[..., None] * p["b_im"] + coef_im[..., None] * p["b_re"]
    tau = jnp.arange(lc + 1, dtype=F32)[:, None, None]
    pmag = jnp.exp(lr * dt * tau)
    pr, pi = pmag * jnp.cos(li * dt * tau), pmag * jnp.sin(li * dt * tau)
    cr, ci = p["c_re"][None], p["c_im"][None]
    car = cr * pr[:, :, None, :] - ci * pi[:, :, None, :]
    cai = cr * pi[:, :, None, :] + ci * pr[:, :, None, :]
    kern = (jnp.einsum('tghp,gpk->tghk', car[:lc], bb_re, precision=hp)
            - jnp.einsum('tghp,gpk->tghk', cai[:lc], bb_im, precision=hp))
    kz = jnp.concatenate([kern, jnp.zeros((1,) + kern.shape[1:], F32)], 0)
    s_i = np.arange(lc)[:, None]
    t_i = np.arange(lc)[None, :]
    m = kz[np.where(t_i >= s_i, t_i - s_i, lc)]
    m = m.transpose(2, 0, 4, 1, 3).reshape(g, lc * hg, lc * hg)
    skip = jnp.tile(p["d"].reshape(g, 1, hg), (1, lc, 1)).reshape(g, 1, lc * hg)
    m = m + skip * jnp.eye(lc * hg, dtype=F32)[None]
    prs, pis = pr[lc - 1 - np.arange(lc)], pi[lc - 1 - np.arange(lc)]
    wre = prs[..., None] * bb_re[None] - pis[..., None] * bb_im[None]
    wim = prs[..., None] * bb_im[None] + pis[..., None] * bb_re[None]
    wre = wre.transpose(1, 0, 3, 2).reshape(g, lc * hg, ps)
    wim = wim.transpose(1, 0, 3, 2).reshape(g, lc * hg, ps)
    vre = car[1:].transpose(1, 3, 0, 2).reshape(g, ps, lc * hg)
    vim = (-cai[1:]).transpose(1, 3, 0, 2).reshape(g, ps, lc * hg)
    eye2 = jnp.eye(2, dtype=F32)[None, :, None, :, None]

    def pair(z):
        r, c = z.shape[1:]
        return (z.reshape(g // 2, 2, r, 1, c) * eye2).reshape(g // 2, 2 * r, 2 * c).astype(BF16)

    are = pr[lc].reshape(g // 2, 1, 2 * ps)
    aim = pi[lc].reshape(g // 2, 1, 2 * ps)
    return pair(m), pair(wre), pair(wim), pair(vre), pair(vim), are, aim


def _gelu_tanh(y):
    return 0.5 * y * (1.0 + jnp.tanh(math.sqrt(2.0 / math.pi) * (y + 0.044715 * (y * y * y))))


def _s5_body(u_ref, m_ref, wre_ref, wim_ref, vre_ref, vim_ref, are_ref, aim_ref, z_ref, xre, xim, sre, sim, *, pairs):
    nc = u_ref.shape[1]
    for q in range(pairs):
        u = u_ref[q]
        xre[q] = jnp.dot(u, wre_ref[q], preferred_element_type=F32)
        xim[q] = jnp.dot(u, wim_ref[q], preferred_element_type=F32)
    ar = [are_ref[q] for q in range(pairs)]
    ai = [aim_ref[q] for q in range(pairs)]

    def step(c, carry):
        new = []
        row = pl.ds(c, 1)
        for q in range(pairs):
            re, im = carry[2 * q], carry[2 * q + 1]
            sre[q, row, :] = re
            sim[q, row, :] = im
            new.append(ar[q] * re - ai[q] * im + xre[q, row, :])
            new.append(ar[q] * im + ai[q] * re + xim[q, row, :])
        return tuple(new)

    zero = jnp.zeros((1, are_ref.shape[-1]), F32)
    lax.fori_loop(0, nc, step, (zero,) * (2 * pairs), unroll=8)
    for q in range(pairs):
        y = (jnp.dot(u_ref[q], m_ref[q], preferred_element_type=F32)
             + jnp.dot(sre[q].astype(BF16), vre_ref[q], preferred_element_type=F32)
             + jnp.dot(sim[q].astype(BF16), vim_ref[q], preferred_element_type=F32))
        z_ref[q] = _gelu_tanh(y).astype(z_ref.dtype)


def s5_layer(hbf, p, *, batch, seq):
    n, D = hbf.shape
    lc, hg = S5_CHUNK, S5_GROUP
    g2 = D // hg // 2
    pw = 2 * lc * hg
    tabs = _s5_tables(p)
    nc = seq // lc
    u = hbf.reshape(n // lc, lc, g2, 2, hg).transpose(2, 0, 3, 1, 4).reshape(g2, n // lc, pw)
    pairs = S5_PAIRS_PER_STEP
    sw = 2 * S5_STATE
    wspec = lambda r, c: pl.BlockSpec((pairs, r, c), lambda i, b: (i, 0, 0))
    z = pl.pallas_call(
        functools.partial(_s5_body, pairs=pairs),
        grid=(g2 // pairs, batch),
        in_specs=[pl.BlockSpec((pairs, nc, pw), lambda i, b: (i, b, 0)),
                  wspec(pw, pw), wspec(pw, sw), wspec(pw, sw), wspec(sw, pw), wspec(sw, pw),
                  wspec(1, sw), wspec(1, sw)],
        out_specs=pl.BlockSpec((pairs, nc, pw), lambda i, b: (i, b, 0)),
        out_shape=jax.ShapeDtypeStruct((g2, n // lc, pw), BF16),
        scratch_shapes=[pltpu.VMEM((pairs, nc, sw), F32)] * 4,
        compiler_params=_cparams("parallel", "parallel"),
        name="s5_scan",
    )(u, *tabs)
    z = z.reshape(g2, n // lc, 2, lc, hg).transpose(1, 3, 0, 2, 4).reshape(n, D)
    zz = mm(z, p["w_glu"])
    return rowwise(lambda a: (a[:, :D] * jax.nn.sigmoid(a[:, D:]),), [zz], [], [], [(D, F32)], seq=seq, tm=256,
                   name="s5_glu")[0]


def _modulate(x, sc, sh, dtype, *, seq):
    d = x.shape[1]
    return rowwise(lambda xv, s, h: (xv * (1.0 + s) + h,), [x], [sc, sh], [], [(d, dtype)], seq=seq, tm=256,
                   name="modulate")[0]


def kernel(x, c, positions, ada_w, ada_b, ln_g, ln_b, router_w, router_bias, moe_w_gate, moe_w_up, moe_w_down,
           rwkv_mu, rwkv_w_rkv, rwkv_w_o, rwkv_w0, rwkv_w1, rwkv_w2, rwkv_a0, rwkv_a1, rwkv_a2, rwkv_g1, rwkv_g2,
           rwkv_k_k, rwkv_k_a, rwkv_r_k, rwkv_gn_g, rwkv_gn_b, rwkv_v0, rwkv_v1, rwkv_v2, dsa_w_in, dsa_kv_norm,
           dsa_w_uk, dsa_w_uv, dsa_idx_ln_g, dsa_idx_ln_b, dsa_w_o, s5_lam_re, s5_lam_im, s5_log_dt, s5_b_re,
           s5_b_im, s5_c_re, s5_c_im, s5_d, s5_w_glu):
    batch, seq, d = x.shape
    depth = ada_w.shape[0]
    n = batch * seq
    xf = x.reshape(n, d)
    mod = ada_mod(c, ada_w, ada_b)[:, :batch].reshape(depth, batch, 1, 6, d)
    v_first = None
    for i in range(depth):
        kind, j = i % N_MIXERS, i // N_MIXERS
        sh1, sc1, g1, sh2, sc2, g2 = (mod[i, :, :, m] for m in range(6))
        if kind == 0:
            p = dict(mu=rwkv_mu[j], w_r=rwkv_w_rkv[j, 0].astype(BF16), w_k=rwkv_w_rkv[j, 1].astype(BF16),
                     w_v=rwkv_w_rkv[j, 2].astype(BF16), w_o=rwkv_w_o[j].astype(BF16),
                     w0=rwkv_w0[j], w1=_wpad(rwkv_w1[j]), w2=_kpad(rwkv_w2[j]),
                     a0=rwkv_a0[j], a1=_wpad(rwkv_a1[j]), a2=_kpad(rwkv_a2[j]),
                     g1=_wpad(rwkv_g1[j]), g2=_kpad(rwkv_g2[j]),
                     k_k=rwkv_k_k[j], k_a=rwkv_k_a[j], r_k=rwkv_r_k[j], gn_g=rwkv_gn_g[j], gn_b=rwkv_gn_b[j],
                     v0=None)
            if j > 0:
                p.update(v0=rwkv_v0[j - 1], v1=_wpad(rwkv_v1[j - 1]), v2=_kpad(rwkv_v2[j - 1]))
            y, v_first = rwkv_layer(xf, sc1, sh1, v_first, p, batch=batch, seq=seq)
        elif kind == 1:
            w_in = dsa_w_in[j]
            hq = DSA_HEADS * DSA_HEAD_DIM
            o1 = hq + DSA_KV_RANK
            o2 = o1 + DSA_ROPE_DIM
            o3 = o2 + IDX_HEADS * IDX_DIM
            o4 = o3 + IDX_DIM
            gap = jnp.zeros((d, SM_R0 - SM_W0 - IDX_HEADS), F32)
            w_small = jnp.concatenate([w_in[:, o3:o4], w_in[:, o4:], gap, w_in[:, o1:o2]], axis=1)
            rope_pass = jnp.zeros((DSA_ROPE_DIM, LAT_W), F32).at[
                jnp.arange(DSA_ROPE_DIM), DSA_KV_RANK + SM_R0 + jnp.arange(DSA_ROPE_DIM)].set(1.0)
            w_uk_pad = jnp.pad(dsa_w_uk[j], ((0, 0), (0, 0), (0, LAT_W - DSA_KV_RANK)))
            w_q2 = jnp.concatenate([jnp.broadcast_to(rope_pass, (DSA_HEADS,) + rope_pass.shape), w_uk_pad], axis=1)
            lane_pad = lambda v: jnp.pad(v, (0, LANES - v.shape[0])).reshape(1, LANES)
            p = dict(w_q=w_in[:, :hq].astype(BF16), w_ckv=w_in[:, hq:o1].astype(BF16),
                     w_qidx=w_in[:, o2:o3].astype(BF16), w_small=w_small.astype(BF16),
                     kv_norm=dsa_kv_norm[j].reshape(1, -1), w_q2=w_q2.astype(BF16), w_uv=dsa_w_uv[j].astype(BF16),
                     idx_ln_g=lane_pad(dsa_idx_ln_g[j]), idx_ln_b=lane_pad(dsa_idx_ln_b[j]),
                     w_o=dsa_w_o[j].astype(BF16))
            hbf = _modulate(xf, sc1, sh1, BF16, seq=seq)
            y = dsa_layer(hbf, positions, p, batch=batch, seq=seq)
        else:
            p = dict(lam_re=s5_lam_re[j], lam_im=s5_lam_im[j], log_dt=s5_log_dt[j], b_re=s5_b_re[j],
                     b_im=s5_b_im[j], c_re=s5_c_re[j], c_im=s5_c_im[j], d=s5_d[j], w_glu=s5_w_glu[j].astype(BF16))
            hbf = _modulate(xf, sc1, sh1, BF16, seq=seq)
            y = s5_layer(hbf, p, batch=batch, seq=seq)
        xf, hbf = post_norm_mix(xf, y, g1, sc2, sh2, ln_g[i, 0:1], ln_b[i, 0:1], seq=seq)
        y0, y1, gates = moe_ffn(xf, hbf, sc2, sh2, router_w, router_bias, moe_w_gate[i].astype(BF16),
                                moe_w_up[i].astype(BF16), moe_w_down[i].astype(BF16), seq=seq)
        xf = post_norm_moe(xf, y0, y1, gates, g2, ln_g[i, 1:2], ln_b[i, 1:2], seq=seq)
    return xf.reshape(batch, seq, d)
```

```python
import functools
import math

import numpy as np
import jax
import jax.numpy as jnp
from jax import lax
from jax.experimental import pallas as pl
from jax.experimental.pallas import tpu as pltpu

F32, BF16 = jnp.float32, jnp.bfloat16

DEPTH = 4
N_MIXERS = 3
RWKV_HEAD = 64
RWKV_GN_EPS = 64e-5
DSA_HEADS = 16
DSA_HEAD_DIM = 128
DSA_ROPE_DIM = 32
DSA_NOPE_DIM = 96
DSA_V_DIM = 128
DSA_KV_RANK = 512
IDX_HEADS = 16
IDX_DIM = 64
IDX_ROPE_DIM = 16
TOPK_MAX = 256
Q_BLOCK = 128
S5_GROUP = 16
S5_STATE = 64
S5_CHUNK = 16
S5_PAIRS_PER_STEP = 4
N_EXPERTS = 32
N_GROUPS = 4
EXPERTS_PER_GROUP = 8
TOP_K = 2
ROPE_THETA = 500000.0
LN_EPS = 1e-5
DEEPNORM_ALPHA = (2 * DEPTH) ** 0.25

LANES = 128
VMEM_LIMIT = 48 * 1024 * 1024
MOE_ROWS = 256
WKV_CHUNK = 64
WKV_HEADS_PER_STEP = 4
WKV_ROWS_PER_STEP = 512


def _cparams(*sem):
    return pltpu.CompilerParams(dimension_semantics=sem, vmem_limit_bytes=VMEM_LIMIT)


def _pick(n, cands):
    for c in cands:
        if n % c == 0:
            return c
    return n


def _mm_body(a_ref, w_ref, o_ref, *, act):
    acc = jnp.dot(a_ref[...].astype(BF16), w_ref[...], preferred_element_type=F32)
    if act == "tanh":
        acc = jnp.tanh(acc)
    elif act == "sigmoid":
        acc = jax.nn.sigmoid(acc)
    o_ref[...] = acc.astype(o_ref.dtype)


def mm(a, w, *, out_dtype=F32, act=None, tm=512):
    m, k = a.shape
    n = w.shape[1]
    tm = _pick(m, (tm, 256, 128, 64, 32, 16, 8))
    tn = _pick(n, (512, 256, 128))
    return pl.pallas_call(
        functools.partial(_mm_body, act=act),
        grid=(m // tm, n // tn),
        in_specs=[pl.BlockSpec((tm, k), lambda i, j: (i, 0)), pl.BlockSpec((k, tn), lambda i, j: (0, j))],
        out_specs=pl.BlockSpec((tm, tn), lambda i, j: (i, j)),
        out_shape=jax.ShapeDtypeStruct((m, n), out_dtype),
        compiler_params=_cparams("parallel", "parallel"),
        name="mm",
    )(a, w)


def _wpad(w):
    n = w.shape[1]
    npad = -(-n // LANES) * LANES
    w = w.astype(BF16)
    return w if npad == n else jnp.pad(w, ((0, 0), (0, npad - n)))


def _kpad(w):
    k = w.shape[0]
    kpad = -(-k // LANES) * LANES
    w = w.astype(BF16)
    return w if kpad == k else jnp.pad(w, ((0, kpad - k), (0, 0)))


def rowwise(fn, rows, perbatch, consts, outs, *, seq, tm, name):
    n = rows[0].shape[0]
    tm = min(tm, seq)
    tpb = seq // tm
    nr, nb, nc = len(rows), len(perbatch), len(consts)

    def body(*refs):
        vals = [r[...] for r in refs[:nr]]
        vals += [r[0] for r in refs[nr:nr + nb]]
        vals += [r[...] for r in refs[nr + nb:nr + nb + nc]]
        res = fn(*vals)
        for o, v in zip(refs[nr + nb + nc:], res):
            o[...] = v.astype(o.dtype)

    in_specs = [pl.BlockSpec((tm, r.shape[1]), lambda i: (i, 0)) for r in rows]
    in_specs += [pl.BlockSpec((1, 1, p.shape[-1]), lambda i: (i // tpb, 0, 0)) for p in perbatch]
    in_specs += [pl.BlockSpec(c.shape, lambda i: (0, 0)) for c in consts]
    res = pl.pallas_call(
        body,
        grid=(n // tm,),
        in_specs=in_specs,
        out_specs=[pl.BlockSpec((tm, w), lambda i: (i, 0)) for w, _ in outs],
        out_shape=[jax.ShapeDtypeStruct((n, w), dt) for w, dt in outs],
        compiler_params=_cparams("parallel"),
        name=name,
    )(*rows, *perbatch, *consts)
    return res


def _layer_norm(z, g, b):
    mu = jnp.mean(z, -1, keepdims=True)
    zc = z - mu
    var = jnp.mean(zc * zc, -1, keepdims=True)
    return zc * lax.rsqrt(var + LN_EPS) * g + b


def _split_dot(x, m):
    xh = x.astype(BF16)
    xl = (x - xh.astype(F32)).astype(BF16)
    return jnp.dot(xh, m, preferred_element_type=F32) + jnp.dot(xl, m, preferred_element_type=F32)


def _seg_sum(x, ones_bd):
    parts = [_split_dot(x[:, j:j + LANES], ones_bd) for j in range(0, x.shape[1], LANES)]
    return jnp.concatenate(parts, -1)


def _head_ones():
    i = np.arange(LANES)
    return jnp.asarray((i[:, None] // RWKV_HEAD) == (i[None, :] // RWKV_HEAD), BF16)


def _ada_body(c_ref, w_ref, b_ref, o_ref):
    acc = jnp.dot(c_ref[...], w_ref[0], preferred_element_type=F32, precision=lax.Precision.HIGHEST)
    o_ref[0] = acc + b_ref[0]


def ada_mod(c, ada_w, ada_b):
    depth, d, n = ada_w.shape
    b = c.shape[0]
    cp = jnp.pad(c, ((0, 8 - b), (0, 0)))
    tn = 1024
    return pl.pallas_call(
        _ada_body,
        grid=(depth, n // tn),
        in_specs=[pl.BlockSpec((8, d), lambda i, j: (0, 0)),
                  pl.BlockSpec((1, d, tn), lambda i, j: (i, 0, j)),
                  pl.BlockSpec((1, 1, tn), lambda i, j: (i, 0, j))],
        out_specs=pl.BlockSpec((1, 8, tn), lambda i, j: (i, 0, j)),
        out_shape=jax.ShapeDtypeStruct((depth, 8, n), F32),
        compiler_params=_cparams("parallel", "parallel"),
        name="ada_mod",
    )(cp, ada_w, ada_b.reshape(depth, 1, n))


def post_norm_mix(x, y, gate, sc, sh, lng, lnb, *, seq):
    d = x.shape[1]

    def fn(xv, yv, g, s, h, lg, lb):
        xn = _layer_norm(DEEPNORM_ALPHA * xv + (1.0 + g) * yv, lg, lb)
        return xn, xn * (1.0 + s) + h

    return rowwise(fn, [x, y], [gate, sc, sh], [lng, lnb], [(d, F32), (d, BF16)], seq=seq, tm=256, name="post_norm_mix")


def post_norm_moe(x, y0, y1, gates, gate, lng, lnb, *, seq):
    d = x.shape[1]

    def fn(xv, a, b, gt, g, lg, lb):
        y = a * gt[:, 0:1] + b * gt[:, 1:2]
        return (_layer_norm(DEEPNORM_ALPHA * xv + (1.0 + g) * y, lg, lb),)

    return rowwise(fn, [x, y0, y1, gates], [gate], [lng, lnb], [(d, F32)], seq=seq, tm=256, name="post_norm_moe")[0]


def _router_fn(xv, s, h, rw, rb):
    hf = xv * (1.0 + s) + h
    logits = jnp.dot(hf, rw, preferred_element_type=F32, precision=lax.Precision.HIGHEST)
    scores = jax.nn.sigmoid(logits)
    biased = scores + rb
    lane = lax.broadcasted_iota(jnp.int32, biased.shape, 1)
    neg = jnp.float32(-jnp.inf)
    big = jnp.int32(1 << 20)
    best = bi1 = bi2 = None
    for g in range(N_GROUPS):
        ing = (lane >= g * EXPERTS_PER_GROUP) & (lane < (g + 1) * EXPERTS_PER_GROUP)
        v = jnp.where(ing, biased, neg)
        m1 = jnp.max(v, -1, keepdims=True)
        i1 = jnp.min(jnp.where(v == m1, lane, big), -1, keepdims=True)
        v2 = jnp.where(lane == i1, neg, v)
        m2 = jnp.max(v2, -1, keepdims=True)
        i2 = jnp.min(jnp.where(v2 == m2, lane, big), -1, keepdims=True)
        gs = m1 + m2
        if g == 0:
            best, bi1, bi2 = gs, i1, i2
        else:
            better = gs > best
            best = jnp.where(better, gs, best)
            bi1 = jnp.where(better, i1, bi1)
            bi2 = jnp.where(better, i2, bi2)
    g1 = jnp.sum(jnp.where(lane == bi1, scores, 0.0), -1, keepdims=True)
    g2 = jnp.sum(jnp.where(lane == bi2, scores, 0.0), -1, keepdims=True)
    tot = g1 + g2
    two = lax.broadcasted_iota(jnp.int32, (biased.shape[0], TOP_K), 1)
    return jnp.where(two == 0, bi1, bi2), jnp.where(two == 0, g1 / tot, g2 / tot)


def _expert_body(be_ref, nb_ref, x_ref, wg_ref, wu_ref, wd_ref, o_ref):
    del be_ref

    @pl.when(pl.program_id(0) < nb_ref[0])
    def _():
        x = x_ref[...]
        g = jnp.dot(x, wg_ref[0], preferred_element_type=F32)
        u = jnp.dot(x, wu_ref[0], preferred_element_type=F32)
        hid = (g * jax.nn.sigmoid(g) * u).astype(BF16)
        o_ref[...] = jnp.dot(hid, wd_ref[0], preferred_element_type=F32)

    @pl.when(pl.program_id(0) >= nb_ref[0])
    def _():
        o_ref[...] = jnp.zeros_like(o_ref)


def moe_ffn(x, hbf, sc, sh, router_w, router_bias, wg, wu, wd, *, seq):
    n, d = x.shape
    e = router_w.shape[1]
    idx, gates = rowwise(_router_fn, [x], [sc, sh], [router_w, router_bias.reshape(1, e)],
                         [(TOP_K, jnp.int32), (TOP_K, F32)], seq=seq, tm=256, name="router")
    bm = MOE_ROWS
    n_slots = n * TOP_K
    flat_e = idx.reshape(-1)
    onehot = (flat_e[:, None] == jnp.arange(e, dtype=jnp.int32)[None, :]).astype(jnp.int32)
    csum = jnp.cumsum(onehot, axis=0)
    pos = jnp.sum(onehot * csum, axis=1) - 1
    counts = csum[-1]
    padded = (counts + bm - 1) // bm * bm
    pend = jnp.cumsum(padded)
    pstart = pend - padded
    dest = pstart[flat_e] + pos
    n_blocks = n_slots // bm + e
    n_pad = n_blocks * bm
    slot_tok = jnp.zeros((n_pad,), jnp.int32).at[dest].set(jnp.arange(n_slots, dtype=jnp.int32) // TOP_K)
    block_start = jnp.arange(n_blocks, dtype=jnp.int32) * bm
    block_expert = jnp.minimum(jnp.sum((pend[None, :] <= block_start[:, None]).astype(jnp.int32), axis=1), e - 1)
    used_blocks = (pend[-1] // bm).astype(jnp.int32).reshape(1)
    xs = jnp.take(hbf, slot_tok, axis=0)
    de = wg.shape[2]
    yb = pl.pallas_call(
        _expert_body,
        grid_spec=pltpu.PrefetchScalarGridSpec(
            num_scalar_prefetch=2,
            grid=(n_blocks,),
            in_specs=[pl.BlockSpec((bm, d), lambda i, be, nb: (i, 0)),
                      pl.BlockSpec((1, d, de), lambda i, be, nb: (be[i], 0, 0)),
                      pl.BlockSpec((1, d, de), lambda i, be, nb: (be[i], 0, 0)),
                      pl.BlockSpec((1, de, d), lambda i, be, nb: (be[i], 0, 0))],
            out_specs=pl.BlockSpec((bm, d), lambda i, be, nb: (i, 0)),
        ),
        out_shape=jax.ShapeDtypeStruct((n_pad, d), F32),
        compiler_params=_cparams("arbitrary"),
        name="moe_experts",
    )(block_expert, used_blocks, xs, wg, wu, wd)
    dest2 = dest.reshape(n, TOP_K)
    return jnp.take(yb, dest2[:, 0], axis=0), jnp.take(yb, dest2[:, 1], axis=0), gates


def _rwkv_prep_body(x_ref, xp_ref, sc_ref, sh_ref, mu_ref, *outs, tpb):
    sc = 1.0 + sc_ref[0]
    sh = sh_ref[0]
    h = x_ref[...] * sc + sh
    prev_row = xp_ref[7:8, :] * sc + sh
    prev_row = jnp.where(pl.program_id(0) % tpb == 0, 0.0, prev_row)
    rowid = lax.broadcasted_iota(jnp.int32, h.shape, 0)
    hprev = jnp.where(rowid == 0, prev_row, pltpu.roll(h, 1, 0))
    dx = hprev - h
    for m, o in enumerate(outs):
        o[...] = (h + dx * mu_ref[m:m + 1, :]).astype(o.dtype)


def rwkv_prep(x, sc, sh, mu, *, seq):
    n, d = x.shape
    tm = min(256, seq)
    tpb = seq // tm
    return pl.pallas_call(
        functools.partial(_rwkv_prep_body, tpb=tpb),
        grid=(n // tm,),
        in_specs=[pl.BlockSpec((tm, d), lambda i: (i, 0)),
                  pl.BlockSpec((8, d), lambda i: (jnp.maximum(i * (tm // 8) - 1, 0), 0)),
                  pl.BlockSpec((1, 1, d), lambda i: (i // tpb, 0, 0)),
                  pl.BlockSpec((1, 1, d), lambda i: (i // tpb, 0, 0)),
                  pl.BlockSpec((6, d), lambda i: (0, 0))],
        out_specs=[pl.BlockSpec((tm, d), lambda i: (i, 0))] * 6,
        out_shape=[jax.ShapeDtypeStruct((n, d), BF16)] * 6,
        compiler_params=_cparams("parallel"),
        name="rwkv_prep",
    )(x, x, sc, sh, mu)


def _softplus(z):
    return jnp.maximum(z, 0.0) + jnp.log(1.0 + jnp.exp(-jnp.abs(z)))


def _rwkv_gates_fn(k0, lw, al, *rest, has_vres):
    if has_vres:
        v, vfirst, vl, prm, ones_bd = rest
    else:
        prm, ones_bd = rest
    w0, a0, k_k, k_a = prm[0:1], prm[1:2], prm[2:3], prm[3:4]
    log_w = -_softplus(-(w0 + lw)) - 0.5
    logdecay = -jnp.exp(log_w)
    a = jax.nn.sigmoid(a0 + al)
    kk = k0 * k_k
    nrm = jnp.maximum(jnp.sqrt(_seg_sum(kk * kk, ones_bd)), 1e-12)
    kk = kk / nrm
    k = k0 * (1.0 + (a - 1.0) * k_a)
    res = [logdecay, k, kk, kk * a]
    if has_vres:
        res.append(v + (vfirst - v) * jax.nn.sigmoid(prm[4:5] + vl))
    return res


def _wkv_body(r_ref, lw_ref, k_ref, v_ref, kk_ref, b_ref, y_ref, s_ref, *, chunk, heads):
    hd = RWKV_HEAD
    rows = r_ref.shape[0]
    width = heads * hd
    hr = heads * chunk
    nt = (((1,), (1,)), ((), ()))

    @pl.when(pl.program_id(2) == 0)
    def _():
        s_ref[...] = jnp.zeros_like(s_ref)

    ti = lax.broadcasted_iota(jnp.int32, (chunk, chunk), 0)
    si = lax.broadcasted_iota(jnp.int32, (chunk, chunk), 1)
    tri = (ti >= si).astype(BF16)
    ri = lax.broadcasted_iota(jnp.int32, (hr, width), 0)
    ci = lax.broadcasted_iota(jnp.int32, (hr, width), 1)
    head_mask = (ri // chunk == ci // hd).astype(F32)
    rr = lax.broadcasted_iota(jnp.int32, (hr, hr), 0)
    cc = lax.broadcasted_iota(jnp.int32, (hr, hr), 1)
    incl = rr % chunk >= cc % chunk
    strict = rr % chunk > cc % chunk
    eye = (rr == cc).astype(F32)

    def blocked(x):
        return (jnp.concatenate([x] * heads, 0) * head_mask).astype(BF16)

    def step(c, carry):
        sl = pl.ds(pl.multiple_of(c * chunk, chunk), chunk)
        lw = lw_ref[sl, :]
        cs = _split_dot_left(tri, lw)
        ctot = cs[chunk - 1:chunk, :]
        g_inv = jnp.exp(-cs)
        g_end = jnp.exp(ctot - cs)
        kk = kk_ref[sl, :]
        kv = k_ref[sl, :]
        bv = b_ref[sl, :]
        ar = jnp.concatenate([blocked(-kk * jnp.exp(cs - lw)), blocked(r_ref[sl, :] * jnp.exp(cs))], 0)
        bk = jnp.concatenate([blocked(bv * g_inv), blocked(kv * g_inv)], 0)
        ends = jnp.concatenate([blocked(bv * g_end), blocked(kv * g_end)], 0)
        vb = blocked(v_ref[sl, :])
        s_old = s_ref[...]
        m = lax.dot_general(ar, bk, nt, preferred_element_type=F32)
        a_ab = jnp.where(strict, m[:hr, :hr], 0.0)
        a_ak = jnp.where(strict, m[:hr, hr:], 0.0)
        a_r = jnp.concatenate([jnp.where(incl, m[hr:, :hr], 0.0), jnp.where(incl, m[hr:, hr:], 0.0)], 1)
        ars = lax.dot_general(ar, s_old.astype(BF16), nt, preferred_element_type=F32)
        rhs = ars[:hr] + jnp.dot(a_ak.astype(BF16), vb, preferred_element_type=F32)
        inv = eye + a_ab
        apow = a_ab.astype(BF16)
        for _ in range(int(math.log2(chunk)) - 1):
            apow = jnp.dot(apow, apow, preferred_element_type=F32).astype(BF16)
            inv = inv + jnp.dot(inv.astype(BF16), apow, preferred_element_type=F32)
        u = jnp.dot(inv.astype(BF16), rhs.astype(BF16), preferred_element_type=F32)
        uv = jnp.concatenate([u.astype(BF16), vb], 0)
        yb = ars[hr:] + jnp.dot(a_r.astype(BF16), uv, preferred_element_type=F32)
        y = yb[:chunk]
        for h in range(1, heads):
            y = y + yb[h * chunk:(h + 1) * chunk]
        y_ref[sl, :] = y
        s_ref[...] = s_old * jnp.exp(ctot) + lax.dot_general(
            uv, ends, (((0,), (0,)), ((), ())), preferred_element_type=F32)
        return carry

    lax.fori_loop(0, rows // chunk, step, 0)


def _split_dot_left(m, x):
    xh = x.astype(BF16)
    xl = (x - xh.astype(F32)).astype(BF16)
    return jnp.dot(m, xh, preferred_element_type=F32) + jnp.dot(m, xl, preferred_element_type=F32)


def wkv7(r, logdecay, k, v, kk, b, *, batch, seq):
    n, d = r.shape
    chunk = min(WKV_CHUNK, seq)
    rows = min(WKV_ROWS_PER_STEP, seq)
    heads = WKV_HEADS_PER_STEP
    width = heads * RWKV_HEAD
    spb = seq // rows
    spec = pl.BlockSpec((rows, width), lambda bi, hi, ti: (bi * spb + ti, hi))
    return pl.pallas_call(
        functools.partial(_wkv_body, chunk=chunk, heads=heads),
        grid=(batch, d // width, spb),
        in_specs=[spec] * 6,
        out_specs=spec,
        out_shape=jax.ShapeDtypeStruct((n, d), F32),
        scratch_shapes=[pltpu.VMEM((width, width), F32)],
        compiler_params=_cparams("parallel", "parallel", "arbitrary"),
        name="wkv7",
    )(r, logdecay, k, v, kk, b)


def _rwkv_out_fn(y, r, k, v, g, prm, ones_bd):
    r_k, gn_g, gn_b = prm[0:1], prm[1:2], prm[2:3]
    inv_n = 1.0 / RWKV_HEAD
    m_y = _seg_sum(y, ones_bd) * inv_n
    yc = y - m_y
    v_y = _seg_sum(yc * yc, ones_bd) * inv_n
    yn = yc * lax.rsqrt(v_y + RWKV_GN_EPS) * gn_g + gn_b
    bonus = _seg_sum(r * k * r_k, ones_bd) * v
    return ((yn + bonus) * g,)


def rwkv_layer(x, sc, sh, v_first, p, *, batch, seq):
    n, d = x.shape
    xr, xw, xk, xv, xa, xg = rwkv_prep(x, sc, sh, p["mu"], seq=seq)
    r = mm(xr, p["w_r"])
    k0 = mm(xk, p["w_k"])
    v = mm(xv, p["w_v"])
    lw = mm(mm(xw, p["w1"], out_dtype=BF16, act="tanh"), p["w2"])
    al = mm(mm(xa, p["a1"], out_dtype=BF16), p["a2"])
    g = mm(mm(xg, p["g1"], out_dtype=BF16, act="sigmoid"), p["g2"])
    ones_bd = _head_ones()
    has_vres = p["v0"] is not None
    outs = [(d, F32)] * (5 if has_vres else 4)
    if has_vres:
        vl = mm(mm(xv, p["v1"], out_dtype=BF16), p["v2"])
        prm = jnp.stack([p["w0"], p["a0"], p["k_k"], p["k_a"], p["v0"]])
        rows = [k0, lw, al, v, v_first, vl]
    else:
        prm = jnp.stack([p["w0"], p["a0"], p["k_k"], p["k_a"]])
        rows = [k0, lw, al]
    res = rowwise(functools.partial(_rwkv_gates_fn, has_vres=has_vres), rows, [], [prm, ones_bd], outs,
                  seq=seq, tm=128, name="rwkv_gates")
    logdecay, k, kk, b = res[:4]
    if has_vres:
        v = res[4]
    else:
        v_first = v
    y = wkv7(r, logdecay, k, v, kk, b, batch=batch, seq=seq)
    prm2 = jnp.stack([p["r_k"], p["gn_g"], p["gn_b"]])
    yg = rowwise(_rwkv_out_fn, [y, r, k, v, g], [], [prm2, ones_bd], [(d, BF16)], seq=seq, tm=128, name="rwkv_out")[0]
    return mm(yg, p["w_o"]), v_first


SM_W0 = IDX_DIM
SM_R0 = LANES - DSA_ROPE_DIM
LAT_W = 768
LAT_PACK = LAT_W // 2


def _dsa_patterns():
    fq = ROPE_THETA ** (-np.arange(DSA_ROPE_DIM // 2, dtype=np.float32) * np.float32(2.0 / DSA_ROPE_DIM))
    fi = ROPE_THETA ** (-np.arange(IDX_ROPE_DIM // 2, dtype=np.float32) * np.float32(2.0 / IDX_ROPE_DIM))
    hq, hi = DSA_ROPE_DIM // 2, IDX_ROPE_DIM // 2
    pat = np.zeros((16, LANES), np.float32)
    lane = np.arange(LANES)
    pat[0, :hq] = fq; pat[0, hq:2 * hq] = fq
    pat[1, :hq] = -1.0; pat[2, hq:2 * hq] = 1.0
    l64 = lane % IDX_DIM
    pat[3] = np.where(l64 < hi, fi[np.minimum(l64, hi - 1)], np.where(l64 < 2 * hi, fi[np.clip(l64 - hi, 0, hi - 1)], 0.0))
    pat[4] = np.where(l64 < hi, -1.0, 0.0); pat[5] = np.where((l64 >= hi) & (l64 < 2 * hi), 1.0, 0.0)
    pat[6, :hi] = fi; pat[6, hi:2 * hi] = fi
    pat[6, SM_R0:SM_R0 + hq] = fq; pat[6, SM_R0 + hq:] = fq
    pat[7, :hi] = -1.0; pat[8, hi:2 * hi] = 1.0
    pat[9, SM_R0:SM_R0 + hq] = -1.0; pat[10, SM_R0 + hq:] = 1.0
    pat[11, :IDX_DIM] = 1.0
    pat[12, SM_W0:SM_W0 + IDX_HEADS] = IDX_HEADS ** -0.5 * IDX_DIM ** -0.5
    pat[13, SM_R0:] = 1.0
    return jnp.asarray(pat)


def _rot(x, shift):
    return pltpu.roll(x, shift % x.shape[1], 1)


def _tile_lanes(v, reps):
    return jnp.concatenate([v] * reps, 1) if reps > 1 else v


def _dsa_prep_fn(q, ckv, qidx, small, pos, kvn, lng, lnb, pat):
    posf = pos.astype(F32)
    hq, hi = DSA_ROPE_DIM // 2, IDX_ROPE_DIM // 2
    ang = posf * pat[0:1]
    c, s = jnp.cos(ang), jnp.sin(ang)
    reps = q.shape[1] // LANES
    qr = (q * _tile_lanes(c, reps) + _rot(q, -hq) * _tile_lanes(s * pat[1:2], reps)
          + _rot(q, hq) * _tile_lanes(s * pat[2:3], reps))
    ang = posf * pat[3:4]
    c, s = jnp.cos(ang), jnp.sin(ang)
    reps = qidx.shape[1] // LANES
    qi = (qidx * _tile_lanes(c, reps) + _rot(qidx, -hi) * _tile_lanes(s * pat[4:5], reps)
          + _rot(qidx, hi) * _tile_lanes(s * pat[5:6], reps))
    ckn = ckv * lax.rsqrt(jnp.mean(ckv * ckv, -1, keepdims=True) + 1e-6) * kvn
    mk = pat[11:12]
    inv = 1.0 / IDX_DIM
    mu = jnp.sum(small * mk, -1, keepdims=True) * inv
    dv = (small - mu) * mk
    var = jnp.sum(dv * dv, -1, keepdims=True) * inv
    y = (dv * lax.rsqrt(var + LN_EPS) * lng + lnb) * mk + small * pat[12:13] + small * pat[13:14]
    ang = posf * pat[6:7]
    c, s = jnp.cos(ang), jnp.sin(ang)
    so = (y * c + _rot(y, -hi) * (s * pat[7:8]) + _rot(y, hi) * (s * pat[8:9])
          + _rot(y, -hq) * (s * pat[9:10]) + _rot(y, hq) * (s * pat[10:11]))
    lat = jnp.concatenate([ckn, so * pat[13:14], jnp.zeros_like(so)], 1)
    lo = lax.bitcast_convert_type(lat[:, :LAT_PACK].astype(BF16).astype(F32), jnp.uint32) >> 16
    hi_b = lax.bitcast_convert_type(lat[:, LAT_PACK:].astype(BF16).astype(F32), jnp.uint32) & jnp.uint32(0xFFFF0000)
    return qr, lo | hi_b, qi, so, so


def _sortable(x):
    b = lax.bitcast_convert_type(x, jnp.int32)
    return jnp.where(b < 0, b ^ jnp.int32(0x7FFFFFFF), b)


def _dsa_select_body(qi_ref, sm_ref, kx_ref, sel_ref, key_ref, ib_ref, *, tq, ts, topk, seq):
    sub = 32
    t0 = pl.program_id(1) * tq
    nchunk = (t0 + tq + ts - 1) // ts
    qpos = t0 + lax.broadcasted_iota(jnp.int32, (tq, 1), 0)
    w = sm_ref[:, SM_W0:SM_W0 + IDX_HEADS]
    neg = jnp.float32(-jnp.inf)

    def score_chunk(c, carry):
        off = pl.multiple_of(c * ts, ts)
        ks = kx_ref[pl.ds(off, ts), :][:, :IDX_DIM]
        acc = jnp.zeros((tq, ts), F32)
        for h in range(IDX_HEADS):
            s = lax.dot_general(qi_ref[:, h * IDX_DIM:(h + 1) * IDX_DIM], ks, (((1,), (1,)), ((), ())),
                                preferred_element_type=F32)
            acc = acc + w[:, h:h + 1] * jnp.maximum(s, 0.0)
        kpos = off + lax.broadcasted_iota(jnp.int32, (1, ts), 1)
        key_ref[:, pl.ds(off, ts)] = _sortable(jnp.where(kpos <= qpos, acc, neg))
        return carry

    lax.fori_loop(0, nchunk, score_chunk, 0)

    tpc = ts // LANES
    lane_iota = lax.broadcasted_iota(jnp.int32, (1, LANES), 1)

    def count(pred):
        def body(c, acc):
            for u in range(tpc):
                off = pl.multiple_of(c * ts + u * LANES, LANES)
                acc = acc + pred(key_ref[:, pl.ds(off, LANES)], off + lane_iota).astype(jnp.int32)
            return acc
        acc = lax.fori_loop(0, nchunk, body, jnp.zeros((tq, LANES), jnp.int32))
        return jnp.sum(acc, -1, keepdims=True)

    sign = jnp.int32(-2 ** 31)
    thr_u = jnp.zeros((tq, 1), jnp.int32)
    for bit in range(31, -1, -1):
        cand_u = thr_u | jnp.int32(np.uint32(1 << bit).astype(np.int32))
        cand = cand_u ^ sign
        cnt = count(lambda k, kp, cand=cand: k >= cand)
        thr_u = jnp.where(cnt >= topk, cand_u, thr_u)
    thr = thr_u ^ sign
    need = topk - count(lambda k, kp: k > thr)
    n_eq = count(lambda k, kp: k == thr)

    ib_ref[...] = jnp.full((tq, 1), seq, jnp.int32)

    @pl.when(jnp.max(n_eq - need) > 0)
    def _():
        ib = jnp.zeros((tq, 1), jnp.int32)
        for bit in range(int(math.log2(seq)) - 1, -1, -1):
            cand = ib | jnp.int32(1 << bit)
            cnt = count(lambda k, kp, cand=cand: (k == thr) & (kp < cand))
            ib = jnp.where(cnt < need, cand, ib)
        ib_ref[...] = ib

    ib = ib_ref[...]

    def mask_tiles(c, acc):
        for u in range(tpc):
            off = pl.multiple_of(c * ts + u * LANES, LANES)
            kpos = off + lane_iota
            k = key_ref[:, pl.ds(off, LANES)]
            m = (((k > thr) | ((k == thr) & (kpos <= ib))) & (kpos <= qpos)).astype(jnp.int32)
            key_ref[:, pl.ds(off, LANES)] = m
            acc = acc + m
        return acc

    colcnt = lax.fori_loop(0, nchunk, mask_tiles, jnp.zeros((tq, LANES), jnp.int32))
    li = lax.broadcasted_iota(jnp.int32, (LANES, LANES), 0)
    lj = lax.broadcasted_iota(jnp.int32, (LANES, LANES), 1)
    cin = jnp.dot(colcnt.astype(BF16), (li <= lj).astype(BF16), preferred_element_type=F32).astype(jnp.int32)
    total = cin[:, LANES - 1:LANES]
    slot = lax.broadcasted_iota(jnp.int32, (tq, topk), 1)
    lane_of = jnp.zeros((tq, topk), jnp.int32)
    base = jnp.zeros((tq, topk), jnp.int32)
    for l in range(LANES):
        c = cin[:, l:l + 1]
        le = c <= slot
        lane_of = lane_of + le.astype(jnp.int32)
        base = jnp.maximum(base, jnp.where(le, c, 0))
    lane_of = jnp.minimum(lane_of, LANES - 1)
    rank = slot - base
    nhalf = topk // LANES
    for r0 in range(0, tq, sub):
        rows = slice(r0, r0 + sub)
        lanes_h = [lane_of[rows, i * LANES:(i + 1) * LANES] for i in range(nhalf)]
        ranks_h = [rank[rows, i * LANES:(i + 1) * LANES] for i in range(nhalf)]

        def tile_body(c, carry):
            carry = list(carry)
            for u in range(tpc):
                off = pl.multiple_of(c * ts + u * LANES, LANES)
                carry[0] = carry[0] + key_ref[r0:r0 + sub, pl.ds(off, LANES)]
                for i in range(nhalf):
                    got = jnp.take_along_axis(carry[0], lanes_h[i], axis=1)
                    carry[1 + i] = carry[1 + i] + (got <= ranks_h[i]).astype(jnp.int32)
            return tuple(carry)

        res = lax.fori_loop(0, nchunk, tile_body, (jnp.zeros((sub, LANES), jnp.int32),) * (1 + nhalf))
        pos = jnp.concatenate([res[1 + i] * LANES + lanes_h[i] for i in range(nhalf)], 1)
        slot_sub = lax.broadcasted_iota(jnp.int32, (sub, topk), 1)
        sel_ref[rows, :] = jnp.where(slot_sub < total[rows], pos, -1)


def dsa_select(qi, small, kx, *, batch, seq):
    n = qi.shape[0]
    topk = min(TOPK_MAX, seq // 4)
    tq = min(128, seq)
    ts = min(512, seq)
    qpb = seq // tq
    return pl.pallas_call(
        functools.partial(_dsa_select_body, tq=tq, ts=ts, topk=topk, seq=seq),
        grid=(batch, qpb),
        in_specs=[pl.BlockSpec((tq, qi.shape[1]), lambda b, i: (b * qpb + i, 0)),
                  pl.BlockSpec((tq, LANES), lambda b, i: (b * qpb + i, 0)),
                  pl.BlockSpec((seq, LANES), lambda b, i: (b, 0))],
        out_specs=pl.BlockSpec((tq, topk), lambda b, i: (b * qpb + i, 0)),
        out_shape=jax.ShapeDtypeStruct((n, topk), jnp.int32),
        scratch_shapes=[pltpu.VMEM((tq, seq), jnp.int32), pltpu.VMEM((tq, 1), jnp.int32)],
        compiler_params=_cparams("parallel", "arbitrary"),
        name="dsa_select",
    )(qi, small, kx)


def _dsa_attn_body(sel_s, selv_ref, q2_ref, tbl_ref, o_ref, stage, *, tqa, topk):
    heads = DSA_HEADS
    scale = DSA_HEAD_DIM ** -0.5

    def gather(i, buf):
        base = i * topk
        for j in range(topk):
            stage[buf, pl.ds(j, 1), :] = tbl_ref[pl.ds(sel_s[base + j], 1), :]

    def attend(i, buf):
        pk = stage[buf]
        lo = lax.bitcast_convert_type(pk << 16, F32)
        hi = lax.bitcast_convert_type(pk & jnp.uint32(0xFFFF0000), F32)
        kv = jnp.concatenate([lo, hi], 1).astype(BF16)
        q = q2_ref[pl.ds(pl.multiple_of(i * heads, heads), heads), :]
        s = lax.dot_general(q, kv, (((1,), (1,)), ((), ())), preferred_element_type=F32) * scale
        s = jnp.where(selv_ref[pl.ds(i, 1), :] >= 0, s, -jnp.inf)
        e = jnp.exp(s - jnp.max(s, -1, keepdims=True))
        p = e / jnp.sum(e, -1, keepdims=True)
        o = jnp.dot(p.astype(BF16), kv[:, :DSA_KV_RANK], preferred_element_type=F32)
        o_ref[pl.ds(pl.multiple_of(i * heads, heads), heads), :] = o.astype(o_ref.dtype)

    gather(0, 0)

    def pair(pi, carry):
        i0 = 2 * pi
        gather(i0 + 1, 1)
        attend(i0, 0)
        gather(jnp.minimum(i0 + 2, tqa - 1), 0)
        attend(i0 + 1, 1)
        return carry

    lax.fori_loop(0, tqa // 2, pair, 0)


def dsa_attend(sel, q2, table, *, batch, seq):
    n, topk = sel.shape
    tqa = min(64, seq)
    qpb = seq // tqa
    heads = DSA_HEADS
    return pl.pallas_call(
        functools.partial(_dsa_attn_body, tqa=tqa, topk=topk),
        grid=(batch, qpb),
        in_specs=[pl.BlockSpec((tqa * topk,), lambda b, i: (b * qpb + i,), memory_space=pltpu.SMEM),
                  pl.BlockSpec((tqa, topk), lambda b, i: (b * qpb + i, 0)),
                  pl.BlockSpec((tqa * heads, LAT_W), lambda b, i: (b * qpb + i, 0)),
                  pl.BlockSpec((seq, LAT_PACK), lambda b, i: (b, 0), pipeline_mode=pl.Buffered(1))],
        out_specs=pl.BlockSpec((tqa * heads, DSA_KV_RANK), lambda b, i: (b * qpb + i, 0)),
        out_shape=jax.ShapeDtypeStruct((n * heads, DSA_KV_RANK), BF16),
        scratch_shapes=[pltpu.VMEM((2, topk, LAT_PACK), jnp.uint32)],
        compiler_params=_cparams("parallel", "arbitrary"),
        name="dsa_attend",
    )(jnp.maximum(sel, 0).reshape(-1), sel, q2, table)


def _head_mm_body(a_ref, w_ref, o_ref):
    o_ref[0] = jnp.dot(a_ref[...].astype(BF16), w_ref[0], preferred_element_type=F32).astype(o_ref.dtype)


def head_mm_out(a, w, *, tm=512):
    n = a.shape[0]
    heads, k, m = w.shape
    tm = min(tm, n)
    return pl.pallas_call(
        _head_mm_body,
        grid=(n // tm, heads),
        in_specs=[pl.BlockSpec((tm, k), lambda i, h: (i, h)), pl.BlockSpec((1, k, m), lambda i, h: (h, 0, 0))],
        out_specs=pl.BlockSpec((1, tm, m), lambda i, h: (h, i, 0)),
        out_shape=jax.ShapeDtypeStruct((heads, n, m), BF16),
        compiler_params=_cparams("parallel", "parallel"),
        name="head_mm_out",
    )(a, w)


def _head_mm_in_body(a_ref, w_ref, o_ref):
    o_ref[...] = jnp.dot(a_ref[0], w_ref[0], preferred_element_type=F32).astype(o_ref.dtype)


def head_mm_in(a, w, *, tm=512):
    heads, n, k = a.shape
    m = w.shape[2]
    tm = min(tm, n)
    return pl.pallas_call(
        _head_mm_in_body,
        grid=(n // tm, heads),
        in_specs=[pl.BlockSpec((1, tm, k), lambda i, h: (h, i, 0)), pl.BlockSpec((1, k, m), lambda i, h: (h, 0, 0))],
        out_specs=pl.BlockSpec((tm, m), lambda i, h: (i, h)),
        out_shape=jax.ShapeDtypeStruct((n, heads * m), BF16),
        compiler_params=_cparams("parallel", "parallel"),
        name="head_mm_in",
    )(a, w)


def dsa_layer(hbf, positions, p, *, batch, seq):
    n = hbf.shape[0]
    heads = DSA_HEADS
    q = mm(hbf, p["w_q"])
    ckv = mm(hbf, p["w_ckv"])
    qidx = mm(hbf, p["w_qidx"])
    small = mm(hbf, p["w_small"])
    qr, table, qi, small_o, kx = rowwise(
        _dsa_prep_fn, [q, ckv, qidx, small, positions.reshape(n, 1)], [],
        [p["kv_norm"], p["idx_ln_g"], p["idx_ln_b"], _dsa_patterns()],
        [(q.shape[1], BF16), (LAT_PACK, jnp.uint32), (qidx.shape[1], BF16), (LANES, F32), (LANES, BF16)],
        seq=seq, tm=128, name="dsa_prep")
    sel = dsa_select(qi, small_o, kx, batch=batch, seq=seq)
    q2 = head_mm_out(qr, p["w_q2"])
    q2 = q2.transpose(1, 0, 2).reshape(n * heads, LAT_W)
    o_lat = dsa_attend(sel, q2, table, batch=batch, seq=seq)
    o_lat = o_lat.reshape(n, heads, DSA_KV_RANK).transpose(1, 0, 2)
    out = head_mm_in(o_lat, p["w_uv"])
    return mm(out, p["w_o"])


def _s5_tables(p):
    hp = lax.Precision.HIGHEST
    lc, hg = S5_CHUNK, S5_GROUP
    lr, li = p["lam_re"], p["lam_im"]
    g, ps = lr.shape
    dt = jnp.exp(p["log_dt"])[:, None]
    mag = jnp.exp(lr * dt)
    ab_re, ab_im = mag * jnp.cos(li * dt), mag * jnp.sin(li * dt)
    den = lr * lr + li * li
    nr, ni = ab_re - 1.0, ab_im
    coef_re = (nr * lr + ni * li) / den
    coef_im = (ni * lr - nr * li) / den
    bb_re = coef_re[..., None] * p["b_re"] - coef_im[..., None] * p["b_im"]
    bb_im = coef_re[..., None] * p["b_im"] + coef_im[..., None] * p["b_re"]
    tau = jnp.arange(lc + 1, dtype=F32)[:, None, None]
    pmag = jnp.exp(lr * dt * tau)
    pr, pi = pmag * jnp.cos(li * dt * tau), pmag * jnp.sin(li * dt * tau)
    cr, ci = p["c_re"][None], p["c_im"][None]
    car = cr * pr[:, :, None, :] - ci * pi[:, :, None, :]
    cai = cr * pi[:, :, None, :] + ci * pr[:, :, None, :]
    kern = (jnp.einsum('tghp,gpk->tghk', car[:lc], bb_re, precision=hp)
            - jnp.einsum('tghp,gpk->tghk', cai[:lc], bb_im, precision=hp))
    kz = jnp.concatenate([kern, jnp.zeros((1,) + kern.shape[1:], F32)], 0)
    s_i = np.arange(lc)[:, None]
    t_i = np.arange(lc)[None, :]
    m = kz[np.where(t_i >= s_i, t_i - s_i, lc)]
    m = m.transpose(2, 0, 4, 1, 3).reshape(g, lc * hg, lc * hg)
    skip = jnp.tile(p["d"].reshape(g, 1, hg), (1, lc, 1)).reshape(g, 1, lc * hg)
    m = m + skip * jnp.eye(lc * hg, dtype=F32)[None]
    prs, pis = pr[lc - 1 - np.arange(lc)], pi[lc - 1 - np.arange(lc)]
    wre = prs[..., None] * bb_re[None] - pis[..., None] * bb_im[None]
    wim = prs[..., None] * bb_im[None] + pis[..., None] * bb_re[None]
    wre = wre.transpose(1, 0, 3, 2).reshape(g, lc * hg, ps)
    wim = wim.transpose(1, 0, 3, 2).reshape(g, lc * hg, ps)
    vre = car[1:].transpose(1, 3, 0, 2).reshape(g, ps, lc * hg)
    vim = (-cai[1:]).transpose(1, 3, 0, 2).reshape(g, ps, lc * hg)
    eye2 = jnp.eye(2, dtype=F32)[None, :, None, :, None]

    def pair(z):
        r, c = z.shape[1:]
        return (z.reshape(g // 2, 2, r, 1, c) * eye2).reshape(g // 2, 2 * r, 2 * c).astype(BF16)

    are = pr[lc].reshape(g // 2, 1, 2 * ps)
    aim = pi[lc].reshape(g // 2, 1, 2 * ps)
    return pair(m), pair(wre), pair(wim), pair(vre), pair(vim), are, aim


def _gelu_tanh(y):
    return 0.5 * y * (1.0 + jnp.tanh(math.sqrt(2.0 / math.pi) * (y + 0.044715 * (y * y * y))))


def _s5_body(u_ref, m_ref, wre_ref, wim_ref, vre_ref, vim_ref, are_ref, aim_ref, z_ref, xre, xim, sre, sim, *, pairs):
    nc = u_ref.shape[1]
    for q in range(pairs):
        u = u_ref[q]
        xre[q] = jnp.dot(u, wre_ref[q], preferred_element_type=F32)
        xim[q] = jnp.dot(u, wim_ref[q], preferred_element_type=F32)
    ar = [are_ref[q] for q in range(pairs)]
    ai = [aim_ref[q] for q in range(pairs)]

    def step(c, carry):
        new = []
        row = pl.ds(c, 1)
        for q in range(pairs):
            re, im = carry[2 * q], carry[2 * q + 1]
            sre[q, row, :] = re
            sim[q, row, :] = im
            new.append(ar[q] * re - ai[q] * im + xre[q, row, :])
            new.append(ar[q] * im + ai[q] * re + xim[q, row, :])
        return tuple(new)

    zero = jnp.zeros((1, are_ref.shape[-1]), F32)
    lax.fori_loop(0, nc, step, (zero,) * (2 * pairs), unroll=8)
    for q in range(pairs):
        y = (jnp.dot(u_ref[q], m_ref[q], preferred_element_type=F32)
             + jnp.dot(sre[q].astype(BF16), vre_ref[q], preferred_element_type=F32)
             + jnp.dot(sim[q].astype(BF16), vim_ref[q], preferred_element_type=F32))
        z_ref[q] = _gelu_tanh(y).astype(z_ref.dtype)


def s5_layer(hbf, p, *, batch, seq):
    n, D = hbf.shape
    lc, hg = S5_CHUNK, S5_GROUP
    g2 = D // hg // 2
    pw = 2 * lc * hg
    tabs = _s5_tables(p)
    nc = seq // lc
    u = hbf.reshape(n // lc, lc, g2, 2, hg).transpose(2, 0, 3, 1, 4).reshape(g2, n // lc, pw)
    pairs = S5_PAIRS_PER_STEP
    sw = 2 * S5_STATE
    wspec = lambda r, c: pl.BlockSpec((pairs, r, c), lambda i, b: (i, 0, 0))
    z = pl.pallas_call(
        functools.partial(_s5_body, pairs=pairs),
        grid=(g2 // pairs, batch),
        in_specs=[pl.BlockSpec((pairs, nc, pw), lambda i, b: (i, b, 0)),
                  wspec(pw, pw), wspec(pw, sw), wspec(pw, sw), wspec(sw, pw), wspec(sw, pw),
                  wspec(1, sw), wspec(1, sw)],
        out_specs=pl.BlockSpec((pairs, nc, pw), lambda i, b: (i, b, 0)),
        out_shape=jax.ShapeDtypeStruct((g2, n // lc, pw), BF16),
        scratch_shapes=[pltpu.VMEM((pairs, nc, sw), F32)] * 4,
        compiler_params=_cparams("parallel", "parallel"),
        name="s5_scan",
    )(u, *tabs)
    z = z.reshape(g2, n // lc, 2, lc, hg).transpose(1, 3, 0, 2, 4).reshape(n, D)
    zz = mm(z, p["w_glu"])
    return rowwise(lambda a: (a[:, :D] * jax.nn.sigmoid(a[:, D:]),), [zz], [], [], [(D, F32)], seq=seq, tm=256,
                   name="s5_glu")[0]


def _modulate(x, sc, sh, dtype, *, seq):
    d = x.shape[1]
    return rowwise(lambda xv, s, h: (xv * (1.0 + s) + h,), [x], [sc, sh], [], [(d, dtype)], seq=seq, tm=256,
                   name="modulate")[0]


def kernel(x, c, positions, ada_w, ada_b, ln_g, ln_b, router_w, router_bias, moe_w_gate, moe_w_up, moe_w_down,
           rwkv_mu, rwkv_w_rkv, rwkv_w_o, rwkv_w0, rwkv_w1, rwkv_w2, rwkv_a0, rwkv_a1, rwkv_a2, rwkv_g1, rwkv_g2,
           rwkv_k_k, rwkv_k_a, rwkv_r_k, rwkv_gn_g, rwkv_gn_b, rwkv_v0, rwkv_v1, rwkv_v2, dsa_w_in, dsa_kv_norm,
           dsa_w_uk, dsa_w_uv, dsa_idx_ln_g, dsa_idx_ln_b, dsa_w_o, s5_lam_re, s5_lam_im, s5_log_dt, s5_b_re,
           s5_b_im, s5_c_re, s5_c_im, s5_d, s5_w_glu):
    batch, seq, d = x.shape
    depth = ada_w.shape[0]
    n = batch * seq
    xf = x.reshape(n, d)
    mod = ada_mod(c, ada_w, ada_b)[:, :batch].reshape(depth, batch, 1, 6, d)
    v_first = None
    for i in range(depth):
        kind, j = i % N_MIXERS, i // N_MIXERS
        sh1, sc1, g1, sh2, sc2, g2 = (mod[i, :, :, m] for m in range(6))
        if kind == 0:
            p = dict(mu=rwkv_mu[j], w_r=rwkv_w_rkv[j, 0].astype(BF16), w_k=rwkv_w_rkv[j, 1].astype(BF16),
                     w_v=rwkv_w_rkv[j, 2].astype(BF16), w_o=rwkv_w_o[j].astype(BF16),
                     w0=rwkv_w0[j], w1=_wpad(rwkv_w1[j]), w2=_kpad(rwkv_w2[j]),
                     a0=rwkv_a0[j], a1=_wpad(rwkv_a1[j]), a2=_kpad(rwkv_a2[j]),
                     g1=_wpad(rwkv_g1[j]), g2=_kpad(rwkv_g2[j]),
                     k_k=rwkv_k_k[j], k_a=rwkv_k_a[j], r_k=rwkv_r_k[j], gn_g=rwkv_gn_g[j], gn_b=rwkv_gn_b[j],
                     v0=None)
            if j > 0:
                p.update(v0=rwkv_v0[j - 1], v1=_wpad(rwkv_v1[j - 1]), v2=_kpad(rwkv_v2[j - 1]))
            y, v_first = rwkv_layer(xf, sc1, sh1, v_first, p, batch=batch, seq=seq)
        elif kind == 1:
            w_in = dsa_w_in[j]
            hq = DSA_HEADS * DSA_HEAD_DIM
            o1 = hq + DSA_KV_RANK
            o2 = o1 + DSA_ROPE_DIM
            o3 = o2 + IDX_HEADS * IDX_DIM
            o4 = o3 + IDX_DIM
            gap = jnp.zeros((d, SM_R0 - SM_W0 - IDX_HEADS), F32)
            w_small = jnp.concatenate([w_in[:, o3:o4], w_in[:, o4:], gap, w_in[:, o1:o2]], axis=1)
            rope_pass = jnp.zeros((DSA_ROPE_DIM, LAT_W), F32).at[
                jnp.arange(DSA_ROPE_DIM), DSA_KV_RANK + SM_R0 + jnp.arange(DSA_ROPE_DIM)].set(1.0)
            w_uk_pad = jnp.pad(dsa_w_uk[j], ((0, 0), (0, 0), (0, LAT_W - DSA_KV_RANK)))
            w_q2 = jnp.concatenate([jnp.broadcast_to(rope_pass, (DSA_HEADS,) + rope_pass.shape), w_uk_pad], axis=1)
            lane_pad = lambda v: jnp.pad(v, (0, LANES - v.shape[0])).reshape(1, LANES)
            p = dict(w_q=w_in[:, :hq].astype(BF16), w_ckv=w_in[:, hq:o1].astype(BF16),
                     w_qidx=w_in[:, o2:o3].astype(BF16), w_small=w_small.astype(BF16),
                     kv_norm=dsa_kv_norm[j].reshape(1, -1), w_q2=w_q2.astype(BF16), w_uv=dsa_w_uv[j].astype(BF16),
                     idx_ln_g=lane_pad(dsa_idx_ln_g[j]), idx_ln_b=lane_pad(dsa_idx_ln_b[j]),
                     w_o=dsa_w_o[j].astype(BF16))
            hbf = _modulate(xf, sc1, sh1, BF16, seq=seq)
            y = dsa_layer(hbf, positions, p, batch=batch, seq=seq)
        else:
            p = dict(lam_re=s5_lam_re[j], lam_im=s5_lam_im[j], log_dt=s5_log_dt[j], b_re=s5_b_re[j],
                     b_im=s5_b_im[j], c_re=s5_c_re[j], c_im=s5_c_im[j], d=s5_d[j], w_glu=s5_w_glu[j].astype(BF16))
            hbf = _modulate(xf, sc1, sh1, BF16, seq=seq)
            y = s5_layer(hbf, p, batch=batch, seq=seq)
        xf, hbf = post_norm_mix(xf, y, g1, sc2, sh2, ln_g[i, 0:1], ln_b[i, 0:1], seq=seq)
        y0, y1, gates = moe_ffn(xf, hbf, sc2, sh2, router_w, router_bias, moe_w_gate[i].astype(BF16),
                                moe_w_up[i].astype(BF16), moe_w_down[i].astype(BF16), seq=seq)
        xf = post_norm_moe(xf, y0, y1, gates, g2, ln_g[i, 1:2], ln_b[i, 1:2], seq=seq)
    return xf.reshape(batch, seq, d)
```

```python
import functools
import math

import numpy as np
import jax
import jax.numpy as jnp
from jax import lax
from jax.experimental import pallas as pl
from jax.experimental.pallas import tpu as pltpu

F32, BF16 = jnp.float32, jnp.bfloat16

DEPTH = 4
N_MIXERS = 3
RWKV_HEAD = 64
RWKV_GN_EPS = 64e-5
DSA_HEADS = 16
DSA_HEAD_DIM = 128
DSA_ROPE_DIM = 32
DSA_NOPE_DIM = 96
DSA_V_DIM = 128
DSA_KV_RANK = 512
IDX_HEADS = 16
IDX_DIM = 64
IDX_ROPE_DIM = 16
TOPK_MAX = 256
Q_BLOCK = 128
S5_GROUP = 16
S5_STATE = 64
S5_CHUNK = 16
S5_PAIRS_PER_STEP = 4
N_EXPERTS = 32
N_GROUPS = 4
EXPERTS_PER_GROUP = 8
TOP_K = 2
ROPE_THETA = 500000.0
LN_EPS = 1e-5
DEEPNORM_ALPHA = (2 * DEPTH) ** 0.25

LANES = 128
VMEM_LIMIT = 48 * 1024 * 1024
MOE_ROWS = 256
WKV_CHUNK = 64
WKV_HEADS_PER_STEP = 4
WKV_GROUPS_PER_STEP = 1
WKV_ROWS_PER_STEP = 512


def _cparams(*sem):
    return pltpu.CompilerParams(dimension_semantics=sem, vmem_limit_bytes=VMEM_LIMIT)


def _pick(n, cands):
    for c in cands:
        if n % c == 0:
            return c
    return n


def _mm_body(a_ref, w_ref, o_ref, *, act):
    acc = jnp.dot(a_ref[...].astype(BF16), w_ref[...], preferred_element_type=F32)
    if act == "tanh":
        acc = jnp.tanh(acc)
    elif act == "sigmoid":
        acc = jax.nn.sigmoid(acc)
    o_ref[...] = acc.astype(o_ref.dtype)


def mm(a, w, *, out_dtype=F32, act=None, tm=512):
    m, k = a.shape
    n = w.shape[1]
    tm = _pick(m, (tm, 256, 128, 64, 32, 16, 8))
    tn = _pick(n, (512, 256, 128))
    return pl.pallas_call(
        functools.partial(_mm_body, act=act),
        grid=(m // tm, n // tn),
        in_specs=[pl.BlockSpec((tm, k), lambda i, j: (i, 0)), pl.BlockSpec((k, tn), lambda i, j: (0, j))],
        out_specs=pl.BlockSpec((tm, tn), lambda i, j: (i, j)),
        out_shape=jax.ShapeDtypeStruct((m, n), out_dtype),
        compiler_params=_cparams("parallel", "parallel"),
        name="mm",
    )(a, w)


def _wpad(w):
    n = w.shape[1]
    npad = -(-n // LANES) * LANES
    w = w.astype(BF16)
    return w if npad == n else jnp.pad(w, ((0, 0), (0, npad - n)))


def _kpad(w):
    k = w.shape[0]
    kpad = -(-k // LANES) * LANES
    w = w.astype(BF16)
    return w if kpad == k else jnp.pad(w, ((0, kpad - k), (0, 0)))


def rowwise(fn, rows, perbatch, consts, outs, *, seq, tm, name):
    n = rows[0].shape[0]
    tm = min(tm, seq)
    tpb = seq // tm
    nr, nb, nc = len(rows), len(perbatch), len(consts)

    def body(*refs):
        vals = [r[...] for r in refs[:nr]]
        vals += [r[0] for r in refs[nr:nr + nb]]
        vals += [r[...] for r in refs[nr + nb:nr + nb + nc]]
        res = fn(*vals)
        for o, v in zip(refs[nr + nb + nc:], res):
            o[...] = v.astype(o.dtype)

    in_specs = [pl.BlockSpec((tm, r.shape[1]), lambda i: (i, 0)) for r in rows]
    in_specs += [pl.BlockSpec((1, 1, p.shape[-1]), lambda i: (i // tpb, 0, 0)) for p in perbatch]
    in_specs += [pl.BlockSpec(c.shape, lambda i: (0, 0)) for c in consts]
    res = pl.pallas_call(
        body,
        grid=(n // tm,),
        in_specs=in_specs,
        out_specs=[pl.BlockSpec((tm, w), lambda i: (i, 0)) for w, _ in outs],
        out_shape=[jax.ShapeDtypeStruct((n, w), dt) for w, dt in outs],
        compiler_params=_cparams("parallel"),
        name=name,
    )(*rows, *perbatch, *consts)
    return res


def _layer_norm(z, g, b):
    mu = jnp.mean(z, -1, keepdims=True)
    zc = z - mu
    var = jnp.mean(zc * zc, -1, keepdims=True)
    return zc * lax.rsqrt(var + LN_EPS) * g + b


def _split_dot(x, m):
    xh = x.astype(BF16)
    xl = (x - xh.astype(F32)).astype(BF16)
    return jnp.dot(xh, m, preferred_element_type=F32) + jnp.dot(xl, m, preferred_element_type=F32)


def _seg_sum(x, ones_bd):
    parts = [_split_dot(x[:, j:j + LANES], ones_bd) for j in range(0, x.shape[1], LANES)]
    return jnp.concatenate(parts, -1)


def _head_ones():
    i = np.arange(LANES)
    return jnp.asarray((i[:, None] // RWKV_HEAD) == (i[None, :] // RWKV_HEAD), BF16)


def _ada_body(c_ref, w_ref, b_ref, o_ref):
    acc = jnp.dot(c_ref[...], w_ref[0], preferred_element_type=F32, precision=lax.Precision.HIGHEST)
    o_ref[0] = acc + b_ref[0]


def ada_mod(c, ada_w, ada_b):
    depth, d, n = ada_w.shape
    b = c.shape[0]
    cp = jnp.pad(c, ((0, 8 - b), (0, 0)))
    tn = 1024
    return pl.pallas_call(
        _ada_body,
        grid=(depth, n // tn),
        in_specs=[pl.BlockSpec((8, d), lambda i, j: (0, 0)),
                  pl.BlockSpec((1, d, tn), lambda i, j: (i, 0, j)),
                  pl.BlockSpec((1, 1, tn), lambda i, j: (i, 0, j))],
        out_specs=pl.BlockSpec((1, 8, tn), lambda i, j: (i, 0, j)),
        out_shape=jax.ShapeDtypeStruct((depth, 8, n), F32),
        compiler_params=_cparams("parallel", "parallel"),
        name="ada_mod",
    )(cp, ada_w, ada_b.reshape(depth, 1, n))


def post_norm_mix(x, y, gate, sc, sh, lng, lnb, *, seq):
    d = x.shape[1]

    def fn(xv, yv, g, s, h, lg, lb):
        xn = _layer_norm(DEEPNORM_ALPHA * xv + (1.0 + g) * yv, lg, lb)
        return xn, xn * (1.0 + s) + h

    return rowwise(fn, [x, y], [gate, sc, sh], [lng, lnb], [(d, F32), (d, BF16)], seq=seq, tm=256, name="post_norm_mix")


def post_norm_moe(x, y0, y1, gates, gate, lng, lnb, *, seq):
    d = x.shape[1]

    def fn(xv, a, b, gt, g, lg, lb):
        y = a * gt[:, 0:1] + b * gt[:, 1:2]
        return (_layer_norm(DEEPNORM_ALPHA * xv + (1.0 + g) * y, lg, lb),)

    return rowwise(fn, [x, y0, y1, gates], [gate], [lng, lnb], [(d, F32)], seq=seq, tm=256, name="post_norm_moe")[0]


def _router_fn(xv, s, h, rw, rb):
    hf = xv * (1.0 + s) + h
    logits = jnp.dot(hf, rw, preferred_element_type=F32, precision=lax.Precision.HIGHEST)
    scores = jax.nn.sigmoid(logits)
    biased = scores + rb
    lane = lax.broadcasted_iota(jnp.int32, biased.shape, 1)
    neg = jnp.float32(-jnp.inf)
    big = jnp.int32(1 << 20)
    best = bi1 = bi2 = None
    for g in range(N_GROUPS):
        ing = (lane >= g * EXPERTS_PER_GROUP) & (lane < (g + 1) * EXPERTS_PER_GROUP)
        v = jnp.where(ing, biased, neg)
        m1 = jnp.max(v, -1, keepdims=True)
        i1 = jnp.min(jnp.where(v == m1, lane, big), -1, keepdims=True)
        v2 = jnp.where(lane == i1, neg, v)
        m2 = jnp.max(v2, -1, keepdims=True)
        i2 = jnp.min(jnp.where(v2 == m2, lane, big), -1, keepdims=True)
        gs = m1 + m2
        if g == 0:
            best, bi1, bi2 = gs, i1, i2
        else:
            better = gs > best
            best = jnp.where(better, gs, best)
            bi1 = jnp.where(better, i1, bi1)
            bi2 = jnp.where(better, i2, bi2)
    g1 = jnp.sum(jnp.where(lane == bi1, scores, 0.0), -1, keepdims=True)
    g2 = jnp.sum(jnp.where(lane == bi2, scores, 0.0), -1, keepdims=True)
    tot = g1 + g2
    two = lax.broadcasted_iota(jnp.int32, (biased.shape[0], TOP_K), 1)
    return jnp.where(two == 0, bi1, bi2), jnp.where(two == 0, g1 / tot, g2 / tot)


def _expert_body(be_ref, nb_ref, x_ref, wg_ref, wu_ref, wd_ref, o_ref):
    del be_ref

    @pl.when(pl.program_id(0) < nb_ref[0])
    def _():
        x = x_ref[...]
        g = jnp.dot(x, wg_ref[0], preferred_element_type=F32)
        u = jnp.dot(x, wu_ref[0], preferred_element_type=F32)
        hid = (g * jax.nn.sigmoid(g) * u).astype(BF16)
        o_ref[...] = jnp.dot(hid, wd_ref[0], preferred_element_type=F32)

    @pl.when(pl.program_id(0) >= nb_ref[0])
    def _():
        o_ref[...] = jnp.zeros_like(o_ref)


def moe_ffn(x, hbf, sc, sh, router_w, router_bias, wg, wu, wd, *, seq):
    n, d = x.shape
    e = router_w.shape[1]
    idx, gates = rowwise(_router_fn, [x], [sc, sh], [router_w, router_bias.reshape(1, e)],
                         [(TOP_K, jnp.int32), (TOP_K, F32)], seq=seq, tm=256, name="router")
    bm = MOE_ROWS
    n_slots = n * TOP_K
    flat_e = idx.reshape(-1)
    onehot = (flat_e[:, None] == jnp.arange(e, dtype=jnp.int32)[None, :]).astype(jnp.int32)
    csum = jnp.cumsum(onehot, axis=0)
    pos = jnp.sum(onehot * csum, axis=1) - 1
    counts = csum[-1]
    padded = (counts + bm - 1) // bm * bm
    pend = jnp.cumsum(padded)
    pstart = pend - padded
    dest = pstart[flat_e] + pos
    n_blocks = n_slots // bm + e
    n_pad = n_blocks * bm
    slot_tok = jnp.zeros((n_pad,), jnp.int32).at[dest].set(jnp.arange(n_slots, dtype=jnp.int32) // TOP_K)
    block_start = jnp.arange(n_blocks, dtype=jnp.int32) * bm
    block_expert = jnp.minimum(jnp.sum((pend[None, :] <= block_start[:, None]).astype(jnp.int32), axis=1), e - 1)
    used_blocks = (pend[-1] // bm).astype(jnp.int32).reshape(1)
    xs = jnp.take(hbf, slot_tok, axis=0)
    de = wg.shape[2]
    yb = pl.pallas_call(
        _expert_body,
        grid_spec=pltpu.PrefetchScalarGridSpec(
            num_scalar_prefetch=2,
            grid=(n_blocks,),
            in_specs=[pl.BlockSpec((bm, d), lambda i, be, nb: (i, 0)),
                      pl.BlockSpec((1, d, de), lambda i, be, nb: (be[i], 0, 0)),
                      pl.BlockSpec((1, d, de), lambda i, be, nb: (be[i], 0, 0)),
                      pl.BlockSpec((1, de, d), lambda i, be, nb: (be[i], 0, 0))],
            out_specs=pl.BlockSpec((bm, d), lambda i, be, nb: (i, 0)),
        ),
        out_shape=jax.ShapeDtypeStruct((n_pad, d), F32),
        compiler_params=_cparams("arbitrary"),
        name="moe_experts",
    )(block_expert, used_blocks, xs, wg, wu, wd)
    dest2 = dest.reshape(n, TOP_K)
    return jnp.take(yb, dest2[:, 0], axis=0), jnp.take(yb, dest2[:, 1], axis=0), gates


def _rwkv_prep_body(x_ref, xp_ref, sc_ref, sh_ref, mu_ref, *outs, tpb):
    sc = 1.0 + sc_ref[0]
    sh = sh_ref[0]
    h = x_ref[...] * sc + sh
    prev_row = xp_ref[7:8, :] * sc + sh
    prev_row = jnp.where(pl.program_id(0) % tpb == 0, 0.0, prev_row)
    rowid = lax.broadcasted_iota(jnp.int32, h.shape, 0)
    hprev = jnp.where(rowid == 0, prev_row, pltpu.roll(h, 1, 0))
    dx = hprev - h
    for m, o in enumerate(outs):
        o[...] = (h + dx * mu_ref[m:m + 1, :]).astype(o.dtype)


def rwkv_prep(x, sc, sh, mu, *, seq):
    n, d = x.shape
    tm = min(256, seq)
    tpb = seq // tm
    return pl.pallas_call(
        functools.partial(_rwkv_prep_body, tpb=tpb),
        grid=(n // tm,),
        in_specs=[pl.BlockSpec((tm, d), lambda i: (i, 0)),
                  pl.BlockSpec((8, d), lambda i: (jnp.maximum(i * (tm // 8) - 1, 0), 0)),
                  pl.BlockSpec((1, 1, d), lambda i: (i // tpb, 0, 0)),
                  pl.BlockSpec((1, 1, d), lambda i: (i // tpb, 0, 0)),
                  pl.BlockSpec((6, d), lambda i: (0, 0))],
        out_specs=[pl.BlockSpec((tm, d), lambda i: (i, 0))] * 6,
        out_shape=[jax.ShapeDtypeStruct((n, d), BF16)] * 6,
        compiler_params=_cparams("parallel"),
        name="rwkv_prep",
    )(x, x, sc, sh, mu)


def _softplus(z):
    return jnp.maximum(z, 0.0) + jnp.log(1.0 + jnp.exp(-jnp.abs(z)))


def _rwkv_gates_fn(k0, lw, al, *rest, has_vres):
    if has_vres:
        v, vfirst, vl, prm, ones_bd = rest
    else:
        prm, ones_bd = rest
    w0, a0, k_k, k_a = prm[0:1], prm[1:2], prm[2:3], prm[3:4]
    log_w = -_softplus(-(w0 + lw)) - 0.5
    logdecay = -jnp.exp(log_w)
    a = jax.nn.sigmoid(a0 + al)
    kk = k0 * k_k
    nrm = jnp.maximum(jnp.sqrt(_seg_sum(kk * kk, ones_bd)), 1e-12)
    kk = kk / nrm
    k = k0 * (1.0 + (a - 1.0) * k_a)
    res = [logdecay, k, kk, kk * a]
    if has_vres:
        res.append(v + (vfirst - v) * jax.nn.sigmoid(prm[4:5] + vl))
    return res


def _wkv_body(r_ref, lw_ref, k_ref, v_ref, kk_ref, b_ref, y_ref, *s_refs, chunk, heads):
    hd = RWKV_HEAD
    rows = r_ref.shape[0]
    width = heads * hd
    hr = heads * chunk
    nt = (((1,), (1,)), ((), ()))

    @pl.when(pl.program_id(2) == 0)
    def _():
        for s_ref in s_refs:
            s_ref[...] = jnp.zeros_like(s_ref)

    ti = lax.broadcasted_iota(jnp.int32, (chunk, chunk), 0)
    si = lax.broadcasted_iota(jnp.int32, (chunk, chunk), 1)
    tri = (ti >= si).astype(BF16)
    ri = lax.broadcasted_iota(jnp.int32, (hr, width), 0)
    ci = lax.broadcasted_iota(jnp.int32, (hr, width), 1)
    head_mask = (ri // chunk == ci // hd).astype(F32)
    rr = lax.broadcasted_iota(jnp.int32, (hr, hr), 0)
    cc = lax.broadcasted_iota(jnp.int32, (hr, hr), 1)
    incl = rr % chunk >= cc % chunk
    strict = rr % chunk > cc % chunk
    eye = (rr == cc).astype(F32)

    def blocked(x):
        return (jnp.concatenate([x] * heads, 0) * head_mask).astype(BF16)

    def step(c, carry):
        sl = pl.ds(pl.multiple_of(c * chunk, chunk), chunk)
        for grp, s_ref in enumerate(s_refs):
            group_step(sl, slice(grp * width, (grp + 1) * width), s_ref)
        return carry

    def group_step(sl, ln, s_ref):
        lw = lw_ref[sl, ln]
        cs = _split_dot_left(tri, lw)
        ctot = cs[chunk - 1:chunk, :]
        g_inv = jnp.exp(-cs)
        g_end = jnp.exp(ctot - cs)
        kk = kk_ref[sl, ln]
        kv = k_ref[sl, ln]
        bv = b_ref[sl, ln]
        ar = jnp.concatenate([blocked(-kk * jnp.exp(cs - lw)), blocked(r_ref[sl, ln] * jnp.exp(cs))], 0)
        bk = jnp.concatenate([blocked(bv * g_inv), blocked(kv * g_inv)], 0)
        ends = jnp.concatenate([blocked(bv * g_end), blocked(kv * g_end)], 0)
        vb = blocked(v_ref[sl, ln])
        s_old = s_ref[...]
        m = lax.dot_general(ar, bk, nt, preferred_element_type=F32)
        a_ab = jnp.where(strict, m[:hr, :hr], 0.0)
        a_ak = jnp.where(strict, m[:hr, hr:], 0.0)
        a_r = jnp.concatenate([jnp.where(incl, m[hr:, :hr], 0.0), jnp.where(incl, m[hr:, hr:], 0.0)], 1)
        ars = lax.dot_general(ar, s_old.astype(BF16), nt, preferred_element_type=F32)
        rhs = ars[:hr] + jnp.dot(a_ak.astype(BF16), vb, preferred_element_type=F32)
        inv = eye + a_ab
        apow = a_ab.astype(BF16)
        for _ in range(int(math.log2(chunk)) - 1):
            apow = jnp.dot(apow, apow, preferred_element_type=F32).astype(BF16)
            inv = inv + jnp.dot(inv.astype(BF16), apow, preferred_element_type=F32)
        u = jnp.dot(inv.astype(BF16), rhs.astype(BF16), preferred_element_type=F32)
        uv = jnp.concatenate([u.astype(BF16), vb], 0)
        yb = ars[hr:] + jnp.dot(a_r.astype(BF16), uv, preferred_element_type=F32)
        y = yb[:chunk]
        for h in range(1, heads):
            y = y + yb[h * chunk:(h + 1) * chunk]
        y_ref[sl, ln] = y
        s_ref[...] = s_old * jnp.exp(ctot) + lax.dot_general(
            uv, ends, (((0,), (0,)), ((), ())), preferred_element_type=F32)

    lax.fori_loop(0, rows // chunk, step, 0)


def _split_dot_left(m, x):
    xh = x.astype(BF16)
    xl = (x - xh.astype(F32)).astype(BF16)
    return jnp.dot(m, xh, preferred_element_type=F32) + jnp.dot(m, xl, preferred_element_type=F32)


def wkv7(r, logdecay, k, v, kk, b, *, batch, seq):
    n, d = r.shape
    chunk = min(WKV_CHUNK, seq)
    rows = min(WKV_ROWS_PER_STEP, seq)
    heads = WKV_HEADS_PER_STEP
    groups = WKV_GROUPS_PER_STEP
    width = heads * RWKV_HEAD
    spb = seq // rows
    spec = pl.BlockSpec((rows, groups * width), lambda bi, hi, ti: (bi * spb + ti, hi))
    return pl.pallas_call(
        functools.partial(_wkv_body, chunk=chunk, heads=heads),
        grid=(batch, d // (groups * width), spb),
        in_specs=[spec] * 6,
        out_specs=spec,
        out_shape=jax.ShapeDtypeStruct((n, d), F32),
        scratch_shapes=[pltpu.VMEM((width, width), F32)] * groups,
        compiler_params=_cparams("parallel", "parallel", "arbitrary"),
        name="wkv7",
    )(r, logdecay, k, v, kk, b)


def _rwkv_out_fn(y, r, k, v, g, prm, ones_bd):
    r_k, gn_g, gn_b = prm[0:1], prm[1:2], prm[2:3]
    inv_n = 1.0 / RWKV_HEAD
    m_y = _seg_sum(y, ones_bd) * inv_n
    yc = y - m_y
    v_y = _seg_sum(yc * yc, ones_bd) * inv_n
    yn = yc * lax.rsqrt(v_y + RWKV_GN_EPS) * gn_g + gn_b
    bonus = _seg_sum(r * k * r_k, ones_bd) * v
    return ((yn + bonus) * g,)


def rwkv_layer(x, sc, sh, v_first, p, *, batch, seq):
    n, d = x.shape
    xr, xw, xk, xv, xa, xg = rwkv_prep(x, sc, sh, p["mu"], seq=seq)
    r = mm(xr, p["w_r"])
    k0 = mm(xk, p["w_k"])
    v = mm(xv, p["w_v"])
    lw = mm(mm(xw, p["w1"], out_dtype=BF16, act="tanh"), p["w2"])
    al = mm(mm(xa, p["a1"], out_dtype=BF16), p["a2"])
    g = mm(mm(xg, p["g1"], out_dtype=BF16, act="sigmoid"), p["g2"])
    ones_bd = _head_ones()
    has_vres = p["v0"] is not None
    outs = [(d, F32)] * (5 if has_vres else 4)
    if has_vres:
        vl = mm(mm(xv, p["v1"], out_dtype=BF16), p["v2"])
        prm = jnp.stack([p["w0"], p["a0"], p["k_k"], p["k_a"], p["v0"]])
        rows = [k0, lw, al, v, v_first, vl]
    else:
        prm = jnp.stack([p["w0"], p["a0"], p["k_k"], p["k_a"]])
        rows = [k0, lw, al]
    res = rowwise(functools.partial(_rwkv_gates_fn, has_vres=has_vres), rows, [], [prm, ones_bd], outs,
                  seq=seq, tm=128, name="rwkv_gates")
    logdecay, k, kk, b = res[:4]
    if has_vres:
        v = res[4]
    else:
        v_first = v
    y = wkv7(r, logdecay, k, v, kk, b, batch=batch, seq=seq)
    prm2 = jnp.stack([p["r_k"], p["gn_g"], p["gn_b"]])
    yg = rowwise(_rwkv_out_fn, [y, r, k, v, g], [], [prm2, ones_bd], [(d, BF16)], seq=seq, tm=128, name="rwkv_out")[0]
    return mm(yg, p["w_o"]), v_first


SM_W0 = IDX_DIM
SM_R0 = LANES - DSA_ROPE_DIM
LAT_W = 768
LAT_PACK = LAT_W // 2
LAT_TILES = LAT_PACK // LANES


def _dsa_patterns():
    fq = ROPE_THETA ** (-np.arange(DSA_ROPE_DIM // 2, dtype=np.float32) * np.float32(2.0 / DSA_ROPE_DIM))
    fi = ROPE_THETA ** (-np.arange(IDX_ROPE_DIM // 2, dtype=np.float32) * np.float32(2.0 / IDX_ROPE_DIM))
    hq, hi = DSA_ROPE_DIM // 2, IDX_ROPE_DIM // 2
    pat = np.zeros((16, LANES), np.float32)
    lane = np.arange(LANES)
    pat[0, :hq] = fq; pat[0, hq:2 * hq] = fq
    pat[1, :hq] = -1.0; pat[2, hq:2 * hq] = 1.0
    l64 = lane % IDX_DIM
    pat[3] = np.where(l64 < hi, fi[np.minimum(l64, hi - 1)], np.where(l64 < 2 * hi, fi[np.clip(l64 - hi, 0, hi - 1)], 0.0))
    pat[4] = np.where(l64 < hi, -1.0, 0.0); pat[5] = np.where((l64 >= hi) & (l64 < 2 * hi), 1.0, 0.0)
    pat[6, :hi] = fi; pat[6, hi:2 * hi] = fi
    pat[6, SM_R0:SM_R0 + hq] = fq; pat[6, SM_R0 + hq:] = fq
    pat[7, :hi] = -1.0; pat[8, hi:2 * hi] = 1.0
    pat[9, SM_R0:SM_R0 + hq] = -1.0; pat[10, SM_R0 + hq:] = 1.0
    pat[11, :IDX_DIM] = 1.0
    pat[12, SM_W0:SM_W0 + IDX_HEADS] = IDX_HEADS ** -0.5 * IDX_DIM ** -0.5
    pat[13, SM_R0:] = 1.0
    return jnp.asarray(pat)


def _rot(x, shift):
    return pltpu.roll(x, shift % x.shape[1], 1)


def _tile_lanes(v, reps):
    return jnp.concatenate([v] * reps, 1) if reps > 1 else v


def _dsa_prep_fn(q, ckv, qidx, small, pos, kvn, lng, lnb, pat):
    posf = pos.astype(F32)
    hq, hi = DSA_ROPE_DIM // 2, IDX_ROPE_DIM // 2
    ang = posf * pat[0:1]
    c, s = jnp.cos(ang), jnp.sin(ang)
    reps = q.shape[1] // LANES
    qr = (q * _tile_lanes(c, reps) + _rot(q, -hq) * _tile_lanes(s * pat[1:2], reps)
          + _rot(q, hq) * _tile_lanes(s * pat[2:3], reps))
    ang = posf * pat[3:4]
    c, s = jnp.cos(ang), jnp.sin(ang)
    reps = qidx.shape[1] // LANES
    qi = (qidx * _tile_lanes(c, reps) + _rot(qidx, -hi) * _tile_lanes(s * pat[4:5], reps)
          + _rot(qidx, hi) * _tile_lanes(s * pat[5:6], reps))
    ckn = ckv * lax.rsqrt(jnp.mean(ckv * ckv, -1, keepdims=True) + 1e-6) * kvn
    mk = pat[11:12]
    inv = 1.0 / IDX_DIM
    mu = jnp.sum(small * mk, -1, keepdims=True) * inv
    dv = (small - mu) * mk
    var = jnp.sum(dv * dv, -1, keepdims=True) * inv
    y = (dv * lax.rsqrt(var + LN_EPS) * lng + lnb) * mk + small * pat[12:13] + small * pat[13:14]
    ang = posf * pat[6:7]
    c, s = jnp.cos(ang), jnp.sin(ang)
    so = (y * c + _rot(y, -hi) * (s * pat[7:8]) + _rot(y, hi) * (s * pat[8:9])
          + _rot(y, -hq) * (s * pat[9:10]) + _rot(y, hq) * (s * pat[10:11]))
    lat = jnp.concatenate([ckn, so * pat[13:14], jnp.zeros_like(so)], 1)
    lo = lax.bitcast_convert_type(lat[:, :LAT_PACK].astype(BF16).astype(F32), jnp.uint32) >> 16
    hi_b = lax.bitcast_convert_type(lat[:, LAT_PACK:].astype(BF16).astype(F32), jnp.uint32) & jnp.uint32(0xFFFF0000)
    return qr, lo | hi_b, qi, so, so


def _sortable(x):
    b = lax.bitcast_convert_type(x, jnp.int32)
    return jnp.where(b < 0, b ^ jnp.int32(0x7FFFFFFF), b)


def _dsa_select_body(qi_ref, sm_ref, kx_ref, sel_ref, key_ref, ib_ref, *, tq, ts, topk, seq):
    sub = 32
    t0 = pl.program_id(1) * tq
    nchunk = (t0 + tq + ts - 1) // ts
    qpos = t0 + lax.broadcasted_iota(jnp.int32, (tq, 1), 0)
    w = sm_ref[:, SM_W0:SM_W0 + IDX_HEADS]
    neg = jnp.float32(-jnp.inf)

    def score_chunk(c, carry):
        off = pl.multiple_of(c * ts, ts)
        ks = kx_ref[pl.ds(off, ts), :][:, :IDX_DIM]
        acc = jnp.zeros((tq, ts), F32)
        for h in range(IDX_HEADS):
            s = lax.dot_general(qi_ref[:, h * IDX_DIM:(h + 1) * IDX_DIM], ks, (((1,), (1,)), ((), ())),
                                preferred_element_type=F32)
            acc = acc + w[:, h:h + 1] * jnp.maximum(s, 0.0)
        kpos = off + lax.broadcasted_iota(jnp.int32, (1, ts), 1)
        key_ref[:, pl.ds(off, ts)] = _sortable(jnp.where(kpos <= qpos, acc, neg))
        return carry

    lax.fori_loop(0, nchunk, score_chunk, 0)

    tpc = ts // LANES
    lane_iota = lax.broadcasted_iota(jnp.int32, (1, LANES), 1)

    def count(pred):
        def body(c, acc):
            for u in range(tpc):
                off = pl.multiple_of(c * ts + u * LANES, LANES)
                acc = acc + pred(key_ref[:, pl.ds(off, LANES)], off + lane_iota).astype(jnp.int32)
            return acc
        acc = lax.fori_loop(0, nchunk, body, jnp.zeros((tq, LANES), jnp.int32))
        return jnp.sum(acc, -1, keepdims=True)

    sign = jnp.int32(-2 ** 31)
    thr_u = jnp.zeros((tq, 1), jnp.int32)
    for bit in range(31, -1, -1):
        cand_u = thr_u | jnp.int32(np.uint32(1 << bit).astype(np.int32))
        cand = cand_u ^ sign
        cnt = count(lambda k, kp, cand=cand: k >= cand)
        thr_u = jnp.where(cnt >= topk, cand_u, thr_u)
    thr = thr_u ^ sign
    need = topk - count(lambda k, kp: k > thr)
    n_eq = count(lambda k, kp: k == thr)

    ib_ref[...] = jnp.full((tq, 1), seq, jnp.int32)

    @pl.when(jnp.max(n_eq - need) > 0)
    def _():
        ib = jnp.zeros((tq, 1), jnp.int32)
        for bit in range(int(math.log2(seq)) - 1, -1, -1):
            cand = ib | jnp.int32(1 << bit)
            cnt = count(lambda k, kp, cand=cand: (k == thr) & (kp < cand))
            ib = jnp.where(cnt < need, cand, ib)
        ib_ref[...] = ib

    ib = ib_ref[...]

    def mask_tiles(c, acc):
        for u in range(tpc):
            off = pl.multiple_of(c * ts + u * LANES, LANES)
            kpos = off + lane_iota
            k = key_ref[:, pl.ds(off, LANES)]
            m = (((k > thr) | ((k == thr) & (kpos <= ib))) & (kpos <= qpos)).astype(jnp.int32)
            key_ref[:, pl.ds(off, LANES)] = m
            acc = acc + m
        return acc

    colcnt = lax.fori_loop(0, nchunk, mask_tiles, jnp.zeros((tq, LANES), jnp.int32))
    li = lax.broadcasted_iota(jnp.int32, (LANES, LANES), 0)
    lj = lax.broadcasted_iota(jnp.int32, (LANES, LANES), 1)
    cin = jnp.dot(colcnt.astype(BF16), (li <= lj).astype(BF16), preferred_element_type=F32).astype(jnp.int32)
    total = cin[:, LANES - 1:LANES]
    slot = lax.broadcasted_iota(jnp.int32, (tq, topk), 1)
    lane_of = jnp.zeros((tq, topk), jnp.int32)
    base = jnp.zeros((tq, topk), jnp.int32)
    for l in range(LANES):
        c = cin[:, l:l + 1]
        le = c <= slot
        lane_of = lane_of + le.astype(jnp.int32)
        base = jnp.maximum(base, jnp.where(le, c, 0))
    lane_of = jnp.minimum(lane_of, LANES - 1)
    rank = slot - base
    nhalf = topk // LANES
    for r0 in range(0, tq, sub):
        rows = slice(r0, r0 + sub)
        lanes_h = [lane_of[rows, i * LANES:(i + 1) * LANES] for i in range(nhalf)]
        ranks_h = [rank[rows, i * LANES:(i + 1) * LANES] for i in range(nhalf)]

        def tile_body(c, carry):
            carry = list(carry)
            for u in range(tpc):
                off = pl.multiple_of(c * ts + u * LANES, LANES)
                carry[0] = carry[0] + key_ref[r0:r0 + sub, pl.ds(off, LANES)]
                for i in range(nhalf):
                    got = jnp.take_along_axis(carry[0], lanes_h[i], axis=1)
                    carry[1 + i] = carry[1 + i] + (got <= ranks_h[i]).astype(jnp.int32)
            return tuple(carry)

        res = lax.fori_loop(0, nchunk, tile_body, (jnp.zeros((sub, LANES), jnp.int32),) * (1 + nhalf))
        pos = jnp.concatenate([res[1 + i] * LANES + lanes_h[i] for i in range(nhalf)], 1)
        slot_sub = lax.broadcasted_iota(jnp.int32, (sub, topk), 1)
        sel_ref[rows, :] = jnp.where(slot_sub < total[rows], pos, -1)


def dsa_select(qi, small, kx, *, batch, seq):
    n = qi.shape[0]
    topk = min(TOPK_MAX, seq // 4)
    tq = min(128, seq)
    ts = min(512, seq)
    qpb = seq // tq
    return pl.pallas_call(
        functools.partial(_dsa_select_body, tq=tq, ts=ts, topk=topk, seq=seq),
        grid=(batch, qpb),
        in_specs=[pl.BlockSpec((tq, qi.shape[1]), lambda b, i: (b * qpb + i, 0)),
                  pl.BlockSpec((tq, LANES), lambda b, i: (b * qpb + i, 0)),
                  pl.BlockSpec((seq, LANES), lambda b, i: (b, 0))],
        out_specs=pl.BlockSpec((tq, topk), lambda b, i: (b * qpb + i, 0)),
        out_shape=jax.ShapeDtypeStruct((n, topk), jnp.int32),
        scratch_shapes=[pltpu.VMEM((tq, seq), jnp.int32), pltpu.VMEM((tq, 1), jnp.int32)],
        compiler_params=_cparams("parallel", "arbitrary"),
        name="dsa_select",
    )(qi, small, kx)


def _dsa_attn_body(sel_s, selv_ref, q2_ref, tbl_ref, o_ref, *stages, tqa, topk):
    heads = DSA_HEADS
    scale = DSA_HEAD_DIM ** -0.5
    lt = LAT_TILES

    def gather(i, buf):
        base = i * topk
        for j in range(topk):
            stages[buf][pl.ds(j * lt, lt), :] = tbl_ref[pl.ds(sel_s[base + j], lt), :]

    def attend(i0, bufs):
        qs = range(len(bufs))
        pk = [jnp.concatenate([stages[b][pl.ds(t, topk, stride=lt), :] for t in range(lt)], 1) for b in bufs]
        kv = [jnp.concatenate([lax.bitcast_convert_type(x << 16, F32),
                               lax.bitcast_convert_type(x & jnp.uint32(0xFFFF0000), F32)], 1).astype(BF16)
              for x in pk]
        rows = [pl.ds(pl.multiple_of((i0 + a) * heads, heads), heads) for a in qs]
        s = [lax.dot_general(q2_ref[rows[a], :], kv[a], (((1,), (1,)), ((), ())), preferred_element_type=F32) * scale
             for a in qs]
        s = [jnp.where(selv_ref[pl.ds(i0 + a, 1), :] >= 0, s[a], -jnp.inf) for a in qs]
        e = [jnp.exp(s[a] - jnp.max(s[a], -1, keepdims=True)) for a in qs]
        p = [(e[a] / jnp.sum(e[a], -1, keepdims=True)).astype(BF16) for a in qs]
        o = [jnp.dot(p[a], kv[a][:, :DSA_KV_RANK], preferred_element_type=F32) for a in qs]
        for a in qs:
            o_ref[rows[a], :] = o[a].astype(o_ref.dtype)

    gather(0, 0)
    gather(1, 1)

    def quad(pi, carry):
        i0 = 4 * pi
        gather(i0 + 2, 2)
        gather(i0 + 3, 3)
        attend(i0, (0, 1))
        gather(jnp.minimum(i0 + 4, tqa - 1), 0)
        gather(jnp.minimum(i0 + 5, tqa - 1), 1)
        attend(i0 + 2, (2, 3))
        return carry

    lax.fori_loop(0, tqa // 4, quad, 0)


def dsa_attend(sel, q2, table, *, batch, seq):
    n, topk = sel.shape
    tqa = min(64, seq)
    qpb = seq // tqa
    heads = DSA_HEADS
    return pl.pallas_call(
        functools.partial(_dsa_attn_body, tqa=tqa, topk=topk),
        grid=(batch, qpb),
        in_specs=[pl.BlockSpec((tqa * topk,), lambda b, i: (b * qpb + i,), memory_space=pltpu.SMEM),
                  pl.BlockSpec((tqa, topk), lambda b, i: (b * qpb + i, 0)),
                  pl.BlockSpec((tqa * heads, LAT_W), lambda b, i: (b * qpb + i, 0)),
                  pl.BlockSpec((seq * LAT_TILES, LANES), lambda b, i: (b, 0), pipeline_mode=pl.Buffered(1))],
        out_specs=pl.BlockSpec((tqa * heads, DSA_KV_RANK), lambda b, i: (b * qpb + i, 0)),
        out_shape=jax.ShapeDtypeStruct((n * heads, DSA_KV_RANK), BF16),
        scratch_shapes=[pltpu.VMEM((topk * LAT_TILES, LANES), jnp.uint32)] * 4,
        compiler_params=_cparams("parallel", "arbitrary"),
        name="dsa_attend",
    )((jnp.maximum(sel, 0) * LAT_TILES).reshape(-1), sel, q2, table.reshape(n * LAT_TILES, LANES))


def _head_mm_body(a_ref, w_ref, o_ref):
    o_ref[0] = jnp.dot(a_ref[...].astype(BF16), w_ref[0], preferred_element_type=F32).astype(o_ref.dtype)


def head_mm_out(a, w, *, tm=512):
    n = a.shape[0]
    heads, k, m = w.shape
    tm = min(tm, n)
    return pl.pallas_call(
        _head_mm_body,
        grid=(n // tm, heads),
        in_specs=[pl.BlockSpec((tm, k), lambda i, h: (i, h)), pl.BlockSpec((1, k, m), lambda i, h: (h, 0, 0))],
        out_specs=pl.BlockSpec((1, tm, m), lambda i, h: (h, i, 0)),
        out_shape=jax.ShapeDtypeStruct((heads, n, m), BF16),
        compiler_params=_cparams("parallel", "parallel"),
        name="head_mm_out",
    )(a, w)


def _head_mm_in_body(a_ref, w_ref, o_ref):
    o_ref[...] = jnp.dot(a_ref[0], w_ref[0], preferred_element_type=F32).astype(o_ref.dtype)


def head_mm_in(a, w, *, tm=512):
    heads, n, k = a.shape
    m = w.shape[2]
    tm = min(tm, n)
    return pl.pallas_call(
        _head_mm_in_body,
        grid=(n // tm, heads),
        in_specs=[pl.BlockSpec((1, tm, k), lambda i, h: (h, i, 0)), pl.BlockSpec((1, k, m), lambda i, h: (h, 0, 0))],
        out_specs=pl.BlockSpec((tm, m), lambda i, h: (i, h)),
        out_shape=jax.ShapeDtypeStruct((n, heads * m), BF16),
        compiler_params=_cparams("parallel", "parallel"),
        name="head_mm_in",
    )(a, w)


def dsa_layer(hbf, positions, p, *, batch, seq):
    n = hbf.shape[0]
    heads = DSA_HEADS
    q = mm(hbf, p["w_q"])
    ckv = mm(hbf, p["w_ckv"])
    qidx = mm(hbf, p["w_qidx"])
    small = mm(hbf, p["w_small"])
    qr, table, qi, small_o, kx = rowwise(
        _dsa_prep_fn, [q, ckv, qidx, small, positions.reshape(n, 1)], [],
        [p["kv_norm"], p["idx_ln_g"], p["idx_ln_b"], _dsa_patterns()],
        [(q.shape[1], BF16), (LAT_PACK, jnp.uint32), (qidx.shape[1], BF16), (LANES, F32), (LANES, BF16)],
        seq=seq, tm=128, name="dsa_prep")
    sel = dsa_select(qi, small_o, kx, batch=batch, seq=seq)
    q2 = head_mm_out(qr, p["w_q2"])
    q2 = q2.transpose(1, 0, 2).reshape(n * heads, LAT_W)
    o_lat = dsa_attend(sel, q2, table, batch=batch, seq=seq)
    o_lat = o_lat.reshape(n, heads, DSA_KV_RANK).transpose(1, 0, 2)
    out = head_mm_in(o_lat, p["w_uv"])
    return mm(out, p["w_o"])


def _s5_tables(p):
    hp = lax.Precision.HIGHEST
    lc, hg = S5_CHUNK, S5_GROUP
    lr, li = p["lam_re"], p["lam_im"]
    g, ps = lr.shape
    dt = jnp.exp(p["log_dt"])[:, None]
    mag = jnp.exp(lr * dt)
    ab_re, ab_im = mag * jnp.cos(li * dt), mag * jnp.sin(li * dt)
    den = lr * lr + li * li
    nr, ni = ab_re - 1.0, ab_im
    coef_re = (nr * lr + ni * li) / den
    coef_im = (ni * lr - nr * li) / den
    bb_re = coef_re[..., None] * p["b_re"] - coef_im[..., None] * p["b_im"]
    bb_im = coef_re[..., None] * p["b_im"] + coef_im[..., None] * p["b_re"]
    tau = jnp.arange(lc + 1, dtype=F32)[:, None, None]
    pmag = jnp.exp(lr * dt * tau)
    pr, pi = pmag * jnp.cos(li * dt * tau), pmag * jnp.sin(li * dt * tau)
    cr, ci = p["c_re"][None], p["c_im"][None]
    car = cr * pr[:, :, None, :] - ci * pi[:, :, None, :]
    cai = cr * pi[:, :, None, :] + ci * pr[:, :, None, :]
    kern = (jnp.einsum('tghp,gpk->tghk', car[:lc], bb_re, precision=hp)
            - jnp.einsum('tghp,gpk->tghk', cai[:lc], bb_im, precision=hp))
    kz = jnp.concatenate([kern, jnp.zeros((1,) + kern.shape[1:], F32)], 0)
    s_i = np.arange(lc)[:, None]
    t_i = np.arange(lc)[None, :]
    m = kz[np.where(t_i >= s_i, t_i - s_i, lc)]
    m = m.transpose(2, 0, 4, 1, 3).reshape(g, lc * hg, lc * hg)
    skip = jnp.tile(p["d"].reshape(g, 1, hg), (1, lc, 1)).reshape(g, 1, lc * hg)
    m = m + skip * jnp.eye(lc * hg, dtype=F32)[None]
    prs, pis = pr[lc - 1 - np.arange(lc)], pi[lc - 1 - np.arange(lc)]
    wre = prs[..., None] * bb_re[None] - pis[..., None] * bb_im[None]
    wim = prs[..., None] * bb_im[None] + pis[..., None] * bb_re[None]
    wre = wre.transpose(1, 0, 3, 2).reshape(g, lc * hg, ps)
    wim = wim.transpose(1, 0, 3, 2).reshape(g, lc * hg, ps)
    vre = car[1:].transpose(1, 3, 0, 2).reshape(g, ps, lc * hg)
    vim = (-cai[1:]).transpose(1, 3, 0, 2).reshape(g, ps, lc * hg)
    eye2 = jnp.eye(2, dtype=F32)[None, :, None, :, None]

    def pair(z):
        r, c = z.shape[1:]
        return (z.reshape(g // 2, 2, r, 1, c) * eye2).reshape(g // 2, 2 * r, 2 * c).astype(BF16)

    are = pr[lc].reshape(g // 2, 1, 2 * ps)
    aim = pi[lc].reshape(g // 2, 1, 2 * ps)
    return pair(m), pair(wre), pair(wim), pair(vre), pair(vim), are, aim


def _gelu_tanh(y):
    return 0.5 * y * (1.0 + jnp.tanh(math.sqrt(2.0 / math.pi) * (y + 0.044715 * (y * y * y))))


def _s5_body(u_ref, m_ref, wre_ref, wim_ref, vre_ref, vim_ref, are_ref, aim_ref, z_ref, xre, xim, sre, sim, *, pairs):
    nc = u_ref.shape[1]
    for q in range(pairs):
        u = u_ref[q]
        xre[q] = jnp.dot(u, wre_ref[q], preferred_element_type=F32)
        xim[q] = jnp.dot(u, wim_ref[q], preferred_element_type=F32)
    ar = [are_ref[q] for q in range(pairs)]
    ai = [aim_ref[q] for q in range(pairs)]

    def step(c, carry):
        new = []
        row = pl.ds(c, 1)
        for q in range(pairs):
            re, im = carry[2 * q], carry[2 * q + 1]
            sre[q, row, :] = re
            sim[q, row, :] = im
            new.append(ar[q] * re - ai[q] * im + xre[q, row, :])
            new.append(ar[q] * im + ai[q] * re + xim[q, row, :])
        return tuple(new)

    zero = jnp.zeros((1, are_ref.shape[-1]), F32)
    lax.fori_loop(0, nc, step, (zero,) * (2 * pairs), unroll=8)
    for q in range(pairs):
        y = (jnp.dot(u_ref[q], m_ref[q], preferred_element_type=F32)
             + jnp.dot(sre[q].astype(BF16), vre_ref[q], preferred_element_type=F32)
             + jnp.dot(sim[q].astype(BF16), vim_ref[q], preferred_element_type=F32))
        z_ref[q] = _gelu_tanh(y).astype(z_ref.dtype)


def s5_layer(hbf, p, *, batch, seq):
    n, D = hbf.shape
    lc, hg = S5_CHUNK, S5_GROUP
    g2 = D // hg // 2
    pw = 2 * lc * hg
    tabs = _s5_tables(p)
    nc = seq // lc
    u = hbf.reshape(n // lc, lc, g2, 2, hg).transpose(2, 0, 3, 1, 4).reshape(g2, n // lc, pw)
    pairs = S5_PAIRS_PER_STEP
    sw = 2 * S5_STATE
    wspec = lambda r, c: pl.BlockSpec((pairs, r, c), lambda i, b: (i, 0, 0))
    z = pl.pallas_call(
        functools.partial(_s5_body, pairs=pairs),
        grid=(g2 // pairs, batch),
        in_specs=[pl.BlockSpec((pairs, nc, pw), lambda i, b: (i, b, 0)),
                  wspec(pw, pw), wspec(pw, sw), wspec(pw, sw), wspec(sw, pw), wspec(sw, pw),
                  wspec(1, sw), wspec(1, sw)],
        out_specs=pl.BlockSpec((pairs, nc, pw), lambda i, b: (i, b, 0)),
        out_shape=jax.ShapeDtypeStruct((g2, n // lc, pw), BF16),
        scratch_shapes=[pltpu.VMEM((pairs, nc, sw), F32)] * 4,
        compiler_params=_cparams("parallel", "parallel"),
        name="s5_scan",
    )(u, *tabs)
    z = z.reshape(g2, n // lc, 2, lc, hg).transpose(1, 3, 0, 2, 4).reshape(n, D)
    zz = mm(z, p["w_glu"])
    return rowwise(lambda a: (a[:, :D] * jax.nn.sigmoid(a[:, D:]),), [zz], [], [], [(D, F32)], seq=seq, tm=256,
                   name="s5_glu")[0]


def _modulate(x, sc, sh, dtype, *, seq):
    d = x.shape[1]
    return rowwise(lambda xv, s, h: (xv * (1.0 + s) + h,), [x], [sc, sh], [], [(d, dtype)], seq=seq, tm=256,
                   name="modulate")[0]


def kernel(x, c, positions, ada_w, ada_b, ln_g, ln_b, router_w, router_bias, moe_w_gate, moe_w_up, moe_w_down,
           rwkv_mu, rwkv_w_rkv, rwkv_w_o, rwkv_w0, rwkv_w1, rwkv_w2, rwkv_a0, rwkv_a1, rwkv_a2, rwkv_g1, rwkv_g2,
           rwkv_k_k, rwkv_k_a, rwkv_r_k, rwkv_gn_g, rwkv_gn_b, rwkv_v0, rwkv_v1, rwkv_v2, dsa_w_in, dsa_kv_norm,
           dsa_w_uk, dsa_w_uv, dsa_idx_ln_g, dsa_idx_ln_b, dsa_w_o, s5_lam_re, s5_lam_im, s5_log_dt, s5_b_re,
           s5_b_im, s5_c_re, s5_c_im, s5_d, s5_w_glu):
    batch, seq, d = x.shape
    depth = ada_w.shape[0]
    n = batch * seq
    xf = x.reshape(n, d)
    mod = ada_mod(c, ada_w, ada_b)[:, :batch].reshape(depth, batch, 1, 6, d)
    v_first = None
    for i in range(depth):
        kind, j = i % N_MIXERS, i // N_MIXERS
        sh1, sc1, g1, sh2, sc2, g2 = (mod[i, :, :, m] for m in range(6))
        if kind == 0:
            p = dict(mu=rwkv_mu[j], w_r=rwkv_w_rkv[j, 0].astype(BF16), w_k=rwkv_w_rkv[j, 1].astype(BF16),
                     w_v=rwkv_w_rkv[j, 2].astype(BF16), w_o=rwkv_w_o[j].astype(BF16),
                     w0=rwkv_w0[j], w1=_wpad(rwkv_w1[j]), w2=_kpad(rwkv_w2[j]),
                     a0=rwkv_a0[j], a1=_wpad(rwkv_a1[j]), a2=_kpad(rwkv_a2[j]),
                     g1=_wpad(rwkv_g1[j]), g2=_kpad(rwkv_g2[j]),
                     k_k=rwkv_k_k[j], k_a=rwkv_k_a[j], r_k=rwkv_r_k[j], gn_g=rwkv_gn_g[j], gn_b=rwkv_gn_b[j],
                     v0=None)
            if j > 0:
                p.update(v0=rwkv_v0[j - 1], v1=_wpad(rwkv_v1[j - 1]), v2=_kpad(rwkv_v2[j - 1]))
            y, v_first = rwkv_layer(xf, sc1, sh1, v_first, p, batch=batch, seq=seq)
        elif kind == 1:
            w_in = dsa_w_in[j]
            hq = DSA_HEADS * DSA_HEAD_DIM
            o1 = hq + DSA_KV_RANK
            o2 = o1 + DSA_ROPE_DIM
            o3 = o2 + IDX_HEADS * IDX_DIM
            o4 = o3 + IDX_DIM
            gap = jnp.zeros((d, SM_R0 - SM_W0 - IDX_HEADS), F32)
            w_small = jnp.concatenate([w_in[:, o3:o4], w_in[:, o4:], gap, w_in[:, o1:o2]], axis=1)
            rope_pass = jnp.zeros((DSA_ROPE_DIM, LAT_W), F32).at[
                jnp.arange(DSA_ROPE_DIM), DSA_KV_RANK + SM_R0 + jnp.arange(DSA_ROPE_DIM)].set(1.0)
            w_uk_pad = jnp.pad(dsa_w_uk[j], ((0, 0), (0, 0), (0, LAT_W - DSA_KV_RANK)))
            w_q2 = jnp.concatenate([jnp.broadcast_to(rope_pass, (DSA_HEADS,) + rope_pass.shape), w_uk_pad], axis=1)
            lane_pad = lambda v: jnp.pad(v, (0, LANES - v.shape[0])).reshape(1, LANES)
            p = dict(w_q=w_in[:, :hq].astype(BF16), w_ckv=w_in[:, hq:o1].astype(BF16),
                     w_qidx=w_in[:, o2:o3].astype(BF16), w_small=w_small.astype(BF16),
                     kv_norm=dsa_kv_norm[j].reshape(1, -1), w_q2=w_q2.astype(BF16), w_uv=dsa_w_uv[j].astype(BF16),
                     idx_ln_g=lane_pad(dsa_idx_ln_g[j]), idx_ln_b=lane_pad(dsa_idx_ln_b[j]),
                     w_o=dsa_w_o[j].astype(BF16))
            hbf = _modulate(xf, sc1, sh1, BF16, seq=seq)
            y = dsa_layer(hbf, positions, p, batch=batch, seq=seq)
        else:
            p = dict(lam_re=s5_lam_re[j], lam_im=s5_lam_im[j], log_dt=s5_log_dt[j], b_re=s5_b_re[j],
                     b_im=s5_b_im[j], c_re=s5_c_re[j], c_im=s5_c_im[j], d=s5_d[j], w_glu=s5_w_glu[j].astype(BF16))
            hbf = _modulate(xf, sc1, sh1, BF16, seq=seq)
            y = s5_layer(hbf, p, batch=batch, seq=seq)
        xf, hbf = post_norm_mix(xf, y, g1, sc2, sh2, ln_g[i, 0:1], ln_b[i, 0:1], seq=seq)
        y0, y1, gates = moe_ffn(xf, hbf, sc2, sh2, router_w, router_bias, moe_w_gate[i].astype(BF16),
                                moe_w_up[i].astype(BF16), moe_w_down[i].astype(BF16), seq=seq)
        xf = post_norm_moe(xf, y0, y1, gates, g2, ln_g[i, 1:2], ln_b[i, 1:2], seq=seq)
    return xf.reshape(batch, seq, d)
```

```python
import functools
import math

import numpy as np
import jax
import jax.numpy as jnp
from jax import lax
from jax.experimental import pallas as pl
from jax.experimental.pallas import tpu as pltpu

F32, BF16 = jnp.float32, jnp.bfloat16

DEPTH = 4
N_MIXERS = 3
RWKV_HEAD = 64
RWKV_GN_EPS = 64e-5
DSA_HEADS = 16
DSA_HEAD_DIM = 128
DSA_ROPE_DIM = 32
DSA_NOPE_DIM = 96
DSA_V_DIM = 128
DSA_KV_RANK = 512
IDX_HEADS = 16
IDX_DIM = 64
IDX_ROPE_DIM = 16
TOPK_MAX = 256
Q_BLOCK = 128
S5_GROUP = 16
S5_STATE = 64
S5_CHUNK = 16
S5_UNIT = 8
N_EXPERTS = 32
N_GROUPS = 4
EXPERTS_PER_GROUP = 8
TOP_K = 2
ROPE_THETA = 500000.0
LN_EPS = 1e-5
DEEPNORM_ALPHA = (2 * DEPTH) ** 0.25

LANES = 128
VMEM_LIMIT = 48 * 1024 * 1024
MOE_ROWS = 256
WKV_CHUNK = 64
WKV_HEADS_PER_STEP = 4
WKV_GROUPS_PER_STEP = 1
WKV_ROWS_PER_STEP = 512


def _cparams(*sem):
    return pltpu.CompilerParams(dimension_semantics=sem, vmem_limit_bytes=VMEM_LIMIT)


def _pick(n, cands):
    for c in cands:
        if n % c == 0:
            return c
    return n


def _mm_body(a_ref, w_ref, o_ref, *, act):
    acc = jnp.dot(a_ref[...].astype(BF16), w_ref[...], preferred_element_type=F32)
    if act == "tanh":
        acc = jnp.tanh(acc)
    elif act == "sigmoid":
        acc = jax.nn.sigmoid(acc)
    o_ref[...] = acc.astype(o_ref.dtype)


def mm(a, w, *, out_dtype=F32, act=None, tm=512):
    m, k = a.shape
    n = w.shape[1]
    tm = _pick(m, (tm, 256, 128, 64, 32, 16, 8))
    tn = _pick(n, (512, 256, 128))
    return pl.pallas_call(
        functools.partial(_mm_body, act=act),
        grid=(m // tm, n // tn),
        in_specs=[pl.BlockSpec((tm, k), lambda i, j: (i, 0)), pl.BlockSpec((k, tn), lambda i, j: (0, j))],
        out_specs=pl.BlockSpec((tm, tn), lambda i, j: (i, j)),
        out_shape=jax.ShapeDtypeStruct((m, n), out_dtype),
        compiler_params=_cparams("parallel", "parallel"),
        name="mm",
    )(a, w)


def _wpad(w):
    n = w.shape[1]
    npad = -(-n // LANES) * LANES
    w = w.astype(BF16)
    return w if npad == n else jnp.pad(w, ((0, 0), (0, npad - n)))


def _kpad(w):
    k = w.shape[0]
    kpad = -(-k // LANES) * LANES
    w = w.astype(BF16)
    return w if kpad == k else jnp.pad(w, ((0, kpad - k), (0, 0)))


def rowwise(fn, rows, perbatch, consts, outs, *, seq, tm, name):
    n = rows[0].shape[0]
    tm = min(tm, seq)
    tpb = seq // tm
    nr, nb, nc = len(rows), len(perbatch), len(consts)

    def body(*refs):
        vals = [r[...] for r in refs[:nr]]
        vals += [r[0] for r in refs[nr:nr + nb]]
        vals += [r[...] for r in refs[nr + nb:nr + nb + nc]]
        res = fn(*vals)
        for o, v in zip(refs[nr + nb + nc:], res):
            o[...] = v.astype(o.dtype)

    in_specs = [pl.BlockSpec((tm, r.shape[1]), lambda i: (i, 0)) for r in rows]
    in_specs += [pl.BlockSpec((1, 1, p.shape[-1]), lambda i: (i // tpb, 0, 0)) for p in perbatch]
    in_specs += [pl.BlockSpec(c.shape, lambda i: (0, 0)) for c in consts]
    res = pl.pallas_call(
        body,
        grid=(n // tm,),
        in_specs=in_specs,
        out_specs=[pl.BlockSpec((tm, w), lambda i: (i, 0)) for w, _ in outs],
        out_shape=[jax.ShapeDtypeStruct((n, w), dt) for w, dt in outs],
        compiler_params=_cparams("parallel"),
        name=name,
    )(*rows, *perbatch, *consts)
    return res


def _layer_norm(z, g, b):
    mu = jnp.mean(z, -1, keepdims=True)
    zc = z - mu
    var = jnp.mean(zc * zc, -1, keepdims=True)
    return zc * lax.rsqrt(var + LN_EPS) * g + b


def _split_dot(x, m):
    xh = x.astype(BF16)
    xl = (x - xh.astype(F32)).astype(BF16)
    return jnp.dot(xh, m, preferred_element_type=F32) + jnp.dot(xl, m, preferred_element_type=F32)


def _seg_sum(x, ones_bd):
    parts = [_split_dot(x[:, j:j + LANES], ones_bd) for j in range(0, x.shape[1], LANES)]
    return jnp.concatenate(parts, -1)


def _head_ones():
    i = np.arange(LANES)
    return jnp.asarray((i[:, None] // RWKV_HEAD) == (i[None, :] // RWKV_HEAD), BF16)


def _ada_body(c_ref, w_ref, b_ref, o_ref):
    acc = jnp.dot(c_ref[...], w_ref[0], preferred_element_type=F32, precision=lax.Precision.HIGHEST)
    o_ref[0] = acc + b_ref[0]


def ada_mod(c, ada_w, ada_b):
    depth, d, n = ada_w.shape
    b = c.shape[0]
    cp = jnp.pad(c, ((0, 8 - b), (0, 0)))
    tn = 1024
    return pl.pallas_call(
        _ada_body,
        grid=(depth, n // tn),
        in_specs=[pl.BlockSpec((8, d), lambda i, j: (0, 0)),
                  pl.BlockSpec((1, d, tn), lambda i, j: (i, 0, j)),
                  pl.BlockSpec((1, 1, tn), lambda i, j: (i, 0, j))],
        out_specs=pl.BlockSpec((1, 8, tn), lambda i, j: (i, 0, j)),
        out_shape=jax.ShapeDtypeStruct((depth, 8, n), F32),
        compiler_params=_cparams("parallel", "parallel"),
        name="ada_mod",
    )(cp, ada_w, ada_b.reshape(depth, 1, n))


def post_norm_mix(x, y, gate, sc, sh, lng, lnb, *, seq):
    d = x.shape[1]

    def fn(xv, yv, g, s, h, lg, lb):
        xn = _layer_norm(DEEPNORM_ALPHA * xv + (1.0 + g) * yv, lg, lb)
        return xn, xn * (1.0 + s) + h

    return rowwise(fn, [x, y], [gate, sc, sh], [lng, lnb], [(d, F32), (d, BF16)], seq=seq, tm=256, name="post_norm_mix")


def post_norm_moe(x, y0, y1, gates, gate, lng, lnb, *, seq):
    d = x.shape[1]

    def fn(xv, a, b, gt, g, lg, lb):
        y = a * gt[:, 0:1] + b * gt[:, 1:2]
        return (_layer_norm(DEEPNORM_ALPHA * xv + (1.0 + g) * y, lg, lb),)

    return rowwise(fn, [x, y0, y1, gates], [gate], [lng, lnb], [(d, F32)], seq=seq, tm=256, name="post_norm_moe")[0]


def _router_fn(xv, s, h, rw, rb):
    hf = xv * (1.0 + s) + h
    logits = jnp.dot(hf, rw, preferred_element_type=F32, precision=lax.Precision.HIGHEST)
    scores = jax.nn.sigmoid(logits)
    biased = scores + rb
    lane = lax.broadcasted_iota(jnp.int32, biased.shape, 1)
    neg = jnp.float32(-jnp.inf)
    big = jnp.int32(1 << 20)
    best = bi1 = bi2 = None
    for g in range(N_GROUPS):
        ing = (lane >= g * EXPERTS_PER_GROUP) & (lane < (g + 1) * EXPERTS_PER_GROUP)
        v = jnp.where(ing, biased, neg)
        m1 = jnp.max(v, -1, keepdims=True)
        i1 = jnp.min(jnp.where(v == m1, lane, big), -1, keepdims=True)
        v2 = jnp.where(lane == i1, neg, v)
        m2 = jnp.max(v2, -1, keepdims=True)
        i2 = jnp.min(jnp.where(v2 == m2, lane, big), -1, keepdims=True)
        gs = m1 + m2
        if g == 0:
            best, bi1, bi2 = gs, i1, i2
        else:
            better = gs > best
            best = jnp.where(better, gs, best)
            bi1 = jnp.where(better, i1, bi1)
            bi2 = jnp.where(better, i2, bi2)
    g1 = jnp.sum(jnp.where(lane == bi1, scores, 0.0), -1, keepdims=True)
    g2 = jnp.sum(jnp.where(lane == bi2, scores, 0.0), -1, keepdims=True)
    tot = g1 + g2
    two = lax.broadcasted_iota(jnp.int32, (biased.shape[0], TOP_K), 1)
    return jnp.where(two == 0, bi1, bi2), jnp.where(two == 0, g1 / tot, g2 / tot)


def _expert_body(be_ref, nb_ref, x_ref, wg_ref, wu_ref, wd_ref, o_ref):
    del be_ref

    @pl.when(pl.program_id(0) < nb_ref[0])
    def _():
        x = x_ref[...]
        g = jnp.dot(x, wg_ref[0], preferred_element_type=F32)
        u = jnp.dot(x, wu_ref[0], preferred_element_type=F32)
        hid = (g * jax.nn.sigmoid(g) * u).astype(BF16)
        o_ref[...] = jnp.dot(hid, wd_ref[0], preferred_element_type=F32).astype(o_ref.dtype)

    @pl.when(pl.program_id(0) >= nb_ref[0])
    def _():
        o_ref[...] = jnp.zeros_like(o_ref)


def moe_ffn(x, hbf, sc, sh, router_w, router_bias, wg, wu, wd, *, seq):
    n, d = x.shape
    e = router_w.shape[1]
    idx, gates = rowwise(_router_fn, [x], [sc, sh], [router_w, router_bias.reshape(1, e)],
                         [(TOP_K, jnp.int32), (TOP_K, F32)], seq=seq, tm=256, name="router")
    bm = MOE_ROWS
    n_slots = n * TOP_K
    flat_e = idx.reshape(-1)
    onehot = (flat_e[:, None] == jnp.arange(e, dtype=jnp.int32)[None, :]).astype(jnp.int32)
    csum = jnp.cumsum(onehot, axis=0)
    pos = jnp.sum(onehot * csum, axis=1) - 1
    counts = csum[-1]
    padded = (counts + bm - 1) // bm * bm
    pend = jnp.cumsum(padded)
    pstart = pend - padded
    dest = pstart[flat_e] + pos
    n_blocks = n_slots // bm + e
    n_pad = n_blocks * bm
    slot_tok = jnp.zeros((n_pad,), jnp.int32).at[dest].set(jnp.arange(n_slots, dtype=jnp.int32) // TOP_K)
    block_start = jnp.arange(n_blocks, dtype=jnp.int32) * bm
    block_expert = jnp.minimum(jnp.sum((pend[None, :] <= block_start[:, None]).astype(jnp.int32), axis=1), e - 1)
    used_blocks = (pend[-1] // bm).astype(jnp.int32).reshape(1)
    xs = jnp.take(hbf, slot_tok, axis=0)
    de = wg.shape[2]
    yb = pl.pallas_call(
        _expert_body,
        grid_spec=pltpu.PrefetchScalarGridSpec(
            num_scalar_prefetch=2,
            grid=(n_blocks,),
            in_specs=[pl.BlockSpec((bm, d), lambda i, be, nb: (i, 0)),
                      pl.BlockSpec((1, d, de), lambda i, be, nb: (be[i], 0, 0)),
                      pl.BlockSpec((1, d, de), lambda i, be, nb: (be[i], 0, 0)),
                      pl.BlockSpec((1, de, d), lambda i, be, nb: (be[i], 0, 0))],
            out_specs=pl.BlockSpec((bm, d), lambda i, be, nb: (i, 0)),
        ),
        out_shape=jax.ShapeDtypeStruct((n_pad, d), BF16),
        compiler_params=_cparams("arbitrary"),
        name="moe_experts",
    )(block_expert, used_blocks, xs, wg, wu, wd)
    dest2 = dest.reshape(n, TOP_K)
    return jnp.take(yb, dest2[:, 0], axis=0), jnp.take(yb, dest2[:, 1], axis=0), gates


def _rwkv_prep_body(x_ref, xp_ref, sc_ref, sh_ref, mu_ref, *outs, tpb):
    sc = 1.0 + sc_ref[0]
    sh = sh_ref[0]
    h = x_ref[...] * sc + sh
    prev_row = xp_ref[7:8, :] * sc + sh
    prev_row = jnp.where(pl.program_id(0) % tpb == 0, 0.0, prev_row)
    rowid = lax.broadcasted_iota(jnp.int32, h.shape, 0)
    hprev = jnp.where(rowid == 0, prev_row, pltpu.roll(h, 1, 0))
    dx = hprev - h
    for m, o in enumerate(outs):
        o[...] = (h + dx * mu_ref[m:m + 1, :]).astype(o.dtype)


def rwkv_prep(x, sc, sh, mu, *, seq):
    n, d = x.shape
    tm = min(256, seq)
    tpb = seq // tm
    return pl.pallas_call(
        functools.partial(_rwkv_prep_body, tpb=tpb),
        grid=(n // tm,),
        in_specs=[pl.BlockSpec((tm, d), lambda i: (i, 0)),
                  pl.BlockSpec((8, d), lambda i: (jnp.maximum(i * (tm // 8) - 1, 0), 0)),
                  pl.BlockSpec((1, 1, d), lambda i: (i // tpb, 0, 0)),
                  pl.BlockSpec((1, 1, d), lambda i: (i // tpb, 0, 0)),
                  pl.BlockSpec((6, d), lambda i: (0, 0))],
        out_specs=[pl.BlockSpec((tm, d), lambda i: (i, 0))] * 6,
        out_shape=[jax.ShapeDtypeStruct((n, d), BF16)] * 6,
        compiler_params=_cparams("parallel"),
        name="rwkv_prep",
    )(x, x, sc, sh, mu)


def _softplus(z):
    return jnp.maximum(z, 0.0) + jnp.log(1.0 + jnp.exp(-jnp.abs(z)))


def _rwkv_gates_fn(k0, lw, al, *rest, has_vres):
    if has_vres:
        v, vfirst, vl, prm, ones_bd = rest
    else:
        prm, ones_bd = rest
    w0, a0, k_k, k_a = prm[0:1], prm[1:2], prm[2:3], prm[3:4]
    log_w = -_softplus(-(w0 + lw)) - 0.5
    logdecay = -jnp.exp(log_w)
    a = jax.nn.sigmoid(a0 + al)
    kk = k0 * k_k
    nrm = jnp.maximum(jnp.sqrt(_seg_sum(kk * kk, ones_bd)), 1e-12)
    kk = kk / nrm
    k = k0 * (1.0 + (a - 1.0) * k_a)
    res = [logdecay, k, kk, kk * a]
    if has_vres:
        res.append(v + (vfirst - v) * jax.nn.sigmoid(prm[4:5] + vl))
    return res


def _wkv_body(r_ref, lw_ref, k_ref, v_ref, kk_ref, b_ref, y_ref, *s_refs, chunk, heads):
    hd = RWKV_HEAD
    rows = r_ref.shape[0]
    width = heads * hd
    hr = heads * chunk
    nt = (((1,), (1,)), ((), ()))

    @pl.when(pl.program_id(2) == 0)
    def _():
        for s_ref in s_refs:
            s_ref[...] = jnp.zeros_like(s_ref)

    ti = lax.broadcasted_iota(jnp.int32, (chunk, chunk), 0)
    si = lax.broadcasted_iota(jnp.int32, (chunk, chunk), 1)
    tri = (ti >= si).astype(BF16)
    ri = lax.broadcasted_iota(jnp.int32, (hr, width), 0)
    ci = lax.broadcasted_iota(jnp.int32, (hr, width), 1)
    head_mask = (ri // chunk == ci // hd).astype(F32)
    rr = lax.broadcasted_iota(jnp.int32, (hr, hr), 0)
    cc = lax.broadcasted_iota(jnp.int32, (hr, hr), 1)
    incl = rr % chunk >= cc % chunk
    strict = rr % chunk > cc % chunk
    eye = (rr == cc).astype(F32)

    def blocked(x):
        return (jnp.concatenate([x] * heads, 0) * head_mask).astype(BF16)

    def step(c, carry):
        sl = pl.ds(pl.multiple_of(c * chunk, chunk), chunk)
        for grp, s_ref in enumerate(s_refs):
            group_step(sl, slice(grp * width, (grp + 1) * width), s_ref)
        return carry

    def group_step(sl, ln, s_ref):
        lw = lw_ref[sl, ln]
        cs = _split_dot_left(tri, lw)
        ctot = cs[chunk - 1:chunk, :]
        g_inv = jnp.exp(-cs)
        g_end = jnp.exp(ctot - cs)
        kk = kk_ref[sl, ln]
        kv = k_ref[sl, ln]
        bv = b_ref[sl, ln]
        ar = jnp.concatenate([blocked(-kk * jnp.exp(cs - lw)), blocked(r_ref[sl, ln] * jnp.exp(cs))], 0)
        bk = jnp.concatenate([blocked(bv * g_inv), blocked(kv * g_inv)], 0)
        ends = jnp.concatenate([blocked(bv * g_end), blocked(kv * g_end)], 0)
        vb = blocked(v_ref[sl, ln])
        s_old = s_ref[...]
        m = lax.dot_general(ar, bk, nt, preferred_element_type=F32)
        a_ab = jnp.where(strict, m[:hr, :hr], 0.0)
        a_ak = jnp.where(strict, m[:hr, hr:], 0.0)
        a_r = jnp.concatenate([jnp.where(incl, m[hr:, :hr], 0.0), jnp.where(incl, m[hr:, hr:], 0.0)], 1)
        ars = lax.dot_general(ar, s_old.astype(BF16), nt, preferred_element_type=F32)
        rhs = ars[:hr] + jnp.dot(a_ak.astype(BF16), vb, preferred_element_type=F32)
        inv = eye + a_ab
        apow = a_ab.astype(BF16)
        for _ in range(int(math.log2(chunk)) - 1):
            apow = jnp.dot(apow, apow, preferred_element_type=F32).astype(BF16)
            inv = inv + jnp.dot(inv.astype(BF16), apow, preferred_element_type=F32)
        u = jnp.dot(inv.astype(BF16), rhs.astype(BF16), preferred_element_type=F32)
        uv = jnp.concatenate([u.astype(BF16), vb], 0)
        yb = ars[hr:] + jnp.dot(a_r.astype(BF16), uv, preferred_element_type=F32)
        y = yb[:chunk]
        for h in range(1, heads):
            y = y + yb[h * chunk:(h + 1) * chunk]
        y_ref[sl, ln] = y
        s_ref[...] = s_old * jnp.exp(ctot) + lax.dot_general(
            uv, ends, (((0,), (0,)), ((), ())), preferred_element_type=F32)

    lax.fori_loop(0, rows // chunk, step, 0)


def _split_dot_left(m, x):
    xh = x.astype(BF16)
    xl = (x - xh.astype(F32)).astype(BF16)
    return jnp.dot(m, xh, preferred_element_type=F32) + jnp.dot(m, xl, preferred_element_type=F32)


def wkv7(r, logdecay, k, v, kk, b, *, batch, seq):
    n, d = r.shape
    chunk = min(WKV_CHUNK, seq)
    rows = min(WKV_ROWS_PER_STEP, seq)
    heads = WKV_HEADS_PER_STEP
    groups = WKV_GROUPS_PER_STEP
    width = heads * RWKV_HEAD
    spb = seq // rows
    spec = pl.BlockSpec((rows, groups * width), lambda bi, hi, ti: (bi * spb + ti, hi))
    return pl.pallas_call(
        functools.partial(_wkv_body, chunk=chunk, heads=heads),
        grid=(batch, d // (groups * width), spb),
        in_specs=[spec] * 6,
        out_specs=spec,
        out_shape=jax.ShapeDtypeStruct((n, d), F32),
        scratch_shapes=[pltpu.VMEM((width, width), F32)] * groups,
        compiler_params=_cparams("parallel", "parallel", "arbitrary"),
        name="wkv7",
    )(r, logdecay, k, v, kk, b)


def _rwkv_out_fn(y, r, k, v, g, prm, ones_bd):
    r_k, gn_g, gn_b = prm[0:1], prm[1:2], prm[2:3]
    inv_n = 1.0 / RWKV_HEAD
    m_y = _seg_sum(y, ones_bd) * inv_n
    yc = y - m_y
    v_y = _seg_sum(yc * yc, ones_bd) * inv_n
    yn = yc * lax.rsqrt(v_y + RWKV_GN_EPS) * gn_g + gn_b
    bonus = _seg_sum(r * k * r_k, ones_bd) * v
    return ((yn + bonus) * g,)


def rwkv_layer(x, sc, sh, v_first, p, *, batch, seq):
    n, d = x.shape
    xr, xw, xk, xv, xa, xg = rwkv_prep(x, sc, sh, p["mu"], seq=seq)
    r = mm(xr, p["w_r"])
    k0 = mm(xk, p["w_k"])
    v = mm(xv, p["w_v"])
    lw = mm(mm(xw, p["w1"], out_dtype=BF16, act="tanh"), p["w2"])
    al = mm(mm(xa, p["a1"], out_dtype=BF16), p["a2"])
    g = mm(mm(xg, p["g1"], out_dtype=BF16, act="sigmoid"), p["g2"])
    ones_bd = _head_ones()
    has_vres = p["v0"] is not None
    outs = [(d, F32)] * (5 if has_vres else 4)
    if has_vres:
        vl = mm(mm(xv, p["v1"], out_dtype=BF16), p["v2"])
        prm = jnp.stack([p["w0"], p["a0"], p["k_k"], p["k_a"], p["v0"]])
        rows = [k0, lw, al, v, v_first, vl]
    else:
        prm = jnp.stack([p["w0"], p["a0"], p["k_k"], p["k_a"]])
        rows = [k0, lw, al]
    res = rowwise(functools.partial(_rwkv_gates_fn, has_vres=has_vres), rows, [], [prm, ones_bd], outs,
                  seq=seq, tm=128, name="rwkv_gates")
    logdecay, k, kk, b = res[:4]
    if has_vres:
        v = res[4]
    else:
        v_first = v
    y = wkv7(r, logdecay, k, v, kk, b, batch=batch, seq=seq)
    prm2 = jnp.stack([p["r_k"], p["gn_g"], p["gn_b"]])
    yg = rowwise(_rwkv_out_fn, [y, r, k, v, g], [], [prm2, ones_bd], [(d, BF16)], seq=seq, tm=128, name="rwkv_out")[0]
    return mm(yg, p["w_o"]), v_first


SM_W0 = IDX_DIM
SM_R0 = LANES - DSA_ROPE_DIM
LAT_W = 768
LAT_PACK = LAT_W // 2
LAT_TILES = LAT_PACK // LANES
DSA_QUERIES_PER_ATTEND = 4


def _dsa_patterns():
    fq = ROPE_THETA ** (-np.arange(DSA_ROPE_DIM // 2, dtype=np.float32) * np.float32(2.0 / DSA_ROPE_DIM))
    fi = ROPE_THETA ** (-np.arange(IDX_ROPE_DIM // 2, dtype=np.float32) * np.float32(2.0 / IDX_ROPE_DIM))
    hq, hi = DSA_ROPE_DIM // 2, IDX_ROPE_DIM // 2
    pat = np.zeros((16, LANES), np.float32)
    lane = np.arange(LANES)
    pat[0, :hq] = fq; pat[0, hq:2 * hq] = fq
    pat[1, :hq] = -1.0; pat[2, hq:2 * hq] = 1.0
    l64 = lane % IDX_DIM
    pat[3] = np.where(l64 < hi, fi[np.minimum(l64, hi - 1)], np.where(l64 < 2 * hi, fi[np.clip(l64 - hi, 0, hi - 1)], 0.0))
    pat[4] = np.where(l64 < hi, -1.0, 0.0); pat[5] = np.where((l64 >= hi) & (l64 < 2 * hi), 1.0, 0.0)
    pat[6, :hi] = fi; pat[6, hi:2 * hi] = fi
    pat[6, SM_R0:SM_R0 + hq] = fq; pat[6, SM_R0 + hq:] = fq
    pat[7, :hi] = -1.0; pat[8, hi:2 * hi] = 1.0
    pat[9, SM_R0:SM_R0 + hq] = -1.0; pat[10, SM_R0 + hq:] = 1.0
    pat[11, :IDX_DIM] = 1.0
    pat[12, SM_W0:SM_W0 + IDX_HEADS] = IDX_HEADS ** -0.5 * IDX_DIM ** -0.5
    pat[13, SM_R0:] = 1.0
    return jnp.asarray(pat)


def _rot(x, shift):
    return pltpu.roll(x, shift % x.shape[1], 1)


def _tile_lanes(v, reps):
    return jnp.concatenate([v] * reps, 1) if reps > 1 else v


def _dsa_prep_fn(q, ckv, qidx, small, pos, kvn, lng, lnb, pat):
    posf = pos.astype(F32)
    hq, hi = DSA_ROPE_DIM // 2, IDX_ROPE_DIM // 2
    ang = posf * pat[0:1]
    c, s = jnp.cos(ang), jnp.sin(ang)
    reps = q.shape[1] // LANES
    qr = (q * _tile_lanes(c, reps) + _rot(q, -hq) * _tile_lanes(s * pat[1:2], reps)
          + _rot(q, hq) * _tile_lanes(s * pat[2:3], reps))
    ang = posf * pat[3:4]
    c, s = jnp.cos(ang), jnp.sin(ang)
    reps = qidx.shape[1] // LANES
    qi = (qidx * _tile_lanes(c, reps) + _rot(qidx, -hi) * _tile_lanes(s * pat[4:5], reps)
          + _rot(qidx, hi) * _tile_lanes(s * pat[5:6], reps))
    ckn = ckv * lax.rsqrt(jnp.mean(ckv * ckv, -1, keepdims=True) + 1e-6) * kvn
    mk = pat[11:12]
    inv = 1.0 / IDX_DIM
    mu = jnp.sum(small * mk, -1, keepdims=True) * inv
    dv = (small - mu) * mk
    var = jnp.sum(dv * dv, -1, keepdims=True) * inv
    y = (dv * lax.rsqrt(var + LN_EPS) * lng + lnb) * mk + small * pat[12:13] + small * pat[13:14]
    ang = posf * pat[6:7]
    c, s = jnp.cos(ang), jnp.sin(ang)
    so = (y * c + _rot(y, -hi) * (s * pat[7:8]) + _rot(y, hi) * (s * pat[8:9])
          + _rot(y, -hq) * (s * pat[9:10]) + _rot(y, hq) * (s * pat[10:11]))
    lat = jnp.concatenate([ckn, so * pat[13:14], jnp.zeros_like(so)], 1)
    lo = lax.bitcast_convert_type(lat[:, :LAT_PACK].astype(BF16).astype(F32), jnp.uint32) >> 16
    hi_b = lax.bitcast_convert_type(lat[:, LAT_PACK:].astype(BF16).astype(F32), jnp.uint32) & jnp.uint32(0xFFFF0000)
    return qr, lo | hi_b, qi, so, so


def _sortable(x):
    b = lax.bitcast_convert_type(x, jnp.int32)
    return jnp.where(b < 0, b ^ jnp.int32(0x7FFFFFFF), b)


def _dsa_select_body(qi_ref, sm_ref, kx_ref, sel_ref, key_ref, ib_ref, *, tq, ts, topk, seq):
    sub = 32
    t0 = pl.program_id(1) * tq
    nchunk = (t0 + tq + ts - 1) // ts
    qpos = t0 + lax.broadcasted_iota(jnp.int32, (tq, 1), 0)
    w = sm_ref[:, SM_W0:SM_W0 + IDX_HEADS]
    neg = jnp.float32(-jnp.inf)

    def score_chunk(c, carry):
        off = pl.multiple_of(c * ts, ts)
        ks = kx_ref[pl.ds(off, ts), :][:, :IDX_DIM]
        acc = jnp.zeros((tq, ts), F32)
        for h in range(IDX_HEADS):
            s = lax.dot_general(qi_ref[:, h * IDX_DIM:(h + 1) * IDX_DIM], ks, (((1,), (1,)), ((), ())),
                                preferred_element_type=F32)
            acc = acc + w[:, h:h + 1] * jnp.maximum(s, 0.0)
        kpos = off + lax.broadcasted_iota(jnp.int32, (1, ts), 1)
        key_ref[:, pl.ds(off, ts)] = _sortable(jnp.where(kpos <= qpos, acc, neg))
        return carry

    lax.fori_loop(0, nchunk, score_chunk, 0)

    tpc = ts // LANES
    lane_iota = lax.broadcasted_iota(jnp.int32, (1, LANES), 1)

    def count(pred):
        def body(c, acc):
            for u in range(tpc):
                off = pl.multiple_of(c * ts + u * LANES, LANES)
                acc = acc + pred(key_ref[:, pl.ds(off, LANES)], off + lane_iota).astype(jnp.int32)
            return acc
        acc = lax.fori_loop(0, nchunk, body, jnp.zeros((tq, LANES), jnp.int32))
        return jnp.sum(acc, -1, keepdims=True)

    sign = jnp.int32(-2 ** 31)
    thr_u = jnp.zeros((tq, 1), jnp.int32)
    for bit in range(31, -1, -1):
        cand_u = thr_u | jnp.int32(np.uint32(1 << bit).astype(np.int32))
        cand = cand_u ^ sign
        cnt = count(lambda k, kp, cand=cand: k >= cand)
        thr_u = jnp.where(cnt >= topk, cand_u, thr_u)
    thr = thr_u ^ sign
    need = topk - count(lambda k, kp: k > thr)
    n_eq = count(lambda k, kp: k == thr)

    ib_ref[...] = jnp.full((tq, 1), seq, jnp.int32)

    @pl.when(jnp.max(n_eq - need) > 0)
    def _():
        ib = jnp.zeros((tq, 1), jnp.int32)
        for bit in range(int(math.log2(seq)) - 1, -1, -1):
            cand = ib | jnp.int32(1 << bit)
            cnt = count(lambda k, kp, cand=cand: (k == thr) & (kp < cand))
            ib = jnp.where(cnt < need, cand, ib)
        ib_ref[...] = ib

    ib = ib_ref[...]

    def mask_tiles(c, acc):
        for u in range(tpc):
            off = pl.multiple_of(c * ts + u * LANES, LANES)
            kpos = off + lane_iota
            k = key_ref[:, pl.ds(off, LANES)]
            m = (((k > thr) | ((k == thr) & (kpos <= ib))) & (kpos <= qpos)).astype(jnp.int32)
            key_ref[:, pl.ds(off, LANES)] = m
            acc = acc + m
        return acc

    colcnt = lax.fori_loop(0, nchunk, mask_tiles, jnp.zeros((tq, LANES), jnp.int32))
    li = lax.broadcasted_iota(jnp.int32, (LANES, LANES), 0)
    lj = lax.broadcasted_iota(jnp.int32, (LANES, LANES), 1)
    cin = jnp.dot(colcnt.astype(BF16), (li <= lj).astype(BF16), preferred_element_type=F32).astype(jnp.int32)
    total = cin[:, LANES - 1:LANES]
    slot = lax.broadcasted_iota(jnp.int32, (tq, topk), 1)
    lane_of = jnp.zeros((tq, topk), jnp.int32)
    base = jnp.zeros((tq, topk), jnp.int32)
    for l in range(LANES):
        c = cin[:, l:l + 1]
        le = c <= slot
        lane_of = lane_of + le.astype(jnp.int32)
        base = jnp.maximum(base, jnp.where(le, c, 0))
    lane_of = jnp.minimum(lane_of, LANES - 1)
    rank = slot - base
    nhalf = topk // LANES
    for r0 in range(0, tq, sub):
        rows = slice(r0, r0 + sub)
        lanes_h = [lane_of[rows, i * LANES:(i + 1) * LANES] for i in range(nhalf)]
        ranks_h = [rank[rows, i * LANES:(i + 1) * LANES] for i in range(nhalf)]

        def tile_body(c, carry):
            carry = list(carry)
            for u in range(tpc):
                off = pl.multiple_of(c * ts + u * LANES, LANES)
                carry[0] = carry[0] + key_ref[r0:r0 + sub, pl.ds(off, LANES)]
                for i in range(nhalf):
                    got = jnp.take_along_axis(carry[0], lanes_h[i], axis=1)
                    carry[1 + i] = carry[1 + i] + (got <= ranks_h[i]).astype(jnp.int32)
            return tuple(carry)

        res = lax.fori_loop(0, nchunk, tile_body, (jnp.zeros((sub, LANES), jnp.int32),) * (1 + nhalf))
        pos = jnp.concatenate([res[1 + i] * LANES + lanes_h[i] for i in range(nhalf)], 1)
        slot_sub = lax.broadcasted_iota(jnp.int32, (sub, topk), 1)
        sel_ref[rows, :] = jnp.where(slot_sub < total[rows], pos, -1)


def dsa_select(qi, small, kx, *, batch, seq):
    n = qi.shape[0]
    topk = min(TOPK_MAX, seq // 4)
    tq = min(128, seq)
    ts = min(512, seq)
    qpb = seq // tq
    return pl.pallas_call(
        functools.partial(_dsa_select_body, tq=tq, ts=ts, topk=topk, seq=seq),
        grid=(batch, qpb),
        in_specs=[pl.BlockSpec((tq, qi.shape[1]), lambda b, i: (b * qpb + i, 0)),
                  pl.BlockSpec((tq, LANES), lambda b, i: (b * qpb + i, 0)),
                  pl.BlockSpec((seq, LANES), lambda b, i: (b, 0))],
        out_specs=pl.BlockSpec((tq, topk), lambda b, i: (b * qpb + i, 0)),
        out_shape=jax.ShapeDtypeStruct((n, topk), jnp.int32),
        scratch_shapes=[pltpu.VMEM((tq, seq), jnp.int32), pltpu.VMEM((tq, 1), jnp.int32)],
        compiler_params=_cparams("parallel", "arbitrary"),
        name="dsa_select",
    )(qi, small, kx)


def _dsa_attn_body(sel_s, selv_ref, q2_ref, tbl_ref, o_ref, *stages, tqa, topk):
    heads = DSA_HEADS
    scale = DSA_HEAD_DIM ** -0.5
    lt = LAT_TILES

    def gather(i, buf):
        base = i * topk
        for j in range(topk):
            stages[buf][pl.ds(j * lt, lt), :] = tbl_ref[pl.ds(sel_s[base + j], lt), :]

    def attend(i0, bufs):
        qs = range(len(bufs))
        pk = [jnp.concatenate([stages[b][pl.ds(t, topk, stride=lt), :] for t in range(lt)], 1) for b in bufs]
        kv = [jnp.concatenate([lax.bitcast_convert_type(x << 16, F32),
                               lax.bitcast_convert_type(x & jnp.uint32(0xFFFF0000), F32)], 1).astype(BF16)
              for x in pk]
        rows = [pl.ds(pl.multiple_of((i0 + a) * heads, heads), heads) for a in qs]
        s = [lax.dot_general(q2_ref[rows[a], :], kv[a], (((1,), (1,)), ((), ())), preferred_element_type=F32) * scale
             for a in qs]
        s = [jnp.where(selv_ref[pl.ds(i0 + a, 1), :] >= 0, s[a], -jnp.inf) for a in qs]
        e = [jnp.exp(s[a] - jnp.max(s[a], -1, keepdims=True)) for a in qs]
        p = [(e[a] / jnp.sum(e[a], -1, keepdims=True)).astype(BF16) for a in qs]
        o = [jnp.dot(p[a], kv[a][:, :DSA_KV_RANK], preferred_element_type=F32) for a in qs]
        for a in qs:
            o_ref[rows[a], :] = o[a].astype(o_ref.dtype)

    nq = len(stages) // 2
    first, second = tuple(range(nq)), tuple(range(nq, 2 * nq))
    for a in first:
        gather(a, a)

    def body(pi, carry):
        i0 = 2 * nq * pi
        for a in first:
            gather(i0 + nq + a, second[a])
        attend(i0, first)
        for a in first:
            gather(jnp.minimum(i0 + 2 * nq + a, tqa - 1), a)
        attend(i0 + nq, second)
        return carry

    lax.fori_loop(0, tqa // (2 * nq), body, 0)


def dsa_attend(sel, q2, table, *, batch, seq):
    n, topk = sel.shape
    tqa = min(64, seq)
    qpb = seq // tqa
    heads = DSA_HEADS
    return pl.pallas_call(
        functools.partial(_dsa_attn_body, tqa=tqa, topk=topk),
        grid=(batch, qpb),
        in_specs=[pl.BlockSpec((tqa * topk,), lambda b, i: (b * qpb + i,), memory_space=pltpu.SMEM),
                  pl.BlockSpec((tqa, topk), lambda b, i: (b * qpb + i, 0)),
                  pl.BlockSpec((tqa * heads, LAT_W), lambda b, i: (b * qpb + i, 0)),
                  pl.BlockSpec((seq * LAT_TILES, LANES), lambda b, i: (b, 0), pipeline_mode=pl.Buffered(1))],
        out_specs=pl.BlockSpec((tqa * heads, DSA_KV_RANK), lambda b, i: (b * qpb + i, 0)),
        out_shape=jax.ShapeDtypeStruct((n * heads, DSA_KV_RANK), BF16),
        scratch_shapes=[pltpu.VMEM((topk * LAT_TILES, LANES), jnp.uint32)] * (2 * DSA_QUERIES_PER_ATTEND),
        compiler_params=_cparams("parallel", "arbitrary"),
        name="dsa_attend",
    )((jnp.maximum(sel, 0) * LAT_TILES).reshape(-1), sel, q2, table.reshape(n * LAT_TILES, LANES))


def _head_mm_body(a_ref, w_ref, o_ref):
    o_ref[0] = jnp.dot(a_ref[...].astype(BF16), w_ref[0], preferred_element_type=F32).astype(o_ref.dtype)


def head_mm_out(a, w, *, tm=512):
    n = a.shape[0]
    heads, k, m = w.shape
    tm = min(tm, n)
    return pl.pallas_call(
        _head_mm_body,
        grid=(n // tm, heads),
        in_specs=[pl.BlockSpec((tm, k), lambda i, h: (i, h)), pl.BlockSpec((1, k, m), lambda i, h: (h, 0, 0))],
        out_specs=pl.BlockSpec((1, tm, m), lambda i, h: (h, i, 0)),
        out_shape=jax.ShapeDtypeStruct((heads, n, m), BF16),
        compiler_params=_cparams("parallel", "parallel"),
        name="head_mm_out",
    )(a, w)


def _head_mm_in_body(a_ref, w_ref, o_ref):
    o_ref[...] = jnp.dot(a_ref[0], w_ref[0], preferred_element_type=F32).astype(o_ref.dtype)


def head_mm_in(a, w, *, tm=512):
    heads, n, k = a.shape
    m = w.shape[2]
    tm = min(tm, n)
    return pl.pallas_call(
        _head_mm_in_body,
        grid=(n // tm, heads),
        in_specs=[pl.BlockSpec((1, tm, k), lambda i, h: (h, i, 0)), pl.BlockSpec((1, k, m), lambda i, h: (h, 0, 0))],
        out_specs=pl.BlockSpec((tm, m), lambda i, h: (i, h)),
        out_shape=jax.ShapeDtypeStruct((n, heads * m), BF16),
        compiler_params=_cparams("parallel", "parallel"),
        name="head_mm_in",
    )(a, w)


def dsa_layer(hbf, positions, p, *, batch, seq):
    n = hbf.shape[0]
    heads = DSA_HEADS
    q = mm(hbf, p["w_q"])
    ckv = mm(hbf, p["w_ckv"])
    qidx = mm(hbf, p["w_qidx"])
    small = mm(hbf, p["w_small"])
    qr, table, qi, small_o, kx = rowwise(
        _dsa_prep_fn, [q, ckv, qidx, small, positions.reshape(n, 1)], [],
        [p["kv_norm"], p["idx_ln_g"], p["idx_ln_b"], _dsa_patterns()],
        [(q.shape[1], BF16), (LAT_PACK, jnp.uint32), (qidx.shape[1], BF16), (LANES, F32), (LANES, BF16)],
        seq=seq, tm=128, name="dsa_prep")
    sel = dsa_select(qi, small_o, kx, batch=batch, seq=seq)
    q2 = head_mm_out(qr, p["w_q2"])
    q2 = q2.transpose(1, 0, 2).reshape(n * heads, LAT_W)
    o_lat = dsa_attend(sel, q2, table, batch=batch, seq=seq)
    o_lat = o_lat.reshape(n, heads, DSA_KV_RANK).transpose(1, 0, 2)
    out = head_mm_in(o_lat, p["w_uv"])
    return mm(out, p["w_o"])


def _s5_tables(p):
    hp = lax.Precision.HIGHEST
    lc, hg = S5_CHUNK, S5_GROUP
    lr, li = p["lam_re"], p["lam_im"]
    g, ps = lr.shape
    dt = jnp.exp(p["log_dt"])[:, None]
    mag = jnp.exp(lr * dt)
    ab_re, ab_im = mag * jnp.cos(li * dt), mag * jnp.sin(li * dt)
    den = lr * lr + li * li
    nr, ni = ab_re - 1.0, ab_im
    coef_re = (nr * lr + ni * li) / den
    coef_im = (ni * lr - nr * li) / den
    bb_re = coef_re[..., None] * p["b_re"] - coef_im[..., None] * p["b_im"]
    bb_im = coef_re[..., None] * p["b_im"] + coef_im[..., None] * p["b_re"]
    tau = jnp.arange(lc + 1, dtype=F32)[:, None, None]
    pmag = jnp.exp(lr * dt * tau)
    pr, pi = pmag * jnp.cos(li * dt * tau), pmag * jnp.sin(li * dt * tau)
    cr, ci = p["c_re"][None], p["c_im"][None]
    car = cr * pr[:, :, None, :] - ci * pi[:, :, None, :]
    cai = cr * pi[:, :, None, :] + ci * pr[:, :, None, :]
    kern = (jnp.einsum('tghp,gpk->tghk', car[:lc], bb_re, precision=hp)
            - jnp.einsum('tghp,gpk->tghk', cai[:lc], bb_im, precision=hp))
    kz = jnp.concatenate([kern, jnp.zeros((1,) + kern.shape[1:], F32)], 0)
    s_i = np.arange(lc)[:, None]
    t_i = np.arange(lc)[None, :]
    m = kz[np.where(t_i >= s_i, t_i - s_i, lc)]
    m = m.transpose(2, 0, 4, 1, 3).reshape(g, lc * hg, lc * hg)
    skip = jnp.tile(p["d"].reshape(g, 1, hg), (1, lc, 1)).reshape(g, 1, lc * hg)
    m = m + skip * jnp.eye(lc * hg, dtype=F32)[None]
    prs, pis = pr[lc - 1 - np.arange(lc)], pi[lc - 1 - np.arange(lc)]
    wre = prs[..., None] * bb_re[None] - pis[..., None] * bb_im[None]
    wim = prs[..., None] * bb_im[None] + pis[..., None] * bb_re[None]
    wre = wre.transpose(1, 0, 3, 2).reshape(g, lc * hg, ps)
    wim = wim.transpose(1, 0, 3, 2).reshape(g, lc * hg, ps)
    vre = car[1:].transpose(1, 3, 0, 2).reshape(g, ps, lc * hg)
    vim = (-cai[1:]).transpose(1, 3, 0, 2).reshape(g, ps, lc * hg)
    gu = S5_UNIT
    eye_u = jnp.eye(gu, dtype=F32)

    def unit(z, row_t, col_t):
        r, c = z.shape[1:]
        zz = z.reshape(g // gu, gu, r, 1, c) * eye_u[None, :, None, :, None]
        if row_t:
            zz = zz.reshape(g // gu, gu, lc, hg, gu, c).transpose(0, 2, 1, 3, 4, 5).reshape(g // gu, gu * r, gu, c)
        else:
            zz = zz.reshape(g // gu, gu * r, gu, c)
        if col_t:
            zz = zz.reshape(g // gu, gu * r, gu, lc, hg).transpose(0, 1, 3, 2, 4)
        return zz.reshape(g // gu, gu * r, gu * c).astype(BF16)

    are = pr[lc].reshape(g // gu, 1, gu * ps)
    aim = pi[lc].reshape(g // gu, 1, gu * ps)
    return (unit(m, True, True), unit(wre, True, False), unit(wim, True, False),
            unit(vre, False, True), unit(vim, False, True), are, aim)


def _gelu_tanh(y):
    return 0.5 * y * (1.0 + jnp.tanh(math.sqrt(2.0 / math.pi) * (y + 0.044715 * (y * y * y))))


def _s5_body(u_ref, m_ref, wre_ref, wim_ref, vre_ref, vim_ref, are_ref, aim_ref, z_ref, xre, xim, sre, sim, *, pairs):
    nc = u_ref.shape[1]
    for q in range(pairs):
        u = u_ref[q]
        xre[q] = jnp.dot(u, wre_ref[q], preferred_element_type=F32)
        xim[q] = jnp.dot(u, wim_ref[q], preferred_element_type=F32)
    ar = [are_ref[q] for q in range(pairs)]
    ai = [aim_ref[q] for q in range(pairs)]

    def step(c, carry):
        new = []
        row = pl.ds(c, 1)
        for q in range(pairs):
            re, im = carry[2 * q], carry[2 * q + 1]
            sre[q, row, :] = re
            sim[q, row, :] = im
            new.append(ar[q] * re - ai[q] * im + xre[q, row, :])
            new.append(ar[q] * im + ai[q] * re + xim[q, row, :])
        return tuple(new)

    zero = jnp.zeros((1, are_ref.shape[-1]), F32)
    lax.fori_loop(0, nc, step, (zero,) * (2 * pairs), unroll=8)
    rb = min(256, nc)
    for q in range(pairs):
        def out_block(i, carry, q=q):
            rs = pl.ds(pl.multiple_of(i * rb, rb), rb)
            y = (jnp.dot(u_ref[q, rs, :], m_ref[q], preferred_element_type=F32)
                 + jnp.dot(sre[q, rs, :].astype(BF16), vre_ref[q], preferred_element_type=F32)
                 + jnp.dot(sim[q, rs, :].astype(BF16), vim_ref[q], preferred_element_type=F32))
            z_ref[q, rs, :] = _gelu_tanh(y).astype(z_ref.dtype)
            return carry

        lax.fori_loop(0, nc // rb, out_block, 0)


def s5_layer(hbf, p, *, batch, seq):
    n, D = hbf.shape
    lc, hg = S5_CHUNK, S5_GROUP
    gu = S5_UNIT
    g2 = D // hg // gu
    pw = gu * lc * hg
    tabs = _s5_tables(p)
    nc = seq // lc
    u = hbf.reshape(n // lc, lc, g2, gu * hg).transpose(2, 0, 1, 3).reshape(g2, n // lc, pw)
    pairs = 1
    sw = gu * S5_STATE
    wspec = lambda r, c: pl.BlockSpec((pairs, r, c), lambda i, b: (i, 0, 0), pipeline_mode=pl.Buffered(1))
    z = pl.pallas_call(
        functools.partial(_s5_body, pairs=pairs),
        grid=(g2 // pairs, batch),
        in_specs=[pl.BlockSpec((pairs, nc, pw), lambda i, b: (i, b, 0)),
                  wspec(pw, pw), wspec(pw, sw), wspec(pw, sw), wspec(sw, pw), wspec(sw, pw),
                  wspec(1, sw), wspec(1, sw)],
        out_specs=pl.BlockSpec((pairs, nc, pw), lambda i, b: (i, b, 0)),
        out_shape=jax.ShapeDtypeStruct((g2, n // lc, pw), BF16),
        scratch_shapes=[pltpu.VMEM((pairs, nc, sw), F32)] * 4,
        compiler_params=_cparams("parallel", "parallel"),
        name="s5_scan",
    )(u, *tabs)
    z = z.reshape(g2, n // lc, lc, gu * hg).transpose(1, 2, 0, 3).reshape(n, D)
    zz = mm(z, p["w_glu"])
    return rowwise(lambda a: (a[:, :D] * jax.nn.sigmoid(a[:, D:]),), [zz], [], [], [(D, F32)], seq=seq, tm=256,
                   name="s5_glu")[0]


def _modulate(x, sc, sh, dtype, *, seq):
    d = x.shape[1]
    return rowwise(lambda xv, s, h: (xv * (1.0 + s) + h,), [x], [sc, sh], [], [(d, dtype)], seq=seq, tm=256,
                   name="modulate")[0]


def kernel(x, c, positions, ada_w, ada_b, ln_g, ln_b, router_w, router_bias, moe_w_gate, moe_w_up, moe_w_down,
           rwkv_mu, rwkv_w_rkv, rwkv_w_o, rwkv_w0, rwkv_w1, rwkv_w2, rwkv_a0, rwkv_a1, rwkv_a2, rwkv_g1, rwkv_g2,
           rwkv_k_k, rwkv_k_a, rwkv_r_k, rwkv_gn_g, rwkv_gn_b, rwkv_v0, rwkv_v1, rwkv_v2, dsa_w_in, dsa_kv_norm,
           dsa_w_uk, dsa_w_uv, dsa_idx_ln_g, dsa_idx_ln_b, dsa_w_o, s5_lam_re, s5_lam_im, s5_log_dt, s5_b_re,
           s5_b_im, s5_c_re, s5_c_im, s5_d, s5_w_glu):
    batch, seq, d = x.shape
    depth = ada_w.shape[0]
    n = batch * seq
    xf = x.reshape(n, d)
    mod = ada_mod(c, ada_w, ada_b)[:, :batch].reshape(depth, batch, 1, 6, d)
    v_first = None
    for i in range(depth):
        kind, j = i % N_MIXERS, i // N_MIXERS
        sh1, sc1, g1, sh2, sc2, g2 = (mod[i, :, :, m] for m in range(6))
        if kind == 0:
            p = dict(mu=rwkv_mu[j], w_r=rwkv_w_rkv[j, 0].astype(BF16), w_k=rwkv_w_rkv[j, 1].astype(BF16),
                     w_v=rwkv_w_rkv[j, 2].astype(BF16), w_o=rwkv_w_o[j].astype(BF16),
                     w0=rwkv_w0[j], w1=_wpad(rwkv_w1[j]), w2=_kpad(rwkv_w2[j]),
                     a0=rwkv_a0[j], a1=_wpad(rwkv_a1[j]), a2=_kpad(rwkv_a2[j]),
                     g1=_wpad(rwkv_g1[j]), g2=_kpad(rwkv_g2[j]),
                     k_k=rwkv_k_k[j], k_a=rwkv_k_a[j], r_k=rwkv_r_k[j], gn_g=rwkv_gn_g[j], gn_b=rwkv_gn_b[j],
                     v0=None)
            if j > 0:
                p.update(v0=rwkv_v0[j - 1], v1=_wpad(rwkv_v1[j - 1]), v2=_kpad(rwkv_v2[j - 1]))
            y, v_first = rwkv_layer(xf, sc1, sh1, v_first, p, batch=batch, seq=seq)
        elif kind == 1:
            w_in = dsa_w_in[j]
            hq = DSA_HEADS * DSA_HEAD_DIM
            o1 = hq + DSA_KV_RANK
            o2 = o1 + DSA_ROPE_DIM
            o3 = o2 + IDX_HEADS * IDX_DIM
            o4 = o3 + IDX_DIM
            gap = jnp.zeros((d, SM_R0 - SM_W0 - IDX_HEADS), F32)
            w_small = jnp.concatenate([w_in[:, o3:o4], w_in[:, o4:], gap, w_in[:, o1:o2]], axis=1)
            rope_pass = jnp.zeros((DSA_ROPE_DIM, LAT_W), F32).at[
                jnp.arange(DSA_ROPE_DIM), DSA_KV_RANK + SM_R0 + jnp.arange(DSA_ROPE_DIM)].set(1.0)
            w_uk_pad = jnp.pad(dsa_w_uk[j], ((0, 0), (0, 0), (0, LAT_W - DSA_KV_RANK)))
            w_q2 = jnp.concatenate([jnp.broadcast_to(rope_pass, (DSA_HEADS,) + rope_pass.shape), w_uk_pad], axis=1)
            lane_pad = lambda v: jnp.pad(v, (0, LANES - v.shape[0])).reshape(1, LANES)
            p = dict(w_q=w_in[:, :hq].astype(BF16), w_ckv=w_in[:, hq:o1].astype(BF16),
                     w_qidx=w_in[:, o2:o3].astype(BF16), w_small=w_small.astype(BF16),
                     kv_norm=dsa_kv_norm[j].reshape(1, -1), w_q2=w_q2.astype(BF16), w_uv=dsa_w_uv[j].astype(BF16),
                     idx_ln_g=lane_pad(dsa_idx_ln_g[j]), idx_ln_b=lane_pad(dsa_idx_ln_b[j]),
                     w_o=dsa_w_o[j].astype(BF16))
            hbf = _modulate(xf, sc1, sh1, BF16, seq=seq)
            y = dsa_layer(hbf, positions, p, batch=batch, seq=seq)
        else:
            p = dict(lam_re=s5_lam_re[j], lam_im=s5_lam_im[j], log_dt=s5_log_dt[j], b_re=s5_b_re[j],
                     b_im=s5_b_im[j], c_re=s5_c_re[j], c_im=s5_c_im[j], d=s5_d[j], w_glu=s5_w_glu[j].astype(BF16))
            hbf = _modulate(xf, sc1, sh1, BF16, seq=seq)
            y = s5_layer(hbf, p, batch=batch, seq=seq)
        xf, hbf = post_norm_mix(xf, y, g1, sc2, sh2, ln_g[i, 0:1], ln_b[i, 0:1], seq=seq)
        y0, y1, gates = moe_ffn(xf, hbf, sc2, sh2, router_w, router_bias, moe_w_gate[i].astype(BF16),
                                moe_w_up[i].astype(BF16), moe_w_down[i].astype(BF16), seq=seq)
        xf = post_norm_moe(xf, y0, y1, gates, g2, ln_g[i, 1:2], ln_b[i, 1:2], seq=seq)
    return xf.reshape(batch, seq, d)
```

```python
import functools
import itertools
import math

import numpy as np
import jax
import jax.numpy as jnp
from jax import lax
from jax.experimental import pallas as pl
from jax.experimental.pallas import tpu as pltpu

F32, BF16 = jnp.float32, jnp.bfloat16

DEPTH = 4
N_MIXERS = 3
RWKV_HEAD = 64
RWKV_GN_EPS = 64e-5
DSA_HEADS = 16
DSA_HEAD_DIM = 128
DSA_ROPE_DIM = 32
DSA_NOPE_DIM = 96
DSA_V_DIM = 128
DSA_KV_RANK = 512
IDX_HEADS = 16
IDX_DIM = 64
IDX_ROPE_DIM = 16
TOPK_MAX = 256
Q_BLOCK = 128
S5_GROUP = 16
S5_STATE = 64
S5_CHUNK = 16
S5_UNIT = 8
N_EXPERTS = 32
N_GROUPS = 4
EXPERTS_PER_GROUP = 8
TOP_K = 2
ROPE_THETA = 500000.0
LN_EPS = 1e-5
DEEPNORM_ALPHA = (2 * DEPTH) ** 0.25

LANES = 128
VMEM_LIMIT = 48 * 1024 * 1024
MOE_ROWS = 256
WKV_CHUNK = 64
WKV_HEADS_PER_STEP = 4
WKV_GROUPS_PER_STEP = 4
WKV_ROWS_PER_STEP = 512


def _cparams(*sem):
    return pltpu.CompilerParams(dimension_semantics=sem, vmem_limit_bytes=VMEM_LIMIT)


def _pick(n, cands):
    for c in cands:
        if n % c == 0:
            return c
    return n


def _mm_body(a_ref, w_ref, o_ref, *, act):
    acc = jnp.dot(a_ref[...].astype(BF16), w_ref[...], preferred_element_type=F32)
    if act == "tanh":
        acc = jnp.tanh(acc)
    elif act == "sigmoid":
        acc = jax.nn.sigmoid(acc)
    o_ref[...] = acc.astype(o_ref.dtype)


def mm(a, w, *, out_dtype=F32, act=None, tm=512):
    m, k = a.shape
    n = w.shape[1]
    tm = _pick(m, (tm, 256, 128, 64, 32, 16, 8))
    tn = _pick(n, (512, 256, 128))
    return pl.pallas_call(
        functools.partial(_mm_body, act=act),
        grid=(m // tm, n // tn),
        in_specs=[pl.BlockSpec((tm, k), lambda i, j: (i, 0)), pl.BlockSpec((k, tn), lambda i, j: (0, j))],
        out_specs=pl.BlockSpec((tm, tn), lambda i, j: (i, j)),
        out_shape=jax.ShapeDtypeStruct((m, n), out_dtype),
        compiler_params=_cparams("parallel", "parallel"),
        name="mm",
    )(a, w)


def _wpad(w):
    n = w.shape[1]
    npad = -(-n // LANES) * LANES
    w = w.astype(BF16)
    return w if npad == n else jnp.pad(w, ((0, 0), (0, npad - n)))


def _kpad(w):
    k = w.shape[0]
    kpad = -(-k // LANES) * LANES
    w = w.astype(BF16)
    return w if kpad == k else jnp.pad(w, ((0, kpad - k), (0, 0)))


def rowwise(fn, rows, perbatch, consts, outs, *, seq, tm, name):
    n = rows[0].shape[0]
    tm = min(tm, seq)
    tpb = seq // tm
    nr, nb, nc = len(rows), len(perbatch), len(consts)

    def body(*refs):
        vals = [r[...] for r in refs[:nr]]
        vals += [r[0] for r in refs[nr:nr + nb]]
        vals += [r[...] for r in refs[nr + nb:nr + nb + nc]]
        res = fn(*vals)
        for o, v in zip(refs[nr + nb + nc:], res):
            o[...] = v.astype(o.dtype)

    in_specs = [pl.BlockSpec((tm, r.shape[1]), lambda i: (i, 0)) for r in rows]
    in_specs += [pl.BlockSpec((1, 1, p.shape[-1]), lambda i: (i // tpb, 0, 0)) for p in perbatch]
    in_specs += [pl.BlockSpec(c.shape, lambda i: (0, 0)) for c in consts]
    res = pl.pallas_call(
        body,
        grid=(n // tm,),
        in_specs=in_specs,
        out_specs=[pl.BlockSpec((tm, w), lambda i: (i, 0)) for w, _ in outs],
        out_shape=[jax.ShapeDtypeStruct((n, w), dt) for w, dt in outs],
        compiler_params=_cparams("parallel"),
        name=name,
    )(*rows, *perbatch, *consts)
    return res


def _layer_norm(z, g, b):
    mu = jnp.mean(z, -1, keepdims=True)
    zc = z - mu
    var = jnp.mean(zc * zc, -1, keepdims=True)
    return zc * lax.rsqrt(var + LN_EPS) * g + b


def _split_dot(x, m):
    xh = x.astype(BF16)
    xl = (x - xh.astype(F32)).astype(BF16)
    return jnp.dot(xh, m, preferred_element_type=F32) + jnp.dot(xl, m, preferred_element_type=F32)


def _seg_sum(x, ones_bd):
    parts = [_split_dot(x[:, j:j + LANES], ones_bd) for j in range(0, x.shape[1], LANES)]
    return jnp.concatenate(parts, -1)


def _head_ones():
    i = np.arange(LANES)
    return jnp.asarray((i[:, None] // RWKV_HEAD) == (i[None, :] // RWKV_HEAD), BF16)


def _ada_body(c_ref, w_ref, b_ref, o_ref):
    acc = jnp.dot(c_ref[...], w_ref[0], preferred_element_type=F32, precision=lax.Precision.HIGHEST)
    o_ref[0] = acc + b_ref[0]


def ada_mod(c, ada_w, ada_b):
    depth, d, n = ada_w.shape
    b = c.shape[0]
    cp = jnp.pad(c, ((0, 8 - b), (0, 0)))
    tn = 1024
    return pl.pallas_call(
        _ada_body,
        grid=(depth, n // tn),
        in_specs=[pl.BlockSpec((8, d), lambda i, j: (0, 0)),
                  pl.BlockSpec((1, d, tn), lambda i, j: (i, 0, j)),
                  pl.BlockSpec((1, 1, tn), lambda i, j: (i, 0, j))],
        out_specs=pl.BlockSpec((1, 8, tn), lambda i, j: (i, 0, j)),
        out_shape=jax.ShapeDtypeStruct((depth, 8, n), F32),
        compiler_params=_cparams("parallel", "parallel"),
        name="ada_mod",
    )(cp, ada_w, ada_b.reshape(depth, 1, n))


def post_norm_mix(x, y, gate, sc, sh, lng, lnb, *, seq):
    d = x.shape[1]

    def fn(xv, yv, g, s, h, lg, lb):
        xn = _layer_norm(DEEPNORM_ALPHA * xv + (1.0 + g) * yv, lg, lb)
        return xn, xn * (1.0 + s) + h

    return rowwise(fn, [x, y], [gate, sc, sh], [lng, lnb], [(d, F32), (d, BF16)], seq=seq, tm=256, name="post_norm_mix")


def post_norm_moe(x, y0, y1, gates, gate, lng, lnb, *, seq):
    d = x.shape[1]

    def fn(xv, a, b, gt, g, lg, lb):
        y = a * gt[:, 0:1] + b * gt[:, 1:2]
        return (_layer_norm(DEEPNORM_ALPHA * xv + (1.0 + g) * y, lg, lb),)

    return rowwise(fn, [x, y0, y1, gates], [gate], [lng, lnb], [(d, F32)], seq=seq, tm=256, name="post_norm_moe")[0]


def _router_fn(xv, s, h, rw, rb):
    hf = xv * (1.0 + s) + h
    logits = jnp.dot(hf, rw, preferred_element_type=F32, precision=lax.Precision.HIGHEST)
    scores = jax.nn.sigmoid(logits)
    biased = scores + rb
    lane = lax.broadcasted_iota(jnp.int32, biased.shape, 1)
    neg = jnp.float32(-jnp.inf)
    big = jnp.int32(1 << 20)
    best = bi1 = bi2 = None
    for g in range(N_GROUPS):
        ing = (lane >= g * EXPERTS_PER_GROUP) & (lane < (g + 1) * EXPERTS_PER_GROUP)
        v = jnp.where(ing, biased, neg)
        m1 = jnp.max(v, -1, keepdims=True)
        i1 = jnp.min(jnp.where(v == m1, lane, big), -1, keepdims=True)
        v2 = jnp.where(lane == i1, neg, v)
        m2 = jnp.max(v2, -1, keepdims=True)
        i2 = jnp.min(jnp.where(v2 == m2, lane, big), -1, keepdims=True)
        gs = m1 + m2
        if g == 0:
            best, bi1, bi2 = gs, i1, i2
        else:
            better = gs > best
            best = jnp.where(better, gs, best)
            bi1 = jnp.where(better, i1, bi1)
            bi2 = jnp.where(better, i2, bi2)
    g1 = jnp.sum(jnp.where(lane == bi1, scores, 0.0), -1, keepdims=True)
    g2 = jnp.sum(jnp.where(lane == bi2, scores, 0.0), -1, keepdims=True)
    tot = g1 + g2
    two = lax.broadcasted_iota(jnp.int32, (biased.shape[0], TOP_K), 1)
    return jnp.where(two == 0, bi1, bi2), jnp.where(two == 0, g1 / tot, g2 / tot)


def _expert_body(be_ref, nb_ref, x_ref, wg_ref, wu_ref, wd_ref, o_ref):
    del be_ref

    @pl.when(pl.program_id(0) < nb_ref[0])
    def _():
        x = x_ref[...]
        g = jnp.dot(x, wg_ref[0], preferred_element_type=F32)
        u = jnp.dot(x, wu_ref[0], preferred_element_type=F32)
        hid = (g * jax.nn.sigmoid(g) * u).astype(BF16)
        o_ref[...] = jnp.dot(hid, wd_ref[0], preferred_element_type=F32).astype(o_ref.dtype)

    @pl.when(pl.program_id(0) >= nb_ref[0])
    def _():
        o_ref[...] = jnp.zeros_like(o_ref)


def moe_ffn(x, hbf, sc, sh, router_w, router_bias, wg, wu, wd, *, seq):
    n, d = x.shape
    e = router_w.shape[1]
    idx, gates = rowwise(_router_fn, [x], [sc, sh], [router_w, router_bias.reshape(1, e)],
                         [(TOP_K, jnp.int32), (TOP_K, F32)], seq=seq, tm=256, name="router")
    bm = MOE_ROWS
    n_slots = n * TOP_K
    flat_e = idx.reshape(-1)
    onehot = (flat_e[:, None] == jnp.arange(e, dtype=jnp.int32)[None, :]).astype(jnp.int32)
    csum = jnp.cumsum(onehot, axis=0)
    pos = jnp.sum(onehot * csum, axis=1) - 1
    counts = csum[-1]
    padded = (counts + bm - 1) // bm * bm
    pend = jnp.cumsum(padded)
    pstart = pend - padded
    dest = pstart[flat_e] + pos
    n_blocks = n_slots // bm + e
    n_pad = n_blocks * bm
    slot_tok = jnp.zeros((n_pad,), jnp.int32).at[dest].set(jnp.arange(n_slots, dtype=jnp.int32) // TOP_K)
    block_start = jnp.arange(n_blocks, dtype=jnp.int32) * bm
    block_expert = jnp.minimum(jnp.sum((pend[None, :] <= block_start[:, None]).astype(jnp.int32), axis=1), e - 1)
    used_blocks = (pend[-1] // bm).astype(jnp.int32).reshape(1)
    xs = jnp.take(hbf, slot_tok, axis=0)
    de = wg.shape[2]
    yb = pl.pallas_call(
        _expert_body,
        grid_spec=pltpu.PrefetchScalarGridSpec(
            num_scalar_prefetch=2,
            grid=(n_blocks,),
            in_specs=[pl.BlockSpec((bm, d), lambda i, be, nb: (i, 0)),
                      pl.BlockSpec((1, d, de), lambda i, be, nb: (be[i], 0, 0)),
                      pl.BlockSpec((1, d, de), lambda i, be, nb: (be[i], 0, 0)),
                      pl.BlockSpec((1, de, d), lambda i, be, nb: (be[i], 0, 0))],
            out_specs=pl.BlockSpec((bm, d), lambda i, be, nb: (i, 0)),
        ),
        out_shape=jax.ShapeDtypeStruct((n_pad, d), BF16),
        compiler_params=_cparams("arbitrary"),
        name="moe_experts",
    )(block_expert, used_blocks, xs, wg, wu, wd)
    dest2 = dest.reshape(n, TOP_K)
    return jnp.take(yb, dest2[:, 0], axis=0), jnp.take(yb, dest2[:, 1], axis=0), gates


def _rwkv_prep_body(x_ref, xp_ref, sc_ref, sh_ref, mu_ref, *outs, tpb):
    sc = 1.0 + sc_ref[0]
    sh = sh_ref[0]
    h = x_ref[...] * sc + sh
    prev_row = xp_ref[7:8, :] * sc + sh
    prev_row = jnp.where(pl.program_id(0) % tpb == 0, 0.0, prev_row)
    rowid = lax.broadcasted_iota(jnp.int32, h.shape, 0)
    hprev = jnp.where(rowid == 0, prev_row, pltpu.roll(h, 1, 0))
    dx = hprev - h
    for m, o in enumerate(outs):
        o[...] = (h + dx * mu_ref[m:m + 1, :]).astype(o.dtype)


def rwkv_prep(x, sc, sh, mu, *, seq):
    n, d = x.shape
    tm = min(256, seq)
    tpb = seq // tm
    return pl.pallas_call(
        functools.partial(_rwkv_prep_body, tpb=tpb),
        grid=(n // tm,),
        in_specs=[pl.BlockSpec((tm, d), lambda i: (i, 0)),
                  pl.BlockSpec((8, d), lambda i: (jnp.maximum(i * (tm // 8) - 1, 0), 0)),
                  pl.BlockSpec((1, 1, d), lambda i: (i // tpb, 0, 0)),
                  pl.BlockSpec((1, 1, d), lambda i: (i // tpb, 0, 0)),
                  pl.BlockSpec((6, d), lambda i: (0, 0))],
        out_specs=[pl.BlockSpec((tm, d), lambda i: (i, 0))] * 6,
        out_shape=[jax.ShapeDtypeStruct((n, d), BF16)] * 6,
        compiler_params=_cparams("parallel"),
        name="rwkv_prep",
    )(x, x, sc, sh, mu)


def _softplus(z):
    return jnp.maximum(z, 0.0) + jnp.log(1.0 + jnp.exp(-jnp.abs(z)))


def _rwkv_gates_fn(k0, lw, al, *rest, has_vres):
    if has_vres:
        v, vfirst, vl, prm, ones_bd = rest
    else:
        prm, ones_bd = rest
    w0, a0, k_k, k_a = prm[0:1], prm[1:2], prm[2:3], prm[3:4]
    log_w = -_softplus(-(w0 + lw)) - 0.5
    logdecay = -jnp.exp(log_w)
    a = jax.nn.sigmoid(a0 + al)
    kk = k0 * k_k
    nrm = jnp.maximum(jnp.sqrt(_seg_sum(kk * kk, ones_bd)), 1e-12)
    kk = kk / nrm
    k = k0 * (1.0 + (a - 1.0) * k_a)
    res = [logdecay, k, kk, kk * a]
    if has_vres:
        res.append(v + (vfirst - v) * jax.nn.sigmoid(prm[4:5] + vl))
    return res


def _wkv_body(r_ref, lw_ref, k_ref, v_ref, kk_ref, b_ref, y_ref, *s_refs, chunk, heads):
    hd = RWKV_HEAD
    rows = r_ref.shape[0]
    width = heads * hd
    hr = heads * chunk
    nt = (((1,), (1,)), ((), ()))

    @pl.when(pl.program_id(2) == 0)
    def _():
        for s_ref in s_refs:
            s_ref[...] = jnp.zeros_like(s_ref)

    ti = lax.broadcasted_iota(jnp.int32, (chunk, chunk), 0)
    si = lax.broadcasted_iota(jnp.int32, (chunk, chunk), 1)
    tri = (ti >= si).astype(BF16)
    ri = lax.broadcasted_iota(jnp.int32, (hr, width), 0)
    ci = lax.broadcasted_iota(jnp.int32, (hr, width), 1)
    head_mask = (ri // chunk == ci // hd).astype(F32)
    rr = lax.broadcasted_iota(jnp.int32, (hr, hr), 0)
    cc = lax.broadcasted_iota(jnp.int32, (hr, hr), 1)
    strict = rr % chunk > cc % chunk
    eye = (rr == cc).astype(F32)
    tc = lax.broadcasted_iota(jnp.int32, (chunk, 2 * hr), 0)
    sc = lax.broadcasted_iota(jnp.int32, (chunk, 2 * hr), 1) % chunk
    incl_c = tc >= sc
    strict_c = (tc > sc)[:, :hr]

    def blocked(x):
        return (jnp.concatenate([x] * heads, 0) * head_mask).astype(BF16)

    def step(c, carry):
        sl = pl.ds(pl.multiple_of(c * chunk, chunk), chunk)
        gens = [group_step(sl, slice(grp * width, (grp + 1) * width), s_ref) for grp, s_ref in enumerate(s_refs)]
        for _ in itertools.zip_longest(*gens):
            pass
        return carry

    def group_step(sl, ln, s_ref):
        lw = lw_ref[sl, ln]
        cs = _split_dot_left(tri, lw)
        ctot = cs[chunk - 1:chunk, :]
        g_inv = jnp.exp(-cs)
        g_end = jnp.exp(ctot - cs)
        kk = kk_ref[sl, ln]
        kv = k_ref[sl, ln]
        bv = b_ref[sl, ln]
        ag = -kk * jnp.exp(cs - lw)
        bg = blocked(bv * g_inv)
        bk = jnp.concatenate([bg, blocked(kv * g_inv)], 0)
        ends = jnp.concatenate([blocked(bv * g_end), blocked(kv * g_end)], 0)
        vb = blocked(v_ref[sl, ln])
        s_old = s_ref[...]
        yield
        a_ab = jnp.where(strict, lax.dot_general(blocked(ag), bg, nt, preferred_element_type=F32), 0.0)
        ar = jnp.concatenate([ag, r_ref[sl, ln] * jnp.exp(cs)], 0).astype(BF16)
        yield
        m = lax.dot_general(ar, bk, nt, preferred_element_type=F32)
        a_ak = jnp.where(strict_c, m[:chunk, hr:], 0.0)
        a_r = jnp.where(incl_c, m[chunk:], 0.0)
        ars = lax.dot_general(ar, s_old.astype(BF16), nt, preferred_element_type=F32)
        yield
        rhs = blocked(ars[:chunk] + jnp.dot(a_ak.astype(BF16), vb, preferred_element_type=F32))
        inv = eye + a_ab
        apow = a_ab.astype(BF16)
        for _ in range(int(math.log2(chunk)) - 1):
            yield
            apow = jnp.dot(apow, apow, preferred_element_type=F32).astype(BF16)
            inv = inv + jnp.dot(inv.astype(BF16), apow, preferred_element_type=F32)
        yield
        u = jnp.dot(inv.astype(BF16), rhs, preferred_element_type=F32)
        uv = jnp.concatenate([u.astype(BF16), vb], 0)
        yield
        y_ref[sl, ln] = ars[chunk:] + jnp.dot(a_r.astype(BF16), uv, preferred_element_type=F32)
        s_ref[...] = s_old * jnp.exp(ctot) + lax.dot_general(
            uv, ends, (((0,), (0,)), ((), ())), preferred_element_type=F32)

    lax.fori_loop(0, rows // chunk, step, 0)


def _split_dot_left(m, x):
    xh = x.astype(BF16)
    xl = (x - xh.astype(F32)).astype(BF16)
    return jnp.dot(m, xh, preferred_element_type=F32) + jnp.dot(m, xl, preferred_element_type=F32)


def wkv7(r, logdecay, k, v, kk, b, *, batch, seq):
    n, d = r.shape
    chunk = min(WKV_CHUNK, seq)
    rows = min(WKV_ROWS_PER_STEP, seq)
    heads = WKV_HEADS_PER_STEP
    groups = WKV_GROUPS_PER_STEP
    width = heads * RWKV_HEAD
    spb = seq // rows
    spec = pl.BlockSpec((rows, groups * width), lambda bi, hi, ti: (bi * spb + ti, hi))
    return pl.pallas_call(
        functools.partial(_wkv_body, chunk=chunk, heads=heads),
        grid=(batch, d // (groups * width), spb),
        in_specs=[spec] * 6,
        out_specs=spec,
        out_shape=jax.ShapeDtypeStruct((n, d), F32),
        scratch_shapes=[pltpu.VMEM((width, width), F32)] * groups,
        compiler_params=_cparams("parallel", "parallel", "arbitrary"),
        name="wkv7",
    )(r, logdecay, k, v, kk, b)


def _rwkv_out_fn(y, r, k, v, g, prm, ones_bd):
    r_k, gn_g, gn_b = prm[0:1], prm[1:2], prm[2:3]
    inv_n = 1.0 / RWKV_HEAD
    m_y = _seg_sum(y, ones_bd) * inv_n
    yc = y - m_y
    v_y = _seg_sum(yc * yc, ones_bd) * inv_n
    yn = yc * lax.rsqrt(v_y + RWKV_GN_EPS) * gn_g + gn_b
    bonus = _seg_sum(r * k * r_k, ones_bd) * v
    return ((yn + bonus) * g,)


def rwkv_layer(x, sc, sh, v_first, p, *, batch, seq):
    n, d = x.shape
    xr, xw, xk, xv, xa, xg = rwkv_prep(x, sc, sh, p["mu"], seq=seq)
    r = mm(xr, p["w_r"])
    k0 = mm(xk, p["w_k"])
    v = mm(xv, p["w_v"])
    lw = mm(mm(xw, p["w1"], out_dtype=BF16, act="tanh"), p["w2"])
    al = mm(mm(xa, p["a1"], out_dtype=BF16), p["a2"])
    g = mm(mm(xg, p["g1"], out_dtype=BF16, act="sigmoid"), p["g2"])
    ones_bd = _head_ones()
    has_vres = p["v0"] is not None
    outs = [(d, F32)] * (5 if has_vres else 4)
    if has_vres:
        vl = mm(mm(xv, p["v1"], out_dtype=BF16), p["v2"])
        prm = jnp.stack([p["w0"], p["a0"], p["k_k"], p["k_a"], p["v0"]])
        rows = [k0, lw, al, v, v_first, vl]
    else:
        prm = jnp.stack([p["w0"], p["a0"], p["k_k"], p["k_a"]])
        rows = [k0, lw, al]
    res = rowwise(functools.partial(_rwkv_gates_fn, has_vres=has_vres), rows, [], [prm, ones_bd], outs,
                  seq=seq, tm=128, name="rwkv_gates")
    logdecay, k, kk, b = res[:4]
    if has_vres:
        v = res[4]
    else:
        v_first = v
    y = wkv7(r, logdecay, k, v, kk, b, batch=batch, seq=seq)
    prm2 = jnp.stack([p["r_k"], p["gn_g"], p["gn_b"]])
    yg = rowwise(_rwkv_out_fn, [y, r, k, v, g], [], [prm2, ones_bd], [(d, BF16)], seq=seq, tm=128, name="rwkv_out")[0]
    return mm(yg, p["w_o"]), v_first


SM_W0 = IDX_DIM
SM_R0 = LANES - DSA_ROPE_DIM
LAT_W = 768
LAT_PACK = LAT_W // 2
LAT_TILES = LAT_PACK // LANES
DSA_QUERIES_PER_ATTEND = 8


def _dsa_patterns():
    fq = ROPE_THETA ** (-np.arange(DSA_ROPE_DIM // 2, dtype=np.float32) * np.float32(2.0 / DSA_ROPE_DIM))
    fi = ROPE_THETA ** (-np.arange(IDX_ROPE_DIM // 2, dtype=np.float32) * np.float32(2.0 / IDX_ROPE_DIM))
    hq, hi = DSA_ROPE_DIM // 2, IDX_ROPE_DIM // 2
    pat = np.zeros((16, LANES), np.float32)
    lane = np.arange(LANES)
    pat[0, :hq] = fq; pat[0, hq:2 * hq] = fq
    pat[1, :hq] = -1.0; pat[2, hq:2 * hq] = 1.0
    l64 = lane % IDX_DIM
    pat[3] = np.where(l64 < hi, fi[np.minimum(l64, hi - 1)], np.where(l64 < 2 * hi, fi[np.clip(l64 - hi, 0, hi - 1)], 0.0))
    pat[4] = np.where(l64 < hi, -1.0, 0.0); pat[5] = np.where((l64 >= hi) & (l64 < 2 * hi), 1.0, 0.0)
    pat[6, :hi] = fi; pat[6, hi:2 * hi] = fi
    pat[6, SM_R0:SM_R0 + hq] = fq; pat[6, SM_R0 + hq:] = fq
    pat[7, :hi] = -1.0; pat[8, hi:2 * hi] = 1.0
    pat[9, SM_R0:SM_R0 + hq] = -1.0; pat[10, SM_R0 + hq:] = 1.0
    pat[11, :IDX_DIM] = 1.0
    pat[12, SM_W0:SM_W0 + IDX_HEADS] = IDX_HEADS ** -0.5 * IDX_DIM ** -0.5
    pat[13, SM_R0:] = 1.0
    return jnp.asarray(pat)


def _rot(x, shift):
    return pltpu.roll(x, shift % x.shape[1], 1)


def _tile_lanes(v, reps):
    return jnp.concatenate([v] * reps, 1) if reps > 1 else v


def _dsa_prep_fn(q, ckv, qidx, small, pos, kvn, lng, lnb, pat):
    posf = pos.astype(F32)
    hq, hi = DSA_ROPE_DIM // 2, IDX_ROPE_DIM // 2
    ang = posf * pat[0:1]
    c, s = jnp.cos(ang), jnp.sin(ang)
    reps = q.shape[1] // LANES
    qr = (q * _tile_lanes(c, reps) + _rot(q, -hq) * _tile_lanes(s * pat[1:2], reps)
          + _rot(q, hq) * _tile_lanes(s * pat[2:3], reps))
    ang = posf * pat[3:4]
    c, s = jnp.cos(ang), jnp.sin(ang)
    reps = qidx.shape[1] // LANES
    qi = (qidx * _tile_lanes(c, reps) + _rot(qidx, -hi) * _tile_lanes(s * pat[4:5], reps)
          + _rot(qidx, hi) * _tile_lanes(s * pat[5:6], reps))
    ckn = ckv * lax.rsqrt(jnp.mean(ckv * ckv, -1, keepdims=True) + 1e-6) * kvn
    mk = pat[11:12]
    inv = 1.0 / IDX_DIM
    mu = jnp.sum(small * mk, -1, keepdims=True) * inv
    dv = (small - mu) * mk
    var = jnp.sum(dv * dv, -1, keepdims=True) * inv
    y = (dv * lax.rsqrt(var + LN_EPS) * lng + lnb) * mk + small * pat[12:13] + small * pat[13:14]
    ang = posf * pat[6:7]
    c, s = jnp.cos(ang), jnp.sin(ang)
    so = (y * c + _rot(y, -hi) * (s * pat[7:8]) + _rot(y, hi) * (s * pat[8:9])
          + _rot(y, -hq) * (s * pat[9:10]) + _rot(y, hq) * (s * pat[10:11]))
    lat = jnp.concatenate([ckn, so * pat[13:14], jnp.zeros_like(so)], 1)
    lo = lax.bitcast_convert_type(lat[:, :LAT_PACK].astype(BF16).astype(F32), jnp.uint32) >> 16
    hi_b = lax.bitcast_convert_type(lat[:, LAT_PACK:].astype(BF16).astype(F32), jnp.uint32) & jnp.uint32(0xFFFF0000)
    return qr, lo | hi_b, qi, so, so


def _sortable(x):
    b = lax.bitcast_convert_type(x, jnp.int32)
    return jnp.where(b < 0, b ^ jnp.int32(0x7FFFFFFF), b)


def _dsa_select_body(qi_ref, sm_ref, kx_ref, sel_ref, key_ref, ib_ref, *, tq, ts, topk, seq):
    sub = 32
    t0 = pl.program_id(1) * tq
    nchunk = (t0 + tq + ts - 1) // ts
    qpos = t0 + lax.broadcasted_iota(jnp.int32, (tq, 1), 0)
    w = sm_ref[:, SM_W0:SM_W0 + IDX_HEADS]
    neg = jnp.float32(-jnp.inf)

    def score_chunk(c, carry):
        off = pl.multiple_of(c * ts, ts)
        ks = kx_ref[pl.ds(off, ts), :][:, :IDX_DIM]
        acc = jnp.zeros((tq, ts), F32)
        for h in range(IDX_HEADS):
            s = lax.dot_general(qi_ref[:, h * IDX_DIM:(h + 1) * IDX_DIM], ks, (((1,), (1,)), ((), ())),
                                preferred_element_type=F32)
            acc = acc + w[:, h:h + 1] * jnp.maximum(s, 0.0)
        kpos = off + lax.broadcasted_iota(jnp.int32, (1, ts), 1)
        key_ref[:, pl.ds(off, ts)] = _sortable(jnp.where(kpos <= qpos, acc, neg))
        return carry

    lax.fori_loop(0, nchunk, score_chunk, 0)

    tpc = ts // LANES
    lane_iota = lax.broadcasted_iota(jnp.int32, (1, LANES), 1)

    def count(pred):
        def body(c, acc):
            for u in range(tpc):
                off = pl.multiple_of(c * ts + u * LANES, LANES)
                acc = acc + pred(key_ref[:, pl.ds(off, LANES)], off + lane_iota).astype(jnp.int32)
            return acc
        acc = lax.fori_loop(0, nchunk, body, jnp.zeros((tq, LANES), jnp.int32))
        return jnp.sum(acc, -1, keepdims=True)

    sign = jnp.int32(-2 ** 31)
    thr_u = jnp.zeros((tq, 1), jnp.int32)
    for bit in range(31, -1, -1):
        cand_u = thr_u | jnp.int32(np.uint32(1 << bit).astype(np.int32))
        cand = cand_u ^ sign
        cnt = count(lambda k, kp, cand=cand: k >= cand)
        thr_u = jnp.where(cnt >= topk, cand_u, thr_u)
    thr = thr_u ^ sign
    need = topk - count(lambda k, kp: k > thr)
    n_eq = count(lambda k, kp: k == thr)

    ib_ref[...] = jnp.full((tq, 1), seq, jnp.int32)

    @pl.when(jnp.max(n_eq - need) > 0)
    def _():
        ib = jnp.zeros((tq, 1), jnp.int32)
        for bit in range(int(math.log2(seq)) - 1, -1, -1):
            cand = ib | jnp.int32(1 << bit)
            cnt = count(lambda k, kp, cand=cand: (k == thr) & (kp < cand))
            ib = jnp.where(cnt < need, cand, ib)
        ib_ref[...] = ib

    ib = ib_ref[...]

    def mask_tiles(c, acc):
        for u in range(tpc):
            off = pl.multiple_of(c * ts + u * LANES, LANES)
            kpos = off + lane_iota
            k = key_ref[:, pl.ds(off, LANES)]
            m = (((k > thr) | ((k == thr) & (kpos <= ib))) & (kpos <= qpos)).astype(jnp.int32)
            key_ref[:, pl.ds(off, LANES)] = m
            acc = acc + m
        return acc

    colcnt = lax.fori_loop(0, nchunk, mask_tiles, jnp.zeros((tq, LANES), jnp.int32))
    li = lax.broadcasted_iota(jnp.int32, (LANES, LANES), 0)
    lj = lax.broadcasted_iota(jnp.int32, (LANES, LANES), 1)
    cin = jnp.dot(colcnt.astype(BF16), (li <= lj).astype(BF16), preferred_element_type=F32).astype(jnp.int32)
    total = cin[:, LANES - 1:LANES]
    slot = lax.broadcasted_iota(jnp.int32, (tq, topk), 1)
    lane_of = jnp.zeros((tq, topk), jnp.int32)
    base = jnp.zeros((tq, topk), jnp.int32)
    for l in range(LANES):
        c = cin[:, l:l + 1]
        le = c <= slot
        lane_of = lane_of + le.astype(jnp.int32)
        base = jnp.maximum(base, jnp.where(le, c, 0))
    lane_of = jnp.minimum(lane_of, LANES - 1)
    rank = slot - base
    nhalf = topk // LANES
    for r0 in range(0, tq, sub):
        rows = slice(r0, r0 + sub)
        lanes_h = [lane_of[rows, i * LANES:(i + 1) * LANES] for i in range(nhalf)]
        ranks_h = [rank[rows, i * LANES:(i + 1) * LANES] for i in range(nhalf)]

        def tile_body(c, carry):
            carry = list(carry)
            for u in range(tpc):
                off = pl.multiple_of(c * ts + u * LANES, LANES)
                carry[0] = carry[0] + key_ref[r0:r0 + sub, pl.ds(off, LANES)]
                for i in range(nhalf):
                    got = jnp.take_along_axis(carry[0], lanes_h[i], axis=1)
                    carry[1 + i] = carry[1 + i] + (got <= ranks_h[i]).astype(jnp.int32)
            return tuple(carry)

        res = lax.fori_loop(0, nchunk, tile_body, (jnp.zeros((sub, LANES), jnp.int32),) * (1 + nhalf))
        pos = jnp.concatenate([res[1 + i] * LANES + lanes_h[i] for i in range(nhalf)], 1)
        slot_sub = lax.broadcasted_iota(jnp.int32, (sub, topk), 1)
        sel_ref[rows, :] = jnp.where(slot_sub < total[rows], pos, -1)


def dsa_select(qi, small, kx, *, batch, seq):
    n = qi.shape[0]
    topk = min(TOPK_MAX, seq // 4)
    tq = min(128, seq)
    ts = min(512, seq)
    qpb = seq // tq
    return pl.pallas_call(
        functools.partial(_dsa_select_body, tq=tq, ts=ts, topk=topk, seq=seq),
        grid=(batch, qpb),
        in_specs=[pl.BlockSpec((tq, qi.shape[1]), lambda b, i: (b * qpb + i, 0)),
                  pl.BlockSpec((tq, LANES), lambda b, i: (b * qpb + i, 0)),
                  pl.BlockSpec((seq, LANES), lambda b, i: (b, 0))],
        out_specs=pl.BlockSpec((tq, topk), lambda b, i: (b * qpb + i, 0)),
        out_shape=jax.ShapeDtypeStruct((n, topk), jnp.int32),
        scratch_shapes=[pltpu.VMEM((tq, seq), jnp.int32), pltpu.VMEM((tq, 1), jnp.int32)],
        compiler_params=_cparams("parallel", "arbitrary"),
        name="dsa_select",
    )(qi, small, kx)


def _dsa_attn_body(sel_s, selv_ref, q2_ref, tbl_ref, o_ref, *stages, tqa, topk):
    heads = DSA_HEADS
    scale = DSA_HEAD_DIM ** -0.5
    lt = LAT_TILES

    def gather(i, buf):
        base = i * topk
        for j in range(topk):
            stages[buf][pl.ds(j * lt, lt), :] = tbl_ref[pl.ds(sel_s[base + j], lt), :]

    def attend(i0, bufs):
        qs = range(len(bufs))
        pk = [jnp.concatenate([stages[b][pl.ds(t, topk, stride=lt), :] for t in range(lt)], 1) for b in bufs]
        kv = [jnp.concatenate([lax.bitcast_convert_type(x << 16, F32),
                               lax.bitcast_convert_type(x & jnp.uint32(0xFFFF0000), F32)], 1).astype(BF16)
              for x in pk]
        rows = [pl.ds(pl.multiple_of((i0 + a) * heads, heads), heads) for a in qs]
        s = [lax.dot_general(q2_ref[rows[a], :], kv[a], (((1,), (1,)), ((), ())), preferred_element_type=F32) * scale
             for a in qs]
        s = [jnp.where(selv_ref[pl.ds(i0 + a, 1), :] >= 0, s[a], -jnp.inf) for a in qs]
        e = [jnp.exp(s[a] - jnp.max(s[a], -1, keepdims=True)) for a in qs]
        p = [(e[a] / jnp.sum(e[a], -1, keepdims=True)).astype(BF16) for a in qs]
        o = [jnp.dot(p[a], kv[a][:, :DSA_KV_RANK], preferred_element_type=F32) for a in qs]
        for a in qs:
            o_ref[rows[a], :] = o[a].astype(o_ref.dtype)

    nq = len(stages) // 2
    first, second = tuple(range(nq)), tuple(range(nq, 2 * nq))
    for a in first:
        gather(a, a)

    def body(pi, carry):
        i0 = 2 * nq * pi
        for a in first:
            gather(i0 + nq + a, second[a])
        attend(i0, first)
        for a in first:
            gather(jnp.minimum(i0 + 2 * nq + a, tqa - 1), a)
        attend(i0 + nq, second)
        return carry

    lax.fori_loop(0, tqa // (2 * nq), body, 0)


def dsa_attend(sel, q2, table, *, batch, seq):
    n, topk = sel.shape
    tqa = min(64, seq)
    qpb = seq // tqa
    heads = DSA_HEADS
    return pl.pallas_call(
        functools.partial(_dsa_attn_body, tqa=tqa, topk=topk),
        grid=(batch, qpb),
        in_specs=[pl.BlockSpec((tqa * topk,), lambda b, i: (b * qpb + i,), memory_space=pltpu.SMEM),
                  pl.BlockSpec((tqa, topk), lambda b, i: (b * qpb + i, 0)),
                  pl.BlockSpec((tqa * heads, LAT_W), lambda b, i: (b * qpb + i, 0)),
                  pl.BlockSpec((seq * LAT_TILES, LANES), lambda b, i: (b, 0), pipeline_mode=pl.Buffered(1))],
        out_specs=pl.BlockSpec((tqa * heads, DSA_KV_RANK), lambda b, i: (b * qpb + i, 0)),
        out_shape=jax.ShapeDtypeStruct((n * heads, DSA_KV_RANK), BF16),
        scratch_shapes=[pltpu.VMEM((topk * LAT_TILES, LANES), jnp.uint32)] * (2 * DSA_QUERIES_PER_ATTEND),
        compiler_params=_cparams("parallel", "arbitrary"),
        name="dsa_attend",
    )((jnp.maximum(sel, 0) * LAT_TILES).reshape(-1), sel, q2, table.reshape(n * LAT_TILES, LANES))


def _head_mm_body(a_ref, w_ref, o_ref):
    o_ref[0] = jnp.dot(a_ref[...].astype(BF16), w_ref[0], preferred_element_type=F32).astype(o_ref.dtype)


def head_mm_out(a, w, *, tm=512):
    n = a.shape[0]
    heads, k, m = w.shape
    tm = min(tm, n)
    return pl.pallas_call(
        _head_mm_body,
        grid=(n // tm, heads),
        in_specs=[pl.BlockSpec((tm, k), lambda i, h: (i, h)), pl.BlockSpec((1, k, m), lambda i, h: (h, 0, 0))],
        out_specs=pl.BlockSpec((1, tm, m), lambda i, h: (h, i, 0)),
        out_shape=jax.ShapeDtypeStruct((heads, n, m), BF16),
        compiler_params=_cparams("parallel", "parallel"),
        name="head_mm_out",
    )(a, w)


def _head_mm_in_body(a_ref, w_ref, o_ref):
    o_ref[...] = jnp.dot(a_ref[0], w_ref[0], preferred_element_type=F32).astype(o_ref.dtype)


def head_mm_in(a, w, *, tm=512):
    heads, n, k = a.shape
    m = w.shape[2]
    tm = min(tm, n)
    return pl.pallas_call(
        _head_mm_in_body,
        grid=(n // tm, heads),
        in_specs=[pl.BlockSpec((1, tm, k), lambda i, h: (h, i, 0)), pl.BlockSpec((1, k, m), lambda i, h: (h, 0, 0))],
        out_specs=pl.BlockSpec((tm, m), lambda i, h: (i, h)),
        out_shape=jax.ShapeDtypeStruct((n, heads * m), BF16),
        compiler_params=_cparams("parallel", "parallel"),
        name="head_mm_in",
    )(a, w)


def dsa_layer(hbf, positions, p, *, batch, seq):
    n = hbf.shape[0]
    heads = DSA_HEADS
    q = mm(hbf, p["w_q"])
    ckv = mm(hbf, p["w_ckv"])
    qidx = mm(hbf, p["w_qidx"])
    small = mm(hbf, p["w_small"])
    qr, table, qi, small_o, kx = rowwise(
        _dsa_prep_fn, [q, ckv, qidx, small, positions.reshape(n, 1)], [],
        [p["kv_norm"], p["idx_ln_g"], p["idx_ln_b"], _dsa_patterns()],
        [(q.shape[1], BF16), (LAT_PACK, jnp.uint32), (qidx.shape[1], BF16), (LANES, F32), (LANES, BF16)],
        seq=seq, tm=128, name="dsa_prep")
    sel = dsa_select(qi, small_o, kx, batch=batch, seq=seq)
    q2 = head_mm_out(qr, p["w_q2"])
    q2 = q2.transpose(1, 0, 2).reshape(n * heads, LAT_W)
    o_lat = dsa_attend(sel, q2, table, batch=batch, seq=seq)
    o_lat = o_lat.reshape(n, heads, DSA_KV_RANK).transpose(1, 0, 2)
    out = head_mm_in(o_lat, p["w_uv"])
    return mm(out, p["w_o"])


def _s5_tables(p):
    hp = lax.Precision.HIGHEST
    lc, hg = S5_CHUNK, S5_GROUP
    lr, li = p["lam_re"], p["lam_im"]
    g, ps = lr.shape
    dt = jnp.exp(p["log_dt"])[:, None]
    mag = jnp.exp(lr * dt)
    ab_re, ab_im = mag * jnp.cos(li * dt), mag * jnp.sin(li * dt)
    den = lr * lr + li * li
    nr, ni = ab_re - 1.0, ab_im
    coef_re = (nr * lr + ni * li) / den
    coef_im = (ni * lr - nr * li) / den
    bb_re = coef_re[..., None] * p["b_re"] - coef_im[..., None] * p["b_im"]
    bb_im = coef_re[..., None] * p["b_im"] + coef_im[..., None] * p["b_re"]
    tau = jnp.arange(lc + 1, dtype=F32)[:, None, None]
    pmag = jnp.exp(lr * dt * tau)
    pr, pi = pmag * jnp.cos(li * dt * tau), pmag * jnp.sin(li * dt * tau)
    cr, ci = p["c_re"][None], p["c_im"][None]
    car = cr * pr[:, :, None, :] - ci * pi[:, :, None, :]
    cai = cr * pi[:, :, None, :] + ci * pr[:, :, None, :]
    kern = (jnp.einsum('tghp,gpk->tghk', car[:lc], bb_re, precision=hp)
            - jnp.einsum('tghp,gpk->tghk', cai[:lc], bb_im, precision=hp))
    kz = jnp.concatenate([kern, jnp.zeros((1,) + kern.shape[1:], F32)], 0)
    s_i = np.arange(lc)[:, None]
    t_i = np.arange(lc)[None, :]
    m = kz[np.where(t_i >= s_i, t_i - s_i, lc)]
    m = m.transpose(2, 0, 4, 1, 3).reshape(g, lc * hg, lc * hg)
    skip = jnp.tile(p["d"].reshape(g, 1, hg), (1, lc, 1)).reshape(g, 1, lc * hg)
    m = m + skip * jnp.eye(lc * hg, dtype=F32)[None]
    prs, pis = pr[lc - 1 - np.arange(lc)], pi[lc - 1 - np.arange(lc)]
    wre = prs[..., None] * bb_re[None] - pis[..., None] * bb_im[None]
    wim = prs[..., None] * bb_im[None] + pis[..., None] * bb_re[None]
    wre = wre.transpose(1, 0, 3, 2).reshape(g, lc * hg, ps)
    wim = wim.transpose(1, 0, 3, 2).reshape(g, lc * hg, ps)
    vre = car[1:].transpose(1, 3, 0, 2).reshape(g, ps, lc * hg)
    vim = (-cai[1:]).transpose(1, 3, 0, 2).reshape(g, ps, lc * hg)
    gu = S5_UNIT
    eye_u = jnp.eye(gu, dtype=F32)

    def unit(z, row_t, col_t):
        r, c = z.shape[1:]
        zz = z.reshape(g // gu, gu, r, 1, c) * eye_u[None, :, None, :, None]
        if row_t:
            zz = zz.reshape(g // gu, gu, lc, hg, gu, c).transpose(0, 2, 1, 3, 4, 5).reshape(g // gu, gu * r, gu, c)
        else:
            zz = zz.reshape(g // gu, gu * r, gu, c)
        if col_t:
            zz = zz.reshape(g // gu, gu * r, gu, lc, hg).transpose(0, 1, 3, 2, 4)
        return zz.reshape(g // gu, gu * r, gu * c).astype(BF16)

    are = pr[lc].reshape(g // gu, 1, gu * ps)
    aim = pi[lc].reshape(g // gu, 1, gu * ps)
    return (unit(m, True, True), unit(wre, True, False), unit(wim, True, False),
            unit(vre, False, True), unit(vim, False, True), are, aim)


def _gelu_tanh(y):
    return 0.5 * y * (1.0 + jnp.tanh(math.sqrt(2.0 / math.pi) * (y + 0.044715 * (y * y * y))))


def _s5_body(u_ref, m_ref, wre_ref, wim_ref, vre_ref, vim_ref, are_ref, aim_ref, z_ref, xre, xim, sre, sim, *, pairs):
    nc = u_ref.shape[1]
    for q in range(pairs):
        u = u_ref[q]
        xre[q] = jnp.dot(u, wre_ref[q], preferred_element_type=F32)
        xim[q] = jnp.dot(u, wim_ref[q], preferred_element_type=F32)
    ar = [are_ref[q] for q in range(pairs)]
    ai = [aim_ref[q] for q in range(pairs)]

    def step(c, carry):
        new = []
        row = pl.ds(c, 1)
        for q in range(pairs):
            re, im = carry[2 * q], carry[2 * q + 1]
            sre[q, row, :] = re
            sim[q, row, :] = im
            new.append(ar[q] * re - ai[q] * im + xre[q, row, :])
            new.append(ar[q] * im + ai[q] * re + xim[q, row, :])
        return tuple(new)

    zero = jnp.zeros((1, are_ref.shape[-1]), F32)
    lax.fori_loop(0, nc, step, (zero,) * (2 * pairs), unroll=8)
    rb = min(256, nc)
    for q in range(pairs):
        def out_block(i, carry, q=q):
            rs = pl.ds(pl.multiple_of(i * rb, rb), rb)
            y = (jnp.dot(u_ref[q, rs, :], m_ref[q], preferred_element_type=F32)
                 + jnp.dot(sre[q, rs, :].astype(BF16), vre_ref[q], preferred_element_type=F32)
                 + jnp.dot(sim[q, rs, :].astype(BF16), vim_ref[q], preferred_element_type=F32))
            z_ref[q, rs, :] = _gelu_tanh(y).astype(z_ref.dtype)
            return carry

        lax.fori_loop(0, nc // rb, out_block, 0)


def s5_layer(hbf, p, *, batch, seq):
    n, D = hbf.shape
    lc, hg = S5_CHUNK, S5_GROUP
    gu = S5_UNIT
    g2 = D // hg // gu
    pw = gu * lc * hg
    tabs = _s5_tables(p)
    nc = seq // lc
    u = hbf.reshape(n // lc, lc, g2, gu * hg).transpose(2, 0, 1, 3).reshape(g2, n // lc, pw)
    pairs = 1
    sw = gu * S5_STATE
    wspec = lambda r, c: pl.BlockSpec((pairs, r, c), lambda i, b: (i, 0, 0), pipeline_mode=pl.Buffered(1))
    z = pl.pallas_call(
        functools.partial(_s5_body, pairs=pairs),
        grid=(g2 // pairs, batch),
        in_specs=[pl.BlockSpec((pairs, nc, pw), lambda i, b: (i, b, 0)),
                  wspec(pw, pw), wspec(pw, sw), wspec(pw, sw), wspec(sw, pw), wspec(sw, pw),
                  wspec(1, sw), wspec(1, sw)],
        out_specs=pl.BlockSpec((pairs, nc, pw), lambda i, b: (i, b, 0)),
        out_shape=jax.ShapeDtypeStruct((g2, n // lc, pw), BF16),
        scratch_shapes=[pltpu.VMEM((pairs, nc, sw), F32)] * 4,
        compiler_params=_cparams("parallel", "parallel"),
        name="s5_scan",
    )(u, *tabs)
    z = z.reshape(g2, n // lc, lc, gu * hg).transpose(1, 2, 0, 3).reshape(n, D)
    zz = mm(z, p["w_glu"])
    return rowwise(lambda a: (a[:, :D] * jax.nn.sigmoid(a[:, D:]),), [zz], [], [], [(D, F32)], seq=seq, tm=256,
                   name="s5_glu")[0]


def _modulate(x, sc, sh, dtype, *, seq):
    d = x.shape[1]
    return rowwise(lambda xv, s, h: (xv * (1.0 + s) + h,), [x], [sc, sh], [], [(d, dtype)], seq=seq, tm=256,
                   name="modulate")[0]


def kernel(x, c, positions, ada_w, ada_b, ln_g, ln_b, router_w, router_bias, moe_w_gate, moe_w_up, moe_w_down,
           rwkv_mu, rwkv_w_rkv, rwkv_w_o, rwkv_w0, rwkv_w1, rwkv_w2, rwkv_a0, rwkv_a1, rwkv_a2, rwkv_g1, rwkv_g2,
           rwkv_k_k, rwkv_k_a, rwkv_r_k, rwkv_gn_g, rwkv_gn_b, rwkv_v0, rwkv_v1, rwkv_v2, dsa_w_in, dsa_kv_norm,
           dsa_w_uk, dsa_w_uv, dsa_idx_ln_g, dsa_idx_ln_b, dsa_w_o, s5_lam_re, s5_lam_im, s5_log_dt, s5_b_re,
           s5_b_im, s5_c_re, s5_c_im, s5_d, s5_w_glu):
    batch, seq, d = x.shape
    depth = ada_w.shape[0]
    n = batch * seq
    xf = x.reshape(n, d)
    mod = ada_mod(c, ada_w, ada_b)[:, :batch].reshape(depth, batch, 1, 6, d)
    v_first = None
    for i in range(depth):
        kind, j = i % N_MIXERS, i // N_MIXERS
        sh1, sc1, g1, sh2, sc2, g2 = (mod[i, :, :, m] for m in range(6))
        if kind == 0:
            p = dict(mu=rwkv_mu[j], w_r=rwkv_w_rkv[j, 0].astype(BF16), w_k=rwkv_w_rkv[j, 1].astype(BF16),
                     w_v=rwkv_w_rkv[j, 2].astype(BF16), w_o=rwkv_w_o[j].astype(BF16),
                     w0=rwkv_w0[j], w1=_wpad(rwkv_w1[j]), w2=_kpad(rwkv_w2[j]),
                     a0=rwkv_a0[j], a1=_wpad(rwkv_a1[j]), a2=_kpad(rwkv_a2[j]),
                     g1=_wpad(rwkv_g1[j]), g2=_kpad(rwkv_g2[j]),
                     k_k=rwkv_k_k[j], k_a=rwkv_k_a[j], r_k=rwkv_r_k[j], gn_g=rwkv_gn_g[j], gn_b=rwkv_gn_b[j],
                     v0=None)
            if j > 0:
                p.update(v0=rwkv_v0[j - 1], v1=_wpad(rwkv_v1[j - 1]), v2=_kpad(rwkv_v2[j - 1]))
            y, v_first = rwkv_layer(xf, sc1, sh1, v_first, p, batch=batch, seq=seq)
        elif kind == 1:
            w_in = dsa_w_in[j]
            hq = DSA_HEADS * DSA_HEAD_DIM
            o1 = hq + DSA_KV_RANK
            o2 = o1 + DSA_ROPE_DIM
            o3 = o2 + IDX_HEADS * IDX_DIM
            o4 = o3 + IDX_DIM
            gap = jnp.zeros((d, SM_R0 - SM_W0 - IDX_HEADS), F32)
            w_small = jnp.concatenate([w_in[:, o3:o4], w_in[:, o4:], gap, w_in[:, o1:o2]], axis=1)
            rope_pass = jnp.zeros((DSA_ROPE_DIM, LAT_W), F32).at[
                jnp.arange(DSA_ROPE_DIM), DSA_KV_RANK + SM_R0 + jnp.arange(DSA_ROPE_DIM)].set(1.0)
            w_uk_pad = jnp.pad(dsa_w_uk[j], ((0, 0), (0, 0), (0, LAT_W - DSA_KV_RANK)))
            w_q2 = jnp.concatenate([jnp.broadcast_to(rope_pass, (DSA_HEADS,) + rope_pass.shape), w_uk_pad], axis=1)
            lane_pad = lambda v: jnp.pad(v, (0, LANES - v.shape[0])).reshape(1, LANES)
            p = dict(w_q=w_in[:, :hq].astype(BF16), w_ckv=w_in[:, hq:o1].astype(BF16),
                     w_qidx=w_in[:, o2:o3].astype(BF16), w_small=w_small.astype(BF16),
                     kv_norm=dsa_kv_norm[j].reshape(1, -1), w_q2=w_q2.astype(BF16), w_uv=dsa_w_uv[j].astype(BF16),
                     idx_ln_g=lane_pad(dsa_idx_ln_g[j]), idx_ln_b=lane_pad(dsa_idx_ln_b[j]),
                     w_o=dsa_w_o[j].astype(BF16))
            hbf = _modulate(xf, sc1, sh1, BF16, seq=seq)
            y = dsa_layer(hbf, positions, p, batch=batch, seq=seq)
        else:
            p = dict(lam_re=s5_lam_re[j], lam_im=s5_lam_im[j], log_dt=s5_log_dt[j], b_re=s5_b_re[j],
                     b_im=s5_b_im[j], c_re=s5_c_re[j], c_im=s5_c_im[j], d=s5_d[j], w_glu=s5_w_glu[j].astype(BF16))
            hbf = _modulate(xf, sc1, sh1, BF16, seq=seq)
            y = s5_layer(hbf, p, batch=batch, seq=seq)
        xf, hbf = post_norm_mix(xf, y, g1, sc2, sh2, ln_g[i, 0:1], ln_b[i, 0:1], seq=seq)
        y0, y1, gates = moe_ffn(xf, hbf, sc2, sh2, router_w, router_bias, moe_w_gate[i].astype(BF16),
                                moe_w_up[i].astype(BF16), moe_w_down[i].astype(BF16), seq=seq)
        xf = post_norm_moe(xf, y0, y1, gates, g2, ln_g[i, 1:2], ln_b[i, 1:2], seq=seq)
    return xf.reshape(batch, seq, d)
```

```python
import functools
import itertools
import math

import numpy as np
import jax
import jax.numpy as jnp
from jax import lax
from jax.experimental import pallas as pl
from jax.experimental.pallas import tpu as pltpu

F32, BF16 = jnp.float32, jnp.bfloat16

DEPTH = 4
N_MIXERS = 3
RWKV_HEAD = 64
RWKV_GN_EPS = 64e-5
DSA_HEADS = 16
DSA_HEAD_DIM = 128
DSA_ROPE_DIM = 32
DSA_NOPE_DIM = 96
DSA_V_DIM = 128
DSA_KV_RANK = 512
IDX_HEADS = 16
IDX_DIM = 64
IDX_ROPE_DIM = 16
TOPK_MAX = 256
Q_BLOCK = 128
S5_GROUP = 16
S5_STATE = 64
S5_CHUNK = 16
S5_UNIT = 8
N_EXPERTS = 32
N_GROUPS = 4
EXPERTS_PER_GROUP = 8
TOP_K = 2
ROPE_THETA = 500000.0
LN_EPS = 1e-5
DEEPNORM_ALPHA = (2 * DEPTH) ** 0.25

LANES = 128
VMEM_LIMIT = 48 * 1024 * 1024
MOE_ROWS = 256
WKV_CHUNK = 64
WKV_HEADS_PER_STEP = 4
WKV_GROUPS_PER_STEP = 4
WKV_ROWS_PER_STEP = 512


def _cparams(*sem):
    return pltpu.CompilerParams(dimension_semantics=sem, vmem_limit_bytes=VMEM_LIMIT)


def _pick(n, cands):
    for c in cands:
        if n % c == 0:
            return c
    return n


def _mm_body(a_ref, w_ref, o_ref, *, act):
    acc = jnp.dot(a_ref[...].astype(BF16), w_ref[...], preferred_element_type=F32)
    if act == "tanh":
        acc = jnp.tanh(acc)
    elif act == "sigmoid":
        acc = jax.nn.sigmoid(acc)
    o_ref[...] = acc.astype(o_ref.dtype)


def mm(a, w, *, out_dtype=F32, act=None, tm=512):
    m, k = a.shape
    n = w.shape[1]
    tm = _pick(m, (tm, 256, 128, 64, 32, 16, 8))
    tn = _pick(n, (512, 256, 128))
    return pl.pallas_call(
        functools.partial(_mm_body, act=act),
        grid=(m // tm, n // tn),
        in_specs=[pl.BlockSpec((tm, k), lambda i, j: (i, 0)), pl.BlockSpec((k, tn), lambda i, j: (0, j))],
        out_specs=pl.BlockSpec((tm, tn), lambda i, j: (i, j)),
        out_shape=jax.ShapeDtypeStruct((m, n), out_dtype),
        compiler_params=_cparams("parallel", "parallel"),
        name="mm",
    )(a, w)


def _wpad(w):
    n = w.shape[1]
    npad = -(-n // LANES) * LANES
    w = w.astype(BF16)
    return w if npad == n else jnp.pad(w, ((0, 0), (0, npad - n)))


def _kpad(w):
    k = w.shape[0]
    kpad = -(-k // LANES) * LANES
    w = w.astype(BF16)
    return w if kpad == k else jnp.pad(w, ((0, kpad - k), (0, 0)))


def rowwise(fn, rows, perbatch, consts, outs, *, seq, tm, name):
    n = rows[0].shape[0]
    tm = min(tm, seq)
    tpb = seq // tm
    nr, nb, nc = len(rows), len(perbatch), len(consts)

    def body(*refs):
        vals = [r[...] for r in refs[:nr]]
        vals += [r[0] for r in refs[nr:nr + nb]]
        vals += [r[...] for r in refs[nr + nb:nr + nb + nc]]
        res = fn(*vals)
        for o, v in zip(refs[nr + nb + nc:], res):
            o[...] = v.astype(o.dtype)

    in_specs = [pl.BlockSpec((tm, r.shape[1]), lambda i: (i, 0)) for r in rows]
    in_specs += [pl.BlockSpec((1, 1, p.shape[-1]), lambda i: (i // tpb, 0, 0)) for p in perbatch]
    in_specs += [pl.BlockSpec(c.shape, lambda i: (0, 0)) for c in consts]
    res = pl.pallas_call(
        body,
        grid=(n // tm,),
        in_specs=in_specs,
        out_specs=[pl.BlockSpec((tm, w), lambda i: (i, 0)) for w, _ in outs],
        out_shape=[jax.ShapeDtypeStruct((n, w), dt) for w, dt in outs],
        compiler_params=_cparams("parallel"),
        name=name,
    )(*rows, *perbatch, *consts)
    return res


def _layer_norm(z, g, b):
    mu = jnp.mean(z, -1, keepdims=True)
    zc = z - mu
    var = jnp.mean(zc * zc, -1, keepdims=True)
    return zc * lax.rsqrt(var + LN_EPS) * g + b


def _split_dot(x, m):
    xh = x.astype(BF16)
    xl = (x - xh.astype(F32)).astype(BF16)
    return jnp.dot(xh, m, preferred_element_type=F32) + jnp.dot(xl, m, preferred_element_type=F32)


def _seg_sum(x, ones_bd):
    parts = [_split_dot(x[:, j:j + LANES], ones_bd) for j in range(0, x.shape[1], LANES)]
    return jnp.concatenate(parts, -1)


def _head_ones():
    i = np.arange(LANES)
    return jnp.asarray((i[:, None] // RWKV_HEAD) == (i[None, :] // RWKV_HEAD), BF16)


def _ada_body(c_ref, w_ref, b_ref, o_ref):
    acc = jnp.dot(c_ref[...], w_ref[0], preferred_element_type=F32, precision=lax.Precision.HIGHEST)
    o_ref[0] = acc + b_ref[0]


def ada_mod(c, ada_w, ada_b):
    depth, d, n = ada_w.shape
    b = c.shape[0]
    cp = jnp.pad(c, ((0, 8 - b), (0, 0)))
    tn = 1024
    return pl.pallas_call(
        _ada_body,
        grid=(depth, n // tn),
        in_specs=[pl.BlockSpec((8, d), lambda i, j: (0, 0)),
                  pl.BlockSpec((1, d, tn), lambda i, j: (i, 0, j)),
                  pl.BlockSpec((1, 1, tn), lambda i, j: (i, 0, j))],
        out_specs=pl.BlockSpec((1, 8, tn), lambda i, j: (i, 0, j)),
        out_shape=jax.ShapeDtypeStruct((depth, 8, n), F32),
        compiler_params=_cparams("parallel", "parallel"),
        name="ada_mod",
    )(cp, ada_w, ada_b.reshape(depth, 1, n))


def post_norm_mix(x, y, gate, sc, sh, lng, lnb, *, seq):
    d = x.shape[1]

    def fn(xv, yv, g, s, h, lg, lb):
        xn = _layer_norm(DEEPNORM_ALPHA * xv + (1.0 + g) * yv, lg, lb)
        return xn, xn * (1.0 + s) + h

    return rowwise(fn, [x, y], [gate, sc, sh], [lng, lnb], [(d, F32), (d, BF16)], seq=seq, tm=256, name="post_norm_mix")


def post_norm_moe(x, y0, y1, gates, gate, lng, lnb, *, seq):
    d = x.shape[1]

    def fn(xv, a, b, gt, g, lg, lb):
        y = a * gt[:, 0:1] + b * gt[:, 1:2]
        return (_layer_norm(DEEPNORM_ALPHA * xv + (1.0 + g) * y, lg, lb),)

    return rowwise(fn, [x, y0, y1, gates], [gate], [lng, lnb], [(d, F32)], seq=seq, tm=256, name="post_norm_moe")[0]


def _router_fn(xv, s, h, rw, rb):
    hf = xv * (1.0 + s) + h
    logits = jnp.dot(hf, rw, preferred_element_type=F32, precision=lax.Precision.HIGHEST)
    scores = jax.nn.sigmoid(logits)
    biased = scores + rb
    lane = lax.broadcasted_iota(jnp.int32, biased.shape, 1)
    neg = jnp.float32(-jnp.inf)
    big = jnp.int32(1 << 20)
    best = bi1 = bi2 = None
    for g in range(N_GROUPS):
        ing = (lane >= g * EXPERTS_PER_GROUP) & (lane < (g + 1) * EXPERTS_PER_GROUP)
        v = jnp.where(ing, biased, neg)
        m1 = jnp.max(v, -1, keepdims=True)
        i1 = jnp.min(jnp.where(v == m1, lane, big), -1, keepdims=True)
        v2 = jnp.where(lane == i1, neg, v)
        m2 = jnp.max(v2, -1, keepdims=True)
        i2 = jnp.min(jnp.where(v2 == m2, lane, big), -1, keepdims=True)
        gs = m1 + m2
        if g == 0:
            best, bi1, bi2 = gs, i1, i2
        else:
            better = gs > best
            best = jnp.where(better, gs, best)
            bi1 = jnp.where(better, i1, bi1)
            bi2 = jnp.where(better, i2, bi2)
    g1 = jnp.sum(jnp.where(lane == bi1, scores, 0.0), -1, keepdims=True)
    g2 = jnp.sum(jnp.where(lane == bi2, scores, 0.0), -1, keepdims=True)
    tot = g1 + g2
    two = lax.broadcasted_iota(jnp.int32, (biased.shape[0], TOP_K), 1)
    return jnp.where(two == 0, bi1, bi2), jnp.where(two == 0, g1 / tot, g2 / tot)


def _expert_body(be_ref, nb_ref, x_ref, wg_ref, wu_ref, wd_ref, o_ref):
    del be_ref

    @pl.when(pl.program_id(0) < nb_ref[0])
    def _():
        x = x_ref[...]
        g = jnp.dot(x, wg_ref[0], preferred_element_type=F32)
        u = jnp.dot(x, wu_ref[0], preferred_element_type=F32)
        hid = (g * jax.nn.sigmoid(g) * u).astype(BF16)
        o_ref[...] = jnp.dot(hid, wd_ref[0], preferred_element_type=F32).astype(o_ref.dtype)

    @pl.when(pl.program_id(0) >= nb_ref[0])
    def _():
        o_ref[...] = jnp.zeros_like(o_ref)


def moe_ffn(x, hbf, sc, sh, router_w, router_bias, wg, wu, wd, *, seq):
    n, d = x.shape
    e = router_w.shape[1]
    idx, gates = rowwise(_router_fn, [x], [sc, sh], [router_w, router_bias.reshape(1, e)],
                         [(TOP_K, jnp.int32), (TOP_K, F32)], seq=seq, tm=256, name="router")
    bm = MOE_ROWS
    n_slots = n * TOP_K
    flat_e = idx.reshape(-1)
    onehot = (flat_e[:, None] == jnp.arange(e, dtype=jnp.int32)[None, :]).astype(jnp.int32)
    csum = jnp.cumsum(onehot, axis=0)
    pos = jnp.sum(onehot * csum, axis=1) - 1
    counts = csum[-1]
    padded = (counts + bm - 1) // bm * bm
    pend = jnp.cumsum(padded)
    pstart = pend - padded
    dest = pstart[flat_e] + pos
    n_blocks = n_slots // bm + e
    n_pad = n_blocks * bm
    slot_tok = jnp.zeros((n_pad,), jnp.int32).at[dest].set(jnp.arange(n_slots, dtype=jnp.int32) // TOP_K)
    block_start = jnp.arange(n_blocks, dtype=jnp.int32) * bm
    block_expert = jnp.minimum(jnp.sum((pend[None, :] <= block_start[:, None]).astype(jnp.int32), axis=1), e - 1)
    used_blocks = (pend[-1] // bm).astype(jnp.int32).reshape(1)
    xs = jnp.take(hbf, slot_tok, axis=0)
    de = wg.shape[2]
    yb = pl.pallas_call(
        _expert_body,
        grid_spec=pltpu.PrefetchScalarGridSpec(
            num_scalar_prefetch=2,
            grid=(n_blocks,),
            in_specs=[pl.BlockSpec((bm, d), lambda i, be, nb: (i, 0)),
                      pl.BlockSpec((1, d, de), lambda i, be, nb: (be[i], 0, 0)),
                      pl.BlockSpec((1, d, de), lambda i, be, nb: (be[i], 0, 0)),
                      pl.BlockSpec((1, de, d), lambda i, be, nb: (be[i], 0, 0))],
            out_specs=pl.BlockSpec((bm, d), lambda i, be, nb: (i, 0)),
        ),
        out_shape=jax.ShapeDtypeStruct((n_pad, d), BF16),
        compiler_params=_cparams("arbitrary"),
        name="moe_experts",
    )(block_expert, used_blocks, xs, wg, wu, wd)
    dest2 = dest.reshape(n, TOP_K)
    return jnp.take(yb, dest2[:, 0], axis=0), jnp.take(yb, dest2[:, 1], axis=0), gates


def _rwkv_prep_body(x_ref, xp_ref, sc_ref, sh_ref, mu_ref, *outs, tpb):
    sc = 1.0 + sc_ref[0]
    sh = sh_ref[0]
    h = x_ref[...] * sc + sh
    prev_row = xp_ref[7:8, :] * sc + sh
    prev_row = jnp.where(pl.program_id(0) % tpb == 0, 0.0, prev_row)
    rowid = lax.broadcasted_iota(jnp.int32, h.shape, 0)
    hprev = jnp.where(rowid == 0, prev_row, pltpu.roll(h, 1, 0))
    dx = hprev - h
    for m, o in enumerate(outs):
        o[...] = (h + dx * mu_ref[m:m + 1, :]).astype(o.dtype)


def rwkv_prep(x, sc, sh, mu, *, seq):
    n, d = x.shape
    tm = min(256, seq)
    tpb = seq // tm
    return pl.pallas_call(
        functools.partial(_rwkv_prep_body, tpb=tpb),
        grid=(n // tm,),
        in_specs=[pl.BlockSpec((tm, d), lambda i: (i, 0)),
                  pl.BlockSpec((8, d), lambda i: (jnp.maximum(i * (tm // 8) - 1, 0), 0)),
                  pl.BlockSpec((1, 1, d), lambda i: (i // tpb, 0, 0)),
                  pl.BlockSpec((1, 1, d), lambda i: (i // tpb, 0, 0)),
                  pl.BlockSpec((6, d), lambda i: (0, 0))],
        out_specs=[pl.BlockSpec((tm, d), lambda i: (i, 0))] * 6,
        out_shape=[jax.ShapeDtypeStruct((n, d), BF16)] * 6,
        compiler_params=_cparams("parallel"),
        name="rwkv_prep",
    )(x, x, sc, sh, mu)


def _softplus(z):
    return jnp.maximum(z, 0.0) + jnp.log(1.0 + jnp.exp(-jnp.abs(z)))


def _rwkv_gates_fn(k0, lw, al, *rest, has_vres):
    if has_vres:
        v, vfirst, vl, prm, ones_bd = rest
    else:
        prm, ones_bd = rest
    w0, a0, k_k, k_a = prm[0:1], prm[1:2], prm[2:3], prm[3:4]
    log_w = -_softplus(-(w0 + lw)) - 0.5
    logdecay = -jnp.exp(log_w)
    a = jax.nn.sigmoid(a0 + al)
    kk = k0 * k_k
    nrm = jnp.maximum(jnp.sqrt(_seg_sum(kk * kk, ones_bd)), 1e-12)
    kk = kk / nrm
    k = k0 * (1.0 + (a - 1.0) * k_a)
    res = [logdecay, k, kk, kk * a]
    if has_vres:
        res.append(v + (vfirst - v) * jax.nn.sigmoid(prm[4:5] + vl))
    return res


def _wkv_body(r_ref, lw_ref, k_ref, v_ref, kk_ref, b_ref, y_ref, *s_refs, chunk, heads):
    hd = RWKV_HEAD
    rows = r_ref.shape[0]
    width = heads * hd
    hr = heads * chunk
    nt = (((1,), (1,)), ((), ()))

    @pl.when(pl.program_id(2) == 0)
    def _():
        for s_ref in s_refs:
            s_ref[...] = jnp.zeros_like(s_ref)

    ti = lax.broadcasted_iota(jnp.int32, (chunk, chunk), 0)
    si = lax.broadcasted_iota(jnp.int32, (chunk, chunk), 1)
    tri = (ti >= si).astype(BF16)
    ri = lax.broadcasted_iota(jnp.int32, (hr, width), 0)
    ci = lax.broadcasted_iota(jnp.int32, (hr, width), 1)
    head_mask = (ri // chunk == ci // hd).astype(F32)
    rr = lax.broadcasted_iota(jnp.int32, (hr, hr), 0)
    cc = lax.broadcasted_iota(jnp.int32, (hr, hr), 1)
    strict = rr % chunk > cc % chunk
    eye = (rr == cc).astype(F32)
    tc = lax.broadcasted_iota(jnp.int32, (chunk, 2 * hr), 0)
    sc = lax.broadcasted_iota(jnp.int32, (chunk, 2 * hr), 1) % chunk
    incl_c = tc >= sc
    strict_c = (tc > sc)[:, :hr]

    def blocked(x):
        return (jnp.concatenate([x] * heads, 0) * head_mask).astype(BF16)

    def step(c, carry):
        sl = pl.ds(pl.multiple_of(c * chunk, chunk), chunk)
        gens = [group_step(sl, slice(grp * width, (grp + 1) * width), s_ref) for grp, s_ref in enumerate(s_refs)]
        for _ in itertools.zip_longest(*gens):
            pass
        return carry

    def group_step(sl, ln, s_ref):
        lw = lw_ref[sl, ln]
        cs = _split_dot_left(tri, lw)
        ctot = cs[chunk - 1:chunk, :]
        g_inv = jnp.exp(-cs)
        g_end = jnp.exp(ctot - cs)
        kk = kk_ref[sl, ln]
        kv = k_ref[sl, ln]
        bv = b_ref[sl, ln]
        ag = -kk * jnp.exp(cs - lw)
        bg = blocked(bv * g_inv)
        bk = jnp.concatenate([bg, blocked(kv * g_inv)], 0)
        ends = jnp.concatenate([blocked(bv * g_end), blocked(kv * g_end)], 0)
        vb = blocked(v_ref[sl, ln])
        s_old = s_ref[...]
        yield
        a_ab = jnp.where(strict, lax.dot_general(blocked(ag), bg, nt, preferred_element_type=F32), 0.0)
        ar = jnp.concatenate([ag, r_ref[sl, ln] * jnp.exp(cs)], 0).astype(BF16)
        yield
        m = lax.dot_general(ar, bk, nt, preferred_element_type=F32)
        a_ak = jnp.where(strict_c, m[:chunk, hr:], 0.0)
        a_r = jnp.where(incl_c, m[chunk:], 0.0)
        ars = lax.dot_general(ar, s_old.astype(BF16), nt, preferred_element_type=F32)
        yield
        rhs = blocked(ars[:chunk] + jnp.dot(a_ak.astype(BF16), vb, preferred_element_type=F32))
        inv = eye + a_ab
        apow = a_ab.astype(BF16)
        for _ in range(int(math.log2(chunk)) - 1):
            yield
            apow = jnp.dot(apow, apow, preferred_element_type=F32).astype(BF16)
            inv = inv + jnp.dot(inv.astype(BF16), apow, preferred_element_type=F32)
        yield
        u = jnp.dot(inv.astype(BF16), rhs, preferred_element_type=F32)
        uv = jnp.concatenate([u.astype(BF16), vb], 0)
        yield
        y_ref[sl, ln] = ars[chunk:] + jnp.dot(a_r.astype(BF16), uv, preferred_element_type=F32)
        s_ref[...] = s_old * jnp.exp(ctot) + lax.dot_general(
            uv, ends, (((0,), (0,)), ((), ())), preferred_element_type=F32)

    lax.fori_loop(0, rows // chunk, step, 0)


def _split_dot_left(m, x):
    xh = x.astype(BF16)
    xl = (x - xh.astype(F32)).astype(BF16)
    return jnp.dot(m, xh, preferred_element_type=F32) + jnp.dot(m, xl, preferred_element_type=F32)


def wkv7(r, logdecay, k, v, kk, b, *, batch, seq):
    n, d = r.shape
    chunk = min(WKV_CHUNK, seq)
    rows = min(WKV_ROWS_PER_STEP, seq)
    heads = WKV_HEADS_PER_STEP
    groups = WKV_GROUPS_PER_STEP
    width = heads * RWKV_HEAD
    spb = seq // rows
    spec = pl.BlockSpec((rows, groups * width), lambda bi, hi, ti: (bi * spb + ti, hi))
    return pl.pallas_call(
        functools.partial(_wkv_body, chunk=chunk, heads=heads),
        grid=(batch, d // (groups * width), spb),
        in_specs=[spec] * 6,
        out_specs=spec,
        out_shape=jax.ShapeDtypeStruct((n, d), F32),
        scratch_shapes=[pltpu.VMEM((width, width), F32)] * groups,
        compiler_params=_cparams("parallel", "parallel", "arbitrary"),
        name="wkv7",
    )(r, logdecay, k, v, kk, b)


def _rwkv_out_fn(y, r, k, v, g, prm, ones_bd):
    r_k, gn_g, gn_b = prm[0:1], prm[1:2], prm[2:3]
    inv_n = 1.0 / RWKV_HEAD
    m_y = _seg_sum(y, ones_bd) * inv_n
    yc = y - m_y
    v_y = _seg_sum(yc * yc, ones_bd) * inv_n
    yn = yc * lax.rsqrt(v_y + RWKV_GN_EPS) * gn_g + gn_b
    bonus = _seg_sum(r * k * r_k, ones_bd) * v
    return ((yn + bonus) * g,)


def rwkv_layer(x, sc, sh, v_first, p, *, batch, seq):
    n, d = x.shape
    xr, xw, xk, xv, xa, xg = rwkv_prep(x, sc, sh, p["mu"], seq=seq)
    r = mm(xr, p["w_r"])
    k0 = mm(xk, p["w_k"])
    v = mm(xv, p["w_v"])
    lw = mm(mm(xw, p["w1"], out_dtype=BF16, act="tanh"), p["w2"])
    al = mm(mm(xa, p["a1"], out_dtype=BF16), p["a2"])
    g = mm(mm(xg, p["g1"], out_dtype=BF16, act="sigmoid"), p["g2"])
    ones_bd = _head_ones()
    has_vres = p["v0"] is not None
    outs = [(d, F32)] * (5 if has_vres else 4)
    if has_vres:
        vl = mm(mm(xv, p["v1"], out_dtype=BF16), p["v2"])
        prm = jnp.stack([p["w0"], p["a0"], p["k_k"], p["k_a"], p["v0"]])
        rows = [k0, lw, al, v, v_first, vl]
    else:
        prm = jnp.stack([p["w0"], p["a0"], p["k_k"], p["k_a"]])
        rows = [k0, lw, al]
    res = rowwise(functools.partial(_rwkv_gates_fn, has_vres=has_vres), rows, [], [prm, ones_bd], outs,
                  seq=seq, tm=128, name="rwkv_gates")
    logdecay, k, kk, b = res[:4]
    if has_vres:
        v = res[4]
    else:
        v_first = v
    y = wkv7(r, logdecay, k, v, kk, b, batch=batch, seq=seq)
    prm2 = jnp.stack([p["r_k"], p["gn_g"], p["gn_b"]])
    yg = rowwise(_rwkv_out_fn, [y, r, k, v, g], [], [prm2, ones_bd], [(d, BF16)], seq=seq, tm=128, name="rwkv_out")[0]
    return mm(yg, p["w_o"]), v_first


SM_W0 = IDX_DIM
SM_R0 = LANES - DSA_ROPE_DIM
LAT_W = 768
LAT_PACK = LAT_W // 2
LAT_TILES = LAT_PACK // LANES
DSA_QUERIES_PER_ATTEND = 8


def _dsa_patterns():
    fq = ROPE_THETA ** (-np.arange(DSA_ROPE_DIM // 2, dtype=np.float32) * np.float32(2.0 / DSA_ROPE_DIM))
    fi = ROPE_THETA ** (-np.arange(IDX_ROPE_DIM // 2, dtype=np.float32) * np.float32(2.0 / IDX_ROPE_DIM))
    hq, hi = DSA_ROPE_DIM // 2, IDX_ROPE_DIM // 2
    pat = np.zeros((16, LANES), np.float32)
    lane = np.arange(LANES)
    pat[0, :hq] = fq; pat[0, hq:2 * hq] = fq
    pat[1, :hq] = -1.0; pat[2, hq:2 * hq] = 1.0
    l64 = lane % IDX_DIM
    pat[3] = np.where(l64 < hi, fi[np.minimum(l64, hi - 1)], np.where(l64 < 2 * hi, fi[np.clip(l64 - hi, 0, hi - 1)], 0.0))
    pat[4] = np.where(l64 < hi, -1.0, 0.0); pat[5] = np.where((l64 >= hi) & (l64 < 2 * hi), 1.0, 0.0)
    pat[6, :hi] = fi; pat[6, hi:2 * hi] = fi
    pat[6, SM_R0:SM_R0 + hq] = fq; pat[6, SM_R0 + hq:] = fq
    pat[7, :hi] = -1.0; pat[8, hi:2 * hi] = 1.0
    pat[9, SM_R0:SM_R0 + hq] = -1.0; pat[10, SM_R0 + hq:] = 1.0
    pat[11, :IDX_DIM] = 1.0
    pat[12, SM_W0:SM_W0 + IDX_HEADS] = IDX_HEADS ** -0.5 * IDX_DIM ** -0.5
    pat[13, SM_R0:] = 1.0
    return jnp.asarray(pat)


def _rot(x, shift):
    return pltpu.roll(x, shift % x.shape[1], 1)


def _tile_lanes(v, reps):
    return jnp.concatenate([v] * reps, 1) if reps > 1 else v


def _dsa_prep_fn(q, ckv, qidx, small, pos, kvn, lng, lnb, pat):
    posf = pos.astype(F32)
    hq, hi = DSA_ROPE_DIM // 2, IDX_ROPE_DIM // 2
    ang = posf * pat[0:1]
    c, s = jnp.cos(ang), jnp.sin(ang)
    reps = q.shape[1] // LANES
    qr = (q * _tile_lanes(c, reps) + _rot(q, -hq) * _tile_lanes(s * pat[1:2], reps)
          + _rot(q, hq) * _tile_lanes(s * pat[2:3], reps))
    ang = posf * pat[3:4]
    c, s = jnp.cos(ang), jnp.sin(ang)
    reps = qidx.shape[1] // LANES
    qi = (qidx * _tile_lanes(c, reps) + _rot(qidx, -hi) * _tile_lanes(s * pat[4:5], reps)
          + _rot(qidx, hi) * _tile_lanes(s * pat[5:6], reps))
    ckn = ckv * lax.rsqrt(jnp.mean(ckv * ckv, -1, keepdims=True) + 1e-6) * kvn
    mk = pat[11:12]
    inv = 1.0 / IDX_DIM
    mu = jnp.sum(small * mk, -1, keepdims=True) * inv
    dv = (small - mu) * mk
    var = jnp.sum(dv * dv, -1, keepdims=True) * inv
    y = (dv * lax.rsqrt(var + LN_EPS) * lng + lnb) * mk + small * pat[12:13] + small * pat[13:14]
    ang = posf * pat[6:7]
    c, s = jnp.cos(ang), jnp.sin(ang)
    so = (y * c + _rot(y, -hi) * (s * pat[7:8]) + _rot(y, hi) * (s * pat[8:9])
          + _rot(y, -hq) * (s * pat[9:10]) + _rot(y, hq) * (s * pat[10:11]))
    lat = jnp.concatenate([ckn, so * pat[13:14], jnp.zeros_like(so)], 1)
    lo = lax.bitcast_convert_type(lat[:, :LAT_PACK].astype(BF16).astype(F32), jnp.uint32) >> 16
    hi_b = lax.bitcast_convert_type(lat[:, LAT_PACK:].astype(BF16).astype(F32), jnp.uint32) & jnp.uint32(0xFFFF0000)
    return qr, lo | hi_b, qi, so, so


def _sortable(x):
    b = lax.bitcast_convert_type(x, jnp.int32)
    return jnp.where(b < 0, b ^ jnp.int32(0x7FFFFFFF), b)


def _dsa_select_body(qi_ref, sm_ref, kx_ref, sel_ref, key_ref, ib_ref, *, tq, ts, topk, seq):
    sub = 32
    t0 = pl.program_id(1) * tq
    nchunk = (t0 + tq + ts - 1) // ts
    qpos = t0 + lax.broadcasted_iota(jnp.int32, (tq, 1), 0)
    w = sm_ref[:, SM_W0:SM_W0 + IDX_HEADS]
    neg = jnp.float32(-jnp.inf)

    def score_chunk(c, carry):
        off = pl.multiple_of(c * ts, ts)
        ks = kx_ref[pl.ds(off, ts), :][:, :IDX_DIM]
        acc = jnp.zeros((tq, ts), F32)
        for h in range(IDX_HEADS):
            s = lax.dot_general(qi_ref[:, h * IDX_DIM:(h + 1) * IDX_DIM], ks, (((1,), (1,)), ((), ())),
                                preferred_element_type=F32)
            acc = acc + w[:, h:h + 1] * jnp.maximum(s, 0.0)
        kpos = off + lax.broadcasted_iota(jnp.int32, (1, ts), 1)
        key_ref[:, pl.ds(off, ts)] = _sortable(jnp.where(kpos <= qpos, acc, neg))
        return carry

    lax.fori_loop(0, nchunk, score_chunk, 0)

    tpc = ts // LANES
    lane_iota = lax.broadcasted_iota(jnp.int32, (1, LANES), 1)

    def count(pred):
        def body(c, acc):
            for u in range(tpc):
                off = pl.multiple_of(c * ts + u * LANES, LANES)
                acc = acc + pred(key_ref[:, pl.ds(off, LANES)], off + lane_iota).astype(jnp.int32)
            return acc
        acc = lax.fori_loop(0, nchunk, body, jnp.zeros((tq, LANES), jnp.int32))
        return jnp.sum(acc, -1, keepdims=True)

    sign = jnp.int32(-2 ** 31)
    int_min = jnp.full((tq, LANES), -2 ** 31, jnp.int32)

    def top2(c, carry):
        m1, m2 = carry
        for u in range(tpc):
            k = key_ref[:, pl.ds(pl.multiple_of(c * ts + u * LANES, LANES), LANES)]
            m2 = jnp.maximum(m2, jnp.minimum(m1, k))
            m1 = jnp.maximum(m1, k)
        return m1, m2

    m1, m2 = lax.fori_loop(0, nchunk, top2, (int_min, int_min))
    lo_u = jnp.min(m2, -1, keepdims=True) ^ sign
    hi_u = jnp.max(m1, -1, keepdims=True) ^ sign
    differ = jnp.max(lax.shift_right_logical(lo_u ^ hi_u, 1))
    for sh in (1, 2, 4, 8, 16):
        differ = differ | (differ >> sh)
    open_bits = (differ << 1) | 1
    ib_ref[...] = lo_u & ~open_bits

    for bit in range(31, -1, -1):
        @pl.when(((open_bits >> bit) & 1) == 1)
        def _(bit=bit):
            thr_u = ib_ref[...]
            cand_u = thr_u | jnp.int32(np.uint32(1 << bit).astype(np.int32))
            cand = cand_u ^ sign
            cnt = count(lambda k, kp, cand=cand: k >= cand)
            ib_ref[...] = jnp.where(cnt >= topk, cand_u, thr_u)
    thr = ib_ref[...] ^ sign
    need = topk - count(lambda k, kp: k > thr)
    n_eq = count(lambda k, kp: k == thr)

    ib_ref[...] = jnp.full((tq, 1), seq, jnp.int32)

    @pl.when(jnp.max(n_eq - need) > 0)
    def _():
        ib = jnp.zeros((tq, 1), jnp.int32)
        for bit in range(int(math.log2(seq)) - 1, -1, -1):
            cand = ib | jnp.int32(1 << bit)
            cnt = count(lambda k, kp, cand=cand: (k == thr) & (kp < cand))
            ib = jnp.where(cnt < need, cand, ib)
        ib_ref[...] = ib

    ib = ib_ref[...]

    def mask_tiles(c, acc):
        for u in range(tpc):
            off = pl.multiple_of(c * ts + u * LANES, LANES)
            kpos = off + lane_iota
            k = key_ref[:, pl.ds(off, LANES)]
            m = (((k > thr) | ((k == thr) & (kpos <= ib))) & (kpos <= qpos)).astype(jnp.int32)
            key_ref[:, pl.ds(off, LANES)] = m
            acc = acc + m
        return acc

    colcnt = lax.fori_loop(0, nchunk, mask_tiles, jnp.zeros((tq, LANES), jnp.int32))
    li = lax.broadcasted_iota(jnp.int32, (LANES, LANES), 0)
    lj = lax.broadcasted_iota(jnp.int32, (LANES, LANES), 1)
    cin = jnp.dot(colcnt.astype(BF16), (li <= lj).astype(BF16), preferred_element_type=F32).astype(jnp.int32)
    total = cin[:, LANES - 1:LANES]
    slot = lax.broadcasted_iota(jnp.int32, (tq, topk), 1)
    lane_of = jnp.zeros((tq, topk), jnp.int32)
    base = jnp.zeros((tq, topk), jnp.int32)
    for l in range(LANES):
        c = cin[:, l:l + 1]
        le = c <= slot
        lane_of = lane_of + le.astype(jnp.int32)
        base = jnp.maximum(base, jnp.where(le, c, 0))
    lane_of = jnp.minimum(lane_of, LANES - 1)
    rank = slot - base
    nhalf = topk // LANES
    for r0 in range(0, tq, sub):
        rows = slice(r0, r0 + sub)
        lanes_h = [lane_of[rows, i * LANES:(i + 1) * LANES] for i in range(nhalf)]
        ranks_h = [rank[rows, i * LANES:(i + 1) * LANES] for i in range(nhalf)]

        def tile_body(c, carry):
            carry = list(carry)
            for u in range(tpc):
                off = pl.multiple_of(c * ts + u * LANES, LANES)
                carry[0] = carry[0] + key_ref[r0:r0 + sub, pl.ds(off, LANES)]
                for i in range(nhalf):
                    got = jnp.take_along_axis(carry[0], lanes_h[i], axis=1)
                    carry[1 + i] = carry[1 + i] + (got <= ranks_h[i]).astype(jnp.int32)
            return tuple(carry)

        res = lax.fori_loop(0, nchunk, tile_body, (jnp.zeros((sub, LANES), jnp.int32),) * (1 + nhalf))
        pos = jnp.concatenate([res[1 + i] * LANES + lanes_h[i] for i in range(nhalf)], 1)
        slot_sub = lax.broadcasted_iota(jnp.int32, (sub, topk), 1)
        sel_ref[rows, :] = jnp.where(slot_sub < total[rows], pos, -1)


def dsa_select(qi, small, kx, *, batch, seq):
    n = qi.shape[0]
    topk = min(TOPK_MAX, seq // 4)
    tq = min(128, seq)
    ts = min(512, seq)
    qpb = seq // tq
    assert topk <= 2 * LANES
    return pl.pallas_call(
        functools.partial(_dsa_select_body, tq=tq, ts=ts, topk=topk, seq=seq),
        grid=(batch, qpb),
        in_specs=[pl.BlockSpec((tq, qi.shape[1]), lambda b, i: (b * qpb + i, 0)),
                  pl.BlockSpec((tq, LANES), lambda b, i: (b * qpb + i, 0)),
                  pl.BlockSpec((seq, LANES), lambda b, i: (b, 0))],
        out_specs=pl.BlockSpec((tq, topk), lambda b, i: (b * qpb + i, 0)),
        out_shape=jax.ShapeDtypeStruct((n, topk), jnp.int32),
        scratch_shapes=[pltpu.VMEM((tq, seq), jnp.int32), pltpu.VMEM((tq, 1), jnp.int32)],
        compiler_params=_cparams("parallel", "arbitrary"),
        name="dsa_select",
    )(qi, small, kx)


def _dsa_attn_body(sel_s, selv_ref, q2_ref, tbl_ref, o_ref, *stages, tqa, topk):
    heads = DSA_HEADS
    scale = DSA_HEAD_DIM ** -0.5
    lt = LAT_TILES

    def gather(i, buf):
        base = i * topk
        for j in range(topk):
            stages[buf][pl.ds(j * lt, lt), :] = tbl_ref[pl.ds(sel_s[base + j], lt), :]

    def attend(i0, bufs):
        qs = range(len(bufs))
        pk = [jnp.concatenate([stages[b][pl.ds(t, topk, stride=lt), :] for t in range(lt)], 1) for b in bufs]
        kv = [jnp.concatenate([lax.bitcast_convert_type(x << 16, F32),
                               lax.bitcast_convert_type(x & jnp.uint32(0xFFFF0000), F32)], 1).astype(BF16)
              for x in pk]
        rows = [pl.ds(pl.multiple_of((i0 + a) * heads, heads), heads) for a in qs]
        s = [lax.dot_general(q2_ref[rows[a], :], kv[a], (((1,), (1,)), ((), ())), preferred_element_type=F32) * scale
             for a in qs]
        s = [jnp.where(selv_ref[pl.ds(i0 + a, 1), :] >= 0, s[a], -jnp.inf) for a in qs]
        e = [jnp.exp(s[a] - jnp.max(s[a], -1, keepdims=True)) for a in qs]
        p = [(e[a] / jnp.sum(e[a], -1, keepdims=True)).astype(BF16) for a in qs]
        o = [jnp.dot(p[a], kv[a][:, :DSA_KV_RANK], preferred_element_type=F32) for a in qs]
        for a in qs:
            o_ref[rows[a], :] = o[a].astype(o_ref.dtype)

    nq = len(stages) // 2
    first, second = tuple(range(nq)), tuple(range(nq, 2 * nq))
    for a in first:
        gather(a, a)

    def body(pi, carry):
        i0 = 2 * nq * pi
        for a in first:
            gather(i0 + nq + a, second[a])
        attend(i0, first)
        for a in first:
            gather(jnp.minimum(i0 + 2 * nq + a, tqa - 1), a)
        attend(i0 + nq, second)
        return carry

    lax.fori_loop(0, tqa // (2 * nq), body, 0)


def dsa_attend(sel, q2, table, *, batch, seq):
    n, topk = sel.shape
    tqa = min(64, seq)
    qpb = seq // tqa
    heads = DSA_HEADS
    return pl.pallas_call(
        functools.partial(_dsa_attn_body, tqa=tqa, topk=topk),
        grid=(batch, qpb),
        in_specs=[pl.BlockSpec((tqa * topk,), lambda b, i: (b * qpb + i,), memory_space=pltpu.SMEM),
                  pl.BlockSpec((tqa, topk), lambda b, i: (b * qpb + i, 0)),
                  pl.BlockSpec((tqa * heads, LAT_W), lambda b, i: (b * qpb + i, 0)),
                  pl.BlockSpec((seq * LAT_TILES, LANES), lambda b, i: (b, 0), pipeline_mode=pl.Buffered(1))],
        out_specs=pl.BlockSpec((tqa * heads, DSA_KV_RANK), lambda b, i: (b * qpb + i, 0)),
        out_shape=jax.ShapeDtypeStruct((n * heads, DSA_KV_RANK), BF16),
        scratch_shapes=[pltpu.VMEM((topk * LAT_TILES, LANES), jnp.uint32)] * (2 * DSA_QUERIES_PER_ATTEND),
        compiler_params=_cparams("parallel", "arbitrary"),
        name="dsa_attend",
    )((jnp.maximum(sel, 0) * LAT_TILES).reshape(-1), sel, q2, table.reshape(n * LAT_TILES, LANES))


def _head_mm_body(a_ref, w_ref, o_ref):
    o_ref[0] = jnp.dot(a_ref[...].astype(BF16), w_ref[0], preferred_element_type=F32).astype(o_ref.dtype)


def head_mm_out(a, w, *, tm=512):
    n = a.shape[0]
    heads, k, m = w.shape
    tm = min(tm, n)
    return pl.pallas_call(
        _head_mm_body,
        grid=(n // tm, heads),
        in_specs=[pl.BlockSpec((tm, k), lambda i, h: (i, h)), pl.BlockSpec((1, k, m), lambda i, h: (h, 0, 0))],
        out_specs=pl.BlockSpec((1, tm, m), lambda i, h: (h, i, 0)),
        out_shape=jax.ShapeDtypeStruct((heads, n, m), BF16),
        compiler_params=_cparams("parallel", "parallel"),
        name="head_mm_out",
    )(a, w)


def _head_mm_in_body(a_ref, w_ref, o_ref):
    o_ref[...] = jnp.dot(a_ref[0], w_ref[0], preferred_element_type=F32).astype(o_ref.dtype)


def head_mm_in(a, w, *, tm=512):
    heads, n, k = a.shape
    m = w.shape[2]
    tm = min(tm, n)
    return pl.pallas_call(
        _head_mm_in_body,
        grid=(n // tm, heads),
        in_specs=[pl.BlockSpec((1, tm, k), lambda i, h: (h, i, 0)), pl.BlockSpec((1, k, m), lambda i, h: (h, 0, 0))],
        out_specs=pl.BlockSpec((tm, m), lambda i, h: (i, h)),
        out_shape=jax.ShapeDtypeStruct((n, heads * m), BF16),
        compiler_params=_cparams("parallel", "parallel"),
        name="head_mm_in",
    )(a, w)


def dsa_layer(hbf, positions, p, *, batch, seq):
    n = hbf.shape[0]
    heads = DSA_HEADS
    q = mm(hbf, p["w_q"])
    ckv = mm(hbf, p["w_ckv"])
    qidx = mm(hbf, p["w_qidx"])
    small = mm(hbf, p["w_small"])
    qr, table, qi, small_o, kx = rowwise(
        _dsa_prep_fn, [q, ckv, qidx, small, positions.reshape(n, 1)], [],
        [p["kv_norm"], p["idx_ln_g"], p["idx_ln_b"], _dsa_patterns()],
        [(q.shape[1], BF16), (LAT_PACK, jnp.uint32), (qidx.shape[1], BF16), (LANES, F32), (LANES, BF16)],
        seq=seq, tm=128, name="dsa_prep")
    sel = dsa_select(qi, small_o, kx, batch=batch, seq=seq)
    q2 = head_mm_out(qr, p["w_q2"])
    q2 = q2.transpose(1, 0, 2).reshape(n * heads, LAT_W)
    o_lat = dsa_attend(sel, q2, table, batch=batch, seq=seq)
    o_lat = o_lat.reshape(n, heads, DSA_KV_RANK).transpose(1, 0, 2)
    out = head_mm_in(o_lat, p["w_uv"])
    return mm(out, p["w_o"])


def _s5_tables(p):
    hp = lax.Precision.HIGHEST
    lc, hg = S5_CHUNK, S5_GROUP
    lr, li = p["lam_re"], p["lam_im"]
    g, ps = lr.shape
    dt = jnp.exp(p["log_dt"])[:, None]
    mag = jnp.exp(lr * dt)
    ab_re, ab_im = mag * jnp.cos(li * dt), mag * jnp.sin(li * dt)
    den = lr * lr + li * li
    nr, ni = ab_re - 1.0, ab_im
    coef_re = (nr * lr + ni * li) / den
    coef_im = (ni * lr - nr * li) / den
    bb_re = coef_re[..., None] * p["b_re"] - coef_im[..., None] * p["b_im"]
    bb_im = coef_re[..., None] * p["b_im"] + coef_im[..., None] * p["b_re"]
    tau = jnp.arange(lc + 1, dtype=F32)[:, None, None]
    pmag = jnp.exp(lr * dt * tau)
    pr, pi = pmag * jnp.cos(li * dt * tau), pmag * jnp.sin(li * dt * tau)
    cr, ci = p["c_re"][None], p["c_im"][None]
    car = cr * pr[:, :, None, :] - ci * pi[:, :, None, :]
    cai = cr * pi[:, :, None, :] + ci * pr[:, :, None, :]
    kern = (jnp.einsum('tghp,gpk->tghk', car[:lc], bb_re, precision=hp)
            - jnp.einsum('tghp,gpk->tghk', cai[:lc], bb_im, precision=hp))
    kz = jnp.concatenate([kern, jnp.zeros((1,) + kern.shape[1:], F32)], 0)
    s_i = np.arange(lc)[:, None]
    t_i = np.arange(lc)[None, :]
    m = kz[np.where(t_i >= s_i, t_i - s_i, lc)]
    m = m.transpose(2, 0, 4, 1, 3).reshape(g, lc * hg, lc * hg)
    skip = jnp.tile(p["d"].reshape(g, 1, hg), (1, lc, 1)).reshape(g, 1, lc * hg)
    m = m + skip * jnp.eye(lc * hg, dtype=F32)[None]
    prs, pis = pr[lc - 1 - np.arange(lc)], pi[lc - 1 - np.arange(lc)]
    wre = prs[..., None] * bb_re[None] - pis[..., None] * bb_im[None]
    wim = prs[..., None] * bb_im[None] + pis[..., None] * bb_re[None]
    wre = wre.transpose(1, 0, 3, 2).reshape(g, lc * hg, ps)
    wim = wim.transpose(1, 0, 3, 2).reshape(g, lc * hg, ps)
    vre = car[1:].transpose(1, 3, 0, 2).reshape(g, ps, lc * hg)
    vim = (-cai[1:]).transpose(1, 3, 0, 2).reshape(g, ps, lc * hg)
    gu = S5_UNIT
    eye_u = jnp.eye(gu, dtype=F32)

    def unit(z, row_t, col_t):
        r, c = z.shape[1:]
        zz = z.reshape(g // gu, gu, r, 1, c) * eye_u[None, :, None, :, None]
        if row_t:
            zz = zz.reshape(g // gu, gu, lc, hg, gu, c).transpose(0, 2, 1, 3, 4, 5).reshape(g // gu, gu * r, gu, c)
        else:
            zz = zz.reshape(g // gu, gu * r, gu, c)
        if col_t:
            zz = zz.reshape(g // gu, gu * r, gu, lc, hg).transpose(0, 1, 3, 2, 4)
        return zz.reshape(g // gu, gu * r, gu * c).astype(BF16)

    are = pr[lc].reshape(g // gu, 1, gu * ps)
    aim = pi[lc].reshape(g // gu, 1, gu * ps)
    return (unit(m, True, True), unit(wre, True, False), unit(wim, True, False),
            unit(vre, False, True), unit(vim, False, True), are, aim)


def _gelu_tanh(y):
    return 0.5 * y * (1.0 + jnp.tanh(math.sqrt(2.0 / math.pi) * (y + 0.044715 * (y * y * y))))


def _s5_body(u_ref, m_ref, wre_ref, wim_ref, vre_ref, vim_ref, are_ref, aim_ref, z_ref, xre, xim, sre, sim, *, pairs):
    nc = u_ref.shape[1]
    for q in range(pairs):
        u = u_ref[q]
        xre[q] = jnp.dot(u, wre_ref[q], preferred_element_type=F32)
        xim[q] = jnp.dot(u, wim_ref[q], preferred_element_type=F32)
    ar = [are_ref[q] for q in range(pairs)]
    ai = [aim_ref[q] for q in range(pairs)]

    def step(c, carry):
        new = []
        row = pl.ds(c, 1)
        for q in range(pairs):
            re, im = carry[2 * q], carry[2 * q + 1]
            sre[q, row, :] = re
            sim[q, row, :] = im
            new.append(ar[q] * re - ai[q] * im + xre[q, row, :])
            new.append(ar[q] * im + ai[q] * re + xim[q, row, :])
        return tuple(new)

    zero = jnp.zeros((1, are_ref.shape[-1]), F32)
    lax.fori_loop(0, nc, step, (zero,) * (2 * pairs), unroll=8)
    rb = min(256, nc)
    for q in range(pairs):
        def out_block(i, carry, q=q):
            rs = pl.ds(pl.multiple_of(i * rb, rb), rb)
            y = (jnp.dot(u_ref[q, rs, :], m_ref[q], preferred_element_type=F32)
                 + jnp.dot(sre[q, rs, :].astype(BF16), vre_ref[q], preferred_element_type=F32)
                 + jnp.dot(sim[q, rs, :].astype(BF16), vim_ref[q], preferred_element_type=F32))
            z_ref[q, rs, :] = _gelu_tanh(y).astype(z_ref.dtype)
            return carry

        lax.fori_loop(0, nc // rb, out_block, 0)


def s5_layer(hbf, p, *, batch, seq):
    n, D = hbf.shape
    lc, hg = S5_CHUNK, S5_GROUP
    gu = S5_UNIT
    g2 = D // hg // gu
    pw = gu * lc * hg
    tabs = _s5_tables(p)
    nc = seq // lc
    u = hbf.reshape(n // lc, lc, g2, gu * hg).transpose(2, 0, 1, 3).reshape(g2, n // lc, pw)
    pairs = 1
    sw = gu * S5_STATE
    wspec = lambda r, c: pl.BlockSpec((pairs, r, c), lambda i, b: (i, 0, 0), pipeline_mode=pl.Buffered(1))
    z = pl.pallas_call(
        functools.partial(_s5_body, pairs=pairs),
        grid=(g2 // pairs, batch),
        in_specs=[pl.BlockSpec((pairs, nc, pw), lambda i, b: (i, b, 0)),
                  wspec(pw, pw), wspec(pw, sw), wspec(pw, sw), wspec(sw, pw), wspec(sw, pw),
                  wspec(1, sw), wspec(1, sw)],
        out_specs=pl.BlockSpec((pairs, nc, pw), lambda i, b: (i, b, 0)),
        out_shape=jax.ShapeDtypeStruct((g2, n // lc, pw), BF16),
        scratch_shapes=[pltpu.VMEM((pairs, nc, sw), F32)] * 4,
        compiler_params=_cparams("parallel", "parallel"),
        name="s5_scan",
    )(u, *tabs)
    z = z.reshape(g2, n // lc, lc, gu * hg).transpose(1, 2, 0, 3).reshape(n, D)
    zz = mm(z, p["w_glu"])
    return rowwise(lambda a: (a[:, :D] * jax.nn.sigmoid(a[:, D:]),), [zz], [], [], [(D, F32)], seq=seq, tm=256,
                   name="s5_glu")[0]


def _modulate(x, sc, sh, dtype, *, seq):
    d = x.shape[1]
    return rowwise(lambda xv, s, h: (xv * (1.0 + s) + h,), [x], [sc, sh], [], [(d, dtype)], seq=seq, tm=256,
                   name="modulate")[0]


def kernel(x, c, positions, ada_w, ada_b, ln_g, ln_b, router_w, router_bias, moe_w_gate, moe_w_up, moe_w_down,
           rwkv_mu, rwkv_w_rkv, rwkv_w_o, rwkv_w0, rwkv_w1, rwkv_w2, rwkv_a0, rwkv_a1, rwkv_a2, rwkv_g1, rwkv_g2,
           rwkv_k_k, rwkv_k_a, rwkv_r_k, rwkv_gn_g, rwkv_gn_b, rwkv_v0, rwkv_v1, rwkv_v2, dsa_w_in, dsa_kv_norm,
           dsa_w_uk, dsa_w_uv, dsa_idx_ln_g, dsa_idx_ln_b, dsa_w_o, s5_lam_re, s5_lam_im, s5_log_dt, s5_b_re,
           s5_b_im, s5_c_re, s5_c_im, s5_d, s5_w_glu):
    batch, seq, d = x.shape
    depth = ada_w.shape[0]
    n = batch * seq
    xf = x.reshape(n, d)
    mod = ada_mod(c, ada_w, ada_b)[:, :batch].reshape(depth, batch, 1, 6, d)
    v_first = None
    for i in range(depth):
        kind, j = i % N_MIXERS, i // N_MIXERS
        sh1, sc1, g1, sh2, sc2, g2 = (mod[i, :, :, m] for m in range(6))
        if kind == 0:
            p = dict(mu=rwkv_mu[j], w_r=rwkv_w_rkv[j, 0].astype(BF16), w_k=rwkv_w_rkv[j, 1].astype(BF16),
                     w_v=rwkv_w_rkv[j, 2].astype(BF16), w_o=rwkv_w_o[j].astype(BF16),
                     w0=rwkv_w0[j], w1=_wpad(rwkv_w1[j]), w2=_kpad(rwkv_w2[j]),
                     a0=rwkv_a0[j], a1=_wpad(rwkv_a1[j]), a2=_kpad(rwkv_a2[j]),
                     g1=_wpad(rwkv_g1[j]), g2=_kpad(rwkv_g2[j]),
                     k_k=rwkv_k_k[j], k_a=rwkv_k_a[j], r_k=rwkv_r_k[j], gn_g=rwkv_gn_g[j], gn_b=rwkv_gn_b[j],
                     v0=None)
            if j > 0:
                p.update(v0=rwkv_v0[j - 1], v1=_wpad(rwkv_v1[j - 1]), v2=_kpad(rwkv_v2[j - 1]))
            y, v_first = rwkv_layer(xf, sc1, sh1, v_first, p, batch=batch, seq=seq)
        elif kind == 1:
            w_in = dsa_w_in[j]
            hq = DSA_HEADS * DSA_HEAD_DIM
            o1 = hq + DSA_KV_RANK
            o2 = o1 + DSA_ROPE_DIM
            o3 = o2 + IDX_HEADS * IDX_DIM
            o4 = o3 + IDX_DIM
            gap = jnp.zeros((d, SM_R0 - SM_W0 - IDX_HEADS), F32)
            w_small = jnp.concatenate([w_in[:, o3:o4], w_in[:, o4:], gap, w_in[:, o1:o2]], axis=1)
            rope_pass = jnp.zeros((DSA_ROPE_DIM, LAT_W), F32).at[
                jnp.arange(DSA_ROPE_DIM), DSA_KV_RANK + SM_R0 + jnp.arange(DSA_ROPE_DIM)].set(1.0)
            w_uk_pad = jnp.pad(dsa_w_uk[j], ((0, 0), (0, 0), (0, LAT_W - DSA_KV_RANK)))
            w_q2 = jnp.concatenate([jnp.broadcast_to(rope_pass, (DSA_HEADS,) + rope_pass.shape), w_uk_pad], axis=1)
            lane_pad = lambda v: jnp.pad(v, (0, LANES - v.shape[0])).reshape(1, LANES)
            p = dict(w_q=w_in[:, :hq].astype(BF16), w_ckv=w_in[:, hq:o1].astype(BF16),
                     w_qidx=w_in[:, o2:o3].astype(BF16), w_small=w_small.astype(BF16),
                     kv_norm=dsa_kv_norm[j].reshape(1, -1), w_q2=w_q2.astype(BF16), w_uv=dsa_w_uv[j].astype(BF16),
                     idx_ln_g=lane_pad(dsa_idx_ln_g[j]), idx_ln_b=lane_pad(dsa_idx_ln_b[j]),
                     w_o=dsa_w_o[j].astype(BF16))
            hbf = _modulate(xf, sc1, sh1, BF16, seq=seq)
            y = dsa_layer(hbf, positions, p, batch=batch, seq=seq)
        else:
            p = dict(lam_re=s5_lam_re[j], lam_im=s5_lam_im[j], log_dt=s5_log_dt[j], b_re=s5_b_re[j],
                     b_im=s5_b_im[j], c_re=s5_c_re[j], c_im=s5_c_im[j], d=s5_d[j], w_glu=s5_w_glu[j].astype(BF16))
            hbf = _modulate(xf, sc1, sh1, BF16, seq=seq)
            y = s5_layer(hbf, p, batch=batch, seq=seq)
        xf, hbf = post_norm_mix(xf, y, g1, sc2, sh2, ln_g[i, 0:1], ln_b[i, 0:1], seq=seq)
        y0, y1, gates = moe_ffn(xf, hbf, sc2, sh2, router_w, router_bias, moe_w_gate[i].astype(BF16),
                                moe_w_up[i].astype(BF16), moe_w_down[i].astype(BF16), seq=seq)
        xf = post_norm_moe(xf, y0, y1, gates, g2, ln_g[i, 1:2], ln_b[i, 1:2], seq=seq)
    return xf.reshape(batch, seq, d)
```

```python
import functools
import itertools
import math

import numpy as np
import jax
import jax.numpy as jnp
from jax import lax
from jax.experimental import pallas as pl
from jax.experimental.pallas import tpu as pltpu

F32, BF16 = jnp.float32, jnp.bfloat16

DEPTH = 4
N_MIXERS = 3
RWKV_HEAD = 64
RWKV_GN_EPS = 64e-5
DSA_HEADS = 16
DSA_HEAD_DIM = 128
DSA_ROPE_DIM = 32
DSA_NOPE_DIM = 96
DSA_V_DIM = 128
DSA_KV_RANK = 512
IDX_HEADS = 16
IDX_DIM = 64
IDX_ROPE_DIM = 16
TOPK_MAX = 256
Q_BLOCK = 128
S5_GROUP = 16
S5_STATE = 64
S5_CHUNK = 16
S5_UNIT = 8
N_EXPERTS = 32
N_GROUPS = 4
EXPERTS_PER_GROUP = 8
TOP_K = 2
ROPE_THETA = 500000.0
LN_EPS = 1e-5
DEEPNORM_ALPHA = (2 * DEPTH) ** 0.25

LANES = 128
VMEM_LIMIT = 48 * 1024 * 1024
MOE_ROWS = 256
WKV_CHUNK = 64
WKV_HEADS_PER_STEP = 4
WKV_GROUPS_PER_STEP = 8
WKV_ROWS_PER_STEP = 256


def _cparams(*sem):
    return pltpu.CompilerParams(dimension_semantics=sem, vmem_limit_bytes=VMEM_LIMIT)


def _pick(n, cands):
    for c in cands:
        if n % c == 0:
            return c
    return n


def _mm_body(a_ref, w_ref, o_ref, *, act):
    acc = jnp.dot(a_ref[...].astype(BF16), w_ref[...], preferred_element_type=F32)
    if act == "tanh":
        acc = jnp.tanh(acc)
    elif act == "sigmoid":
        acc = jax.nn.sigmoid(acc)
    o_ref[...] = acc.astype(o_ref.dtype)


def mm(a, w, *, out_dtype=F32, act=None, tm=512):
    m, k = a.shape
    n = w.shape[1]
    tm = _pick(m, (tm, 256, 128, 64, 32, 16, 8))
    tn = _pick(n, (512, 256, 128))
    return pl.pallas_call(
        functools.partial(_mm_body, act=act),
        grid=(m // tm, n // tn),
        in_specs=[pl.BlockSpec((tm, k), lambda i, j: (i, 0)), pl.BlockSpec((k, tn), lambda i, j: (0, j))],
        out_specs=pl.BlockSpec((tm, tn), lambda i, j: (i, j)),
        out_shape=jax.ShapeDtypeStruct((m, n), out_dtype),
        compiler_params=_cparams("parallel", "parallel"),
        name="mm",
    )(a, w)


def _wpad(w):
    n = w.shape[1]
    npad = -(-n // LANES) * LANES
    w = w.astype(BF16)
    return w if npad == n else jnp.pad(w, ((0, 0), (0, npad - n)))


def _kpad(w):
    k = w.shape[0]
    kpad = -(-k // LANES) * LANES
    w = w.astype(BF16)
    return w if kpad == k else jnp.pad(w, ((0, kpad - k), (0, 0)))


def rowwise(fn, rows, perbatch, consts, outs, *, seq, tm, name):
    n = rows[0].shape[0]
    tm = min(tm, seq)
    tpb = seq // tm
    nr, nb, nc = len(rows), len(perbatch), len(consts)

    def body(*refs):
        vals = [r[...] for r in refs[:nr]]
        vals += [r[0] for r in refs[nr:nr + nb]]
        vals += [r[...] for r in refs[nr + nb:nr + nb + nc]]
        res = fn(*vals)
        for o, v in zip(refs[nr + nb + nc:], res):
            o[...] = v.astype(o.dtype)

    in_specs = [pl.BlockSpec((tm, r.shape[1]), lambda i: (i, 0)) for r in rows]
    in_specs += [pl.BlockSpec((1, 1, p.shape[-1]), lambda i: (i // tpb, 0, 0)) for p in perbatch]
    in_specs += [pl.BlockSpec(c.shape, lambda i: (0, 0)) for c in consts]
    res = pl.pallas_call(
        body,
        grid=(n // tm,),
        in_specs=in_specs,
        out_specs=[pl.BlockSpec((tm, w), lambda i: (i, 0)) for w, _ in outs],
        out_shape=[jax.ShapeDtypeStruct((n, w), dt) for w, dt in outs],
        compiler_params=_cparams("parallel"),
        name=name,
    )(*rows, *perbatch, *consts)
    return res


def _layer_norm(z, g, b):
    mu = jnp.mean(z, -1, keepdims=True)
    zc = z - mu
    var = jnp.mean(zc * zc, -1, keepdims=True)
    return zc * lax.rsqrt(var + LN_EPS) * g + b


def _split_dot(x, m):
    xh = x.astype(BF16)
    xl = (x - xh.astype(F32)).astype(BF16)
    return jnp.dot(xh, m, preferred_element_type=F32) + jnp.dot(xl, m, preferred_element_type=F32)


def _seg_sum(x, ones_bd):
    parts = [_split_dot(x[:, j:j + LANES], ones_bd) for j in range(0, x.shape[1], LANES)]
    return jnp.concatenate(parts, -1)


def _head_ones():
    i = np.arange(LANES)
    return jnp.asarray((i[:, None] // RWKV_HEAD) == (i[None, :] // RWKV_HEAD), BF16)


def _ada_body(c_ref, w_ref, b_ref, o_ref):
    acc = jnp.dot(c_ref[...], w_ref[0], preferred_element_type=F32, precision=lax.Precision.HIGHEST)
    o_ref[0] = acc + b_ref[0]


def ada_mod(c, ada_w, ada_b):
    depth, d, n = ada_w.shape
    b = c.shape[0]
    cp = jnp.pad(c, ((0, 8 - b), (0, 0)))
    tn = 1024
    return pl.pallas_call(
        _ada_body,
        grid=(depth, n // tn),
        in_specs=[pl.BlockSpec((8, d), lambda i, j: (0, 0)),
                  pl.BlockSpec((1, d, tn), lambda i, j: (i, 0, j)),
                  pl.BlockSpec((1, 1, tn), lambda i, j: (i, 0, j))],
        out_specs=pl.BlockSpec((1, 8, tn), lambda i, j: (i, 0, j)),
        out_shape=jax.ShapeDtypeStruct((depth, 8, n), F32),
        compiler_params=_cparams("parallel", "parallel"),
        name="ada_mod",
    )(cp, ada_w, ada_b.reshape(depth, 1, n))


def post_norm_mix(x, y, gate, sc, sh, lng, lnb, *, seq):
    d = x.shape[1]

    def fn(xv, yv, g, s, h, lg, lb):
        xn = _layer_norm(DEEPNORM_ALPHA * xv + (1.0 + g) * yv, lg, lb)
        return xn, xn * (1.0 + s) + h

    return rowwise(fn, [x, y], [gate, sc, sh], [lng, lnb], [(d, F32), (d, BF16)], seq=seq, tm=256, name="post_norm_mix")


def post_norm_moe(x, y0, y1, gates, gate, lng, lnb, *, seq):
    d = x.shape[1]

    def fn(xv, a, b, gt, g, lg, lb):
        y = a * gt[:, 0:1] + b * gt[:, 1:2]
        return (_layer_norm(DEEPNORM_ALPHA * xv + (1.0 + g) * y, lg, lb),)

    return rowwise(fn, [x, y0, y1, gates], [gate], [lng, lnb], [(d, F32)], seq=seq, tm=256, name="post_norm_moe")[0]


def _router_fn(xv, s, h, rw, rb):
    hf = xv * (1.0 + s) + h
    logits = jnp.dot(hf, rw, preferred_element_type=F32, precision=lax.Precision.HIGHEST)
    scores = jax.nn.sigmoid(logits)
    biased = scores + rb
    lane = lax.broadcasted_iota(jnp.int32, biased.shape, 1)
    neg = jnp.float32(-jnp.inf)
    big = jnp.int32(1 << 20)
    best = bi1 = bi2 = None
    for g in range(N_GROUPS):
        ing = (lane >= g * EXPERTS_PER_GROUP) & (lane < (g + 1) * EXPERTS_PER_GROUP)
        v = jnp.where(ing, biased, neg)
        m1 = jnp.max(v, -1, keepdims=True)
        i1 = jnp.min(jnp.where(v == m1, lane, big), -1, keepdims=True)
        v2 = jnp.where(lane == i1, neg, v)
        m2 = jnp.max(v2, -1, keepdims=True)
        i2 = jnp.min(jnp.where(v2 == m2, lane, big), -1, keepdims=True)
        gs = m1 + m2
        if g == 0:
            best, bi1, bi2 = gs, i1, i2
        else:
            better = gs > best
            best = jnp.where(better, gs, best)
            bi1 = jnp.where(better, i1, bi1)
            bi2 = jnp.where(better, i2, bi2)
    g1 = jnp.sum(jnp.where(lane == bi1, scores, 0.0), -1, keepdims=True)
    g2 = jnp.sum(jnp.where(lane == bi2, scores, 0.0), -1, keepdims=True)
    tot = g1 + g2
    two = lax.broadcasted_iota(jnp.int32, (biased.shape[0], TOP_K), 1)
    return jnp.where(two == 0, bi1, bi2), jnp.where(two == 0, g1 / tot, g2 / tot)


def _expert_body(be_ref, nb_ref, x_ref, wg_ref, wu_ref, wd_ref, o_ref):
    del be_ref

    @pl.when(pl.program_id(0) < nb_ref[0])
    def _():
        x = x_ref[...]
        g = jnp.dot(x, wg_ref[0], preferred_element_type=F32)
        u = jnp.dot(x, wu_ref[0], preferred_element_type=F32)
        hid = (g * jax.nn.sigmoid(g) * u).astype(BF16)
        o_ref[...] = jnp.dot(hid, wd_ref[0], preferred_element_type=F32).astype(o_ref.dtype)

    @pl.when(pl.program_id(0) >= nb_ref[0])
    def _():
        o_ref[...] = jnp.zeros_like(o_ref)


def moe_ffn(x, hbf, sc, sh, router_w, router_bias, wg, wu, wd, *, seq):
    n, d = x.shape
    e = router_w.shape[1]
    idx, gates = rowwise(_router_fn, [x], [sc, sh], [router_w, router_bias.reshape(1, e)],
                         [(TOP_K, jnp.int32), (TOP_K, F32)], seq=seq, tm=256, name="router")
    bm = MOE_ROWS
    n_slots = n * TOP_K
    flat_e = idx.reshape(-1)
    onehot = (flat_e[:, None] == jnp.arange(e, dtype=jnp.int32)[None, :]).astype(jnp.int32)
    csum = jnp.cumsum(onehot, axis=0)
    pos = jnp.sum(onehot * csum, axis=1) - 1
    counts = csum[-1]
    padded = (counts + bm - 1) // bm * bm
    pend = jnp.cumsum(padded)
    pstart = pend - padded
    dest = pstart[flat_e] + pos
    n_blocks = n_slots // bm + e
    n_pad = n_blocks * bm
    slot_tok = jnp.zeros((n_pad,), jnp.int32).at[dest].set(jnp.arange(n_slots, dtype=jnp.int32) // TOP_K)
    block_start = jnp.arange(n_blocks, dtype=jnp.int32) * bm
    block_expert = jnp.minimum(jnp.sum((pend[None, :] <= block_start[:, None]).astype(jnp.int32), axis=1), e - 1)
    used_blocks = (pend[-1] // bm).astype(jnp.int32).reshape(1)
    xs = jnp.take(hbf, slot_tok, axis=0)
    de = wg.shape[2]
    yb = pl.pallas_call(
        _expert_body,
        grid_spec=pltpu.PrefetchScalarGridSpec(
            num_scalar_prefetch=2,
            grid=(n_blocks,),
            in_specs=[pl.BlockSpec((bm, d), lambda i, be, nb: (i, 0)),
                      pl.BlockSpec((1, d, de), lambda i, be, nb: (be[i], 0, 0)),
                      pl.BlockSpec((1, d, de), lambda i, be, nb: (be[i], 0, 0)),
                      pl.BlockSpec((1, de, d), lambda i, be, nb: (be[i], 0, 0))],
            out_specs=pl.BlockSpec((bm, d), lambda i, be, nb: (i, 0)),
        ),
        out_shape=jax.ShapeDtypeStruct((n_pad, d), BF16),
        compiler_params=_cparams("arbitrary"),
        name="moe_experts",
    )(block_expert, used_blocks, xs, wg, wu, wd)
    dest2 = dest.reshape(n, TOP_K)
    return jnp.take(yb, dest2[:, 0], axis=0), jnp.take(yb, dest2[:, 1], axis=0), gates


def _rwkv_prep_body(x_ref, xp_ref, sc_ref, sh_ref, mu_ref, *outs, tpb):
    sc = 1.0 + sc_ref[0]
    sh = sh_ref[0]
    h = x_ref[...] * sc + sh
    prev_row = xp_ref[7:8, :] * sc + sh
    prev_row = jnp.where(pl.program_id(0) % tpb == 0, 0.0, prev_row)
    rowid = lax.broadcasted_iota(jnp.int32, h.shape, 0)
    hprev = jnp.where(rowid == 0, prev_row, pltpu.roll(h, 1, 0))
    dx = hprev - h
    for m, o in enumerate(outs):
        o[...] = (h + dx * mu_ref[m:m + 1, :]).astype(o.dtype)


def rwkv_prep(x, sc, sh, mu, *, seq):
    n, d = x.shape
    tm = min(256, seq)
    tpb = seq // tm
    return pl.pallas_call(
        functools.partial(_rwkv_prep_body, tpb=tpb),
        grid=(n // tm,),
        in_specs=[pl.BlockSpec((tm, d), lambda i: (i, 0)),
                  pl.BlockSpec((8, d), lambda i: (jnp.maximum(i * (tm // 8) - 1, 0), 0)),
                  pl.BlockSpec((1, 1, d), lambda i: (i // tpb, 0, 0)),
                  pl.BlockSpec((1, 1, d), lambda i: (i // tpb, 0, 0)),
                  pl.BlockSpec((6, d), lambda i: (0, 0))],
        out_specs=[pl.BlockSpec((tm, d), lambda i: (i, 0))] * 6,
        out_shape=[jax.ShapeDtypeStruct((n, d), BF16)] * 6,
        compiler_params=_cparams("parallel"),
        name="rwkv_prep",
    )(x, x, sc, sh, mu)


def _softplus(z):
    return jnp.maximum(z, 0.0) + jnp.log(1.0 + jnp.exp(-jnp.abs(z)))


def _rwkv_gates_fn(k0, lw, al, *rest, has_vres):
    if has_vres:
        v, vfirst, vl, prm, ones_bd = rest
    else:
        prm, ones_bd = rest
    w0, a0, k_k, k_a = prm[0:1], prm[1:2], prm[2:3], prm[3:4]
    log_w = -_softplus(-(w0 + lw)) - 0.5
    logdecay = -jnp.exp(log_w)
    a = jax.nn.sigmoid(a0 + al)
    kk = k0 * k_k
    nrm = jnp.maximum(jnp.sqrt(_seg_sum(kk * kk, ones_bd)), 1e-12)
    kk = kk / nrm
    k = k0 * (1.0 + (a - 1.0) * k_a)
    res = [logdecay, k, kk, kk * a]
    if has_vres:
        res.append(v + (vfirst - v) * jax.nn.sigmoid(prm[4:5] + vl))
    return res


def _wkv_body(r_ref, lw_ref, k_ref, v_ref, kk_ref, b_ref, y_ref, *s_refs, chunk, heads):
    hd = RWKV_HEAD
    rows = r_ref.shape[0]
    width = heads * hd
    hr = heads * chunk
    nt = (((1,), (1,)), ((), ()))

    @pl.when(pl.program_id(2) == 0)
    def _():
        for s_ref in s_refs:
            s_ref[...] = jnp.zeros_like(s_ref)

    ti = lax.broadcasted_iota(jnp.int32, (chunk, chunk), 0)
    si = lax.broadcasted_iota(jnp.int32, (chunk, chunk), 1)
    tri = (ti >= si).astype(BF16)
    ri = lax.broadcasted_iota(jnp.int32, (hr, width), 0)
    ci = lax.broadcasted_iota(jnp.int32, (hr, width), 1)
    head_mask = (ri // chunk == ci // hd).astype(F32)
    rr = lax.broadcasted_iota(jnp.int32, (hr, hr), 0)
    cc = lax.broadcasted_iota(jnp.int32, (hr, hr), 1)
    strict = rr % chunk > cc % chunk
    eye = (rr == cc).astype(F32)
    tc = lax.broadcasted_iota(jnp.int32, (chunk, 2 * hr), 0)
    sc = lax.broadcasted_iota(jnp.int32, (chunk, 2 * hr), 1) % chunk
    incl_c = tc >= sc
    strict_c = (tc > sc)[:, :hr]

    def blocked(x):
        return (jnp.concatenate([x] * heads, 0) * head_mask).astype(BF16)

    def step(c, carry):
        sl = pl.ds(pl.multiple_of(c * chunk, chunk), chunk)
        gens = [group_step(sl, slice(grp * width, (grp + 1) * width), s_ref) for grp, s_ref in enumerate(s_refs)]
        for _ in itertools.zip_longest(*gens):
            pass
        return carry

    def group_step(sl, ln, s_ref):
        lw = lw_ref[sl, ln]
        cs = _split_dot_left(tri, lw)
        ctot = cs[chunk - 1:chunk, :]
        g_inv = jnp.exp(-cs)
        g_end = jnp.exp(ctot - cs)
        kk = kk_ref[sl, ln]
        kv = k_ref[sl, ln]
        bv = b_ref[sl, ln]
        ag = -kk * jnp.exp(cs - lw)
        bg = blocked(bv * g_inv)
        bk = jnp.concatenate([bg, blocked(kv * g_inv)], 0)
        ends = jnp.concatenate([blocked(bv * g_end), blocked(kv * g_end)], 0)
        vb = blocked(v_ref[sl, ln])
        s_old = s_ref[...]
        yield
        a_ab = jnp.where(strict, lax.dot_general(blocked(ag), bg, nt, preferred_element_type=F32), 0.0)
        ar = jnp.concatenate([ag, r_ref[sl, ln] * jnp.exp(cs)], 0).astype(BF16)
        yield
        m = lax.dot_general(ar, bk, nt, preferred_element_type=F32)
        a_ak = jnp.where(strict_c, m[:chunk, hr:], 0.0)
        a_r = jnp.where(incl_c, m[chunk:], 0.0)
        ars = lax.dot_general(ar, s_old.astype(BF16), nt, preferred_element_type=F32)
        yield
        rhs = blocked(ars[:chunk] + jnp.dot(a_ak.astype(BF16), vb, preferred_element_type=F32))
        inv = eye + a_ab
        apow = a_ab.astype(BF16)
        for _ in range(int(math.log2(chunk)) - 1):
            yield
            apow = jnp.dot(apow, apow, preferred_element_type=F32).astype(BF16)
            inv = inv + jnp.dot(inv.astype(BF16), apow, preferred_element_type=F32)
        yield
        u = jnp.dot(inv.astype(BF16), rhs, preferred_element_type=F32)
        uv = jnp.concatenate([u.astype(BF16), vb], 0)
        yield
        y_ref[sl, ln] = ars[chunk:] + jnp.dot(a_r.astype(BF16), uv, preferred_element_type=F32)
        s_ref[...] = s_old * jnp.exp(ctot) + lax.dot_general(
            uv, ends, (((0,), (0,)), ((), ())), preferred_element_type=F32)

    lax.fori_loop(0, rows // chunk, step, 0)


def _split_dot_left(m, x):
    xh = x.astype(BF16)
    xl = (x - xh.astype(F32)).astype(BF16)
    return jnp.dot(m, xh, preferred_element_type=F32) + jnp.dot(m, xl, preferred_element_type=F32)


def wkv7(r, logdecay, k, v, kk, b, *, batch, seq):
    n, d = r.shape
    chunk = min(WKV_CHUNK, seq)
    rows = min(WKV_ROWS_PER_STEP, seq)
    heads = WKV_HEADS_PER_STEP
    groups = WKV_GROUPS_PER_STEP
    width = heads * RWKV_HEAD
    spb = seq // rows
    spec = pl.BlockSpec((rows, groups * width), lambda bi, hi, ti: (bi * spb + ti, hi))
    return pl.pallas_call(
        functools.partial(_wkv_body, chunk=chunk, heads=heads),
        grid=(batch, d // (groups * width), spb),
        in_specs=[spec] * 6,
        out_specs=spec,
        out_shape=jax.ShapeDtypeStruct((n, d), F32),
        scratch_shapes=[pltpu.VMEM((width, width), F32)] * groups,
        compiler_params=_cparams("parallel", "parallel", "arbitrary"),
        name="wkv7",
    )(r, logdecay, k, v, kk, b)


def _rwkv_out_fn(y, r, k, v, g, prm, ones_bd):
    r_k, gn_g, gn_b = prm[0:1], prm[1:2], prm[2:3]
    inv_n = 1.0 / RWKV_HEAD
    m_y = _seg_sum(y, ones_bd) * inv_n
    yc = y - m_y
    v_y = _seg_sum(yc * yc, ones_bd) * inv_n
    yn = yc * lax.rsqrt(v_y + RWKV_GN_EPS) * gn_g + gn_b
    bonus = _seg_sum(r * k * r_k, ones_bd) * v
    return ((yn + bonus) * g,)


def rwkv_layer(x, sc, sh, v_first, p, *, batch, seq):
    n, d = x.shape
    xr, xw, xk, xv, xa, xg = rwkv_prep(x, sc, sh, p["mu"], seq=seq)
    r = mm(xr, p["w_r"])
    k0 = mm(xk, p["w_k"])
    v = mm(xv, p["w_v"])
    lw = mm(mm(xw, p["w1"], out_dtype=BF16, act="tanh"), p["w2"])
    al = mm(mm(xa, p["a1"], out_dtype=BF16), p["a2"])
    g = mm(mm(xg, p["g1"], out_dtype=BF16, act="sigmoid"), p["g2"])
    ones_bd = _head_ones()
    has_vres = p["v0"] is not None
    outs = [(d, F32)] * (5 if has_vres else 4)
    if has_vres:
        vl = mm(mm(xv, p["v1"], out_dtype=BF16), p["v2"])
        prm = jnp.stack([p["w0"], p["a0"], p["k_k"], p["k_a"], p["v0"]])
        rows = [k0, lw, al, v, v_first, vl]
    else:
        prm = jnp.stack([p["w0"], p["a0"], p["k_k"], p["k_a"]])
        rows = [k0, lw, al]
    res = rowwise(functools.partial(_rwkv_gates_fn, has_vres=has_vres), rows, [], [prm, ones_bd], outs,
                  seq=seq, tm=128, name="rwkv_gates")
    logdecay, k, kk, b = res[:4]
    if has_vres:
        v = res[4]
    else:
        v_first = v
    y = wkv7(r, logdecay, k, v, kk, b, batch=batch, seq=seq)
    prm2 = jnp.stack([p["r_k"], p["gn_g"], p["gn_b"]])
    yg = rowwise(_rwkv_out_fn, [y, r, k, v, g], [], [prm2, ones_bd], [(d, BF16)], seq=seq, tm=128, name="rwkv_out")[0]
    return mm(yg, p["w_o"]), v_first


SM_W0 = IDX_DIM
SM_R0 = LANES - DSA_ROPE_DIM
LAT_W = 768
LAT_PACK = LAT_W // 2
LAT_TILES = LAT_PACK // LANES
DSA_QUERIES_PER_ATTEND = 8


def _dsa_patterns():
    fq = ROPE_THETA ** (-np.arange(DSA_ROPE_DIM // 2, dtype=np.float32) * np.float32(2.0 / DSA_ROPE_DIM))
    fi = ROPE_THETA ** (-np.arange(IDX_ROPE_DIM // 2, dtype=np.float32) * np.float32(2.0 / IDX_ROPE_DIM))
    hq, hi = DSA_ROPE_DIM // 2, IDX_ROPE_DIM // 2
    pat = np.zeros((16, LANES), np.float32)
    lane = np.arange(LANES)
    pat[0, :hq] = fq; pat[0, hq:2 * hq] = fq
    pat[1, :hq] = -1.0; pat[2, hq:2 * hq] = 1.0
    l64 = lane % IDX_DIM
    pat[3] = np.where(l64 < hi, fi[np.minimum(l64, hi - 1)], np.where(l64 < 2 * hi, fi[np.clip(l64 - hi, 0, hi - 1)], 0.0))
    pat[4] = np.where(l64 < hi, -1.0, 0.0); pat[5] = np.where((l64 >= hi) & (l64 < 2 * hi), 1.0, 0.0)
    pat[6, :hi] = fi; pat[6, hi:2 * hi] = fi
    pat[6, SM_R0:SM_R0 + hq] = fq; pat[6, SM_R0 + hq:] = fq
    pat[7, :hi] = -1.0; pat[8, hi:2 * hi] = 1.0
    pat[9, SM_R0:SM_R0 + hq] = -1.0; pat[10, SM_R0 + hq:] = 1.0
    pat[11, :IDX_DIM] = 1.0
    pat[12, SM_W0:SM_W0 + IDX_HEADS] = IDX_HEADS ** -0.5 * IDX_DIM ** -0.5
    pat[13, SM_R0:] = 1.0
    return jnp.asarray(pat)


def _rot(x, shift):
    return pltpu.roll(x, shift % x.shape[1], 1)


def _tile_lanes(v, reps):
    return jnp.concatenate([v] * reps, 1) if reps > 1 else v


def _dsa_prep_fn(q, ckv, qidx, small, pos, kvn, lng, lnb, pat):
    posf = pos.astype(F32)
    hq, hi = DSA_ROPE_DIM // 2, IDX_ROPE_DIM // 2
    ang = posf * pat[0:1]
    c, s = jnp.cos(ang), jnp.sin(ang)
    reps = q.shape[1] // LANES
    qr = (q * _tile_lanes(c, reps) + _rot(q, -hq) * _tile_lanes(s * pat[1:2], reps)
          + _rot(q, hq) * _tile_lanes(s * pat[2:3], reps))
    ang = posf * pat[3:4]
    c, s = jnp.cos(ang), jnp.sin(ang)
    reps = qidx.shape[1] // LANES
    qi = (qidx * _tile_lanes(c, reps) + _rot(qidx, -hi) * _tile_lanes(s * pat[4:5], reps)
          + _rot(qidx, hi) * _tile_lanes(s * pat[5:6], reps))
    ckn = ckv * lax.rsqrt(jnp.mean(ckv * ckv, -1, keepdims=True) + 1e-6) * kvn
    mk = pat[11:12]
    inv = 1.0 / IDX_DIM
    mu = jnp.sum(small * mk, -1, keepdims=True) * inv
    dv = (small - mu) * mk
    var = jnp.sum(dv * dv, -1, keepdims=True) * inv
    y = (dv * lax.rsqrt(var + LN_EPS) * lng + lnb) * mk + small * pat[12:13] + small * pat[13:14]
    ang = posf * pat[6:7]
    c, s = jnp.cos(ang), jnp.sin(ang)
    so = (y * c + _rot(y, -hi) * (s * pat[7:8]) + _rot(y, hi) * (s * pat[8:9])
          + _rot(y, -hq) * (s * pat[9:10]) + _rot(y, hq) * (s * pat[10:11]))
    lat = jnp.concatenate([ckn, so * pat[13:14], jnp.zeros_like(so)], 1)
    lo = lax.bitcast_convert_type(lat[:, :LAT_PACK].astype(BF16).astype(F32), jnp.uint32) >> 16
    hi_b = lax.bitcast_convert_type(lat[:, LAT_PACK:].astype(BF16).astype(F32), jnp.uint32) & jnp.uint32(0xFFFF0000)
    return qr, lo | hi_b, qi, so, so


def _sortable(x):
    b = lax.bitcast_convert_type(x, jnp.int32)
    return jnp.where(b < 0, b ^ jnp.int32(0x7FFFFFFF), b)


def _dsa_select_body(qi_ref, sm_ref, kx_ref, sel_ref, key_ref, ib_ref, *, tq, ts, topk, seq):
    sub = 32
    t0 = pl.program_id(1) * tq
    nchunk = (t0 + tq + ts - 1) // ts
    qpos = t0 + lax.broadcasted_iota(jnp.int32, (tq, 1), 0)
    w = sm_ref[:, SM_W0:SM_W0 + IDX_HEADS]
    neg = jnp.float32(-jnp.inf)

    def score_chunk(c, carry):
        off = pl.multiple_of(c * ts, ts)
        ks = kx_ref[pl.ds(off, ts), :][:, :IDX_DIM]
        acc = jnp.zeros((tq, ts), F32)
        for h in range(IDX_HEADS):
            s = lax.dot_general(qi_ref[:, h * IDX_DIM:(h + 1) * IDX_DIM], ks, (((1,), (1,)), ((), ())),
                                preferred_element_type=F32)
            acc = acc + w[:, h:h + 1] * jnp.maximum(s, 0.0)
        kpos = off + lax.broadcasted_iota(jnp.int32, (1, ts), 1)
        key_ref[:, pl.ds(off, ts)] = _sortable(jnp.where(kpos <= qpos, acc, neg))
        return carry

    lax.fori_loop(0, nchunk, score_chunk, 0)

    tpc = ts // LANES
    lane_iota = lax.broadcasted_iota(jnp.int32, (1, LANES), 1)

    def count(pred):
        def body(c, acc):
            for u in range(tpc):
                off = pl.multiple_of(c * ts + u * LANES, LANES)
                acc = acc + pred(key_ref[:, pl.ds(off, LANES)], off + lane_iota).astype(jnp.int32)
            return acc
        acc = lax.fori_loop(0, nchunk, body, jnp.zeros((tq, LANES), jnp.int32))
        return jnp.sum(acc, -1, keepdims=True)

    sign = jnp.int32(-2 ** 31)
    thr_u = jnp.zeros((tq, 1), jnp.int32)
    for bit in range(31, -1, -1):
        cand_u = thr_u | jnp.int32(np.uint32(1 << bit).astype(np.int32))
        cand = cand_u ^ sign
        cnt = count(lambda k, kp, cand=cand: k >= cand)
        thr_u = jnp.where(cnt >= topk, cand_u, thr_u)
    thr = thr_u ^ sign
    need = topk - count(lambda k, kp: k > thr)
    n_eq = count(lambda k, kp: k == thr)

    ib_ref[...] = jnp.full((tq, 1), seq, jnp.int32)

    @pl.when(jnp.max(n_eq - need) > 0)
    def _():
        ib = jnp.zeros((tq, 1), jnp.int32)
        for bit in range(int(math.log2(seq)) - 1, -1, -1):
            cand = ib | jnp.int32(1 << bit)
            cnt = count(lambda k, kp, cand=cand: (k == thr) & (kp < cand))
            ib = jnp.where(cnt < need, cand, ib)
        ib_ref[...] = ib

    ib = ib_ref[...]

    def mask_tiles(c, acc):
        for u in range(tpc):
            off = pl.multiple_of(c * ts + u * LANES, LANES)
            kpos = off + lane_iota
            k = key_ref[:, pl.ds(off, LANES)]
            m = (((k > thr) | ((k == thr) & (kpos <= ib))) & (kpos <= qpos)).astype(jnp.int32)
            key_ref[:, pl.ds(off, LANES)] = m
            acc = acc + m
        return acc

    colcnt = lax.fori_loop(0, nchunk, mask_tiles, jnp.zeros((tq, LANES), jnp.int32))
    li = lax.broadcasted_iota(jnp.int32, (LANES, LANES), 0)
    lj = lax.broadcasted_iota(jnp.int32, (LANES, LANES), 1)
    cin = jnp.dot(colcnt.astype(BF16), (li <= lj).astype(BF16), preferred_element_type=F32).astype(jnp.int32)
    total = cin[:, LANES - 1:LANES]
    slot = lax.broadcasted_iota(jnp.int32, (tq, topk), 1)
    lane_of = jnp.zeros((tq, topk), jnp.int32)
    base = jnp.zeros((tq, topk), jnp.int32)
    for l in range(LANES):
        c = cin[:, l:l + 1]
        le = c <= slot
        lane_of = lane_of + le.astype(jnp.int32)
        base = jnp.maximum(base, jnp.where(le, c, 0))
    lane_of = jnp.minimum(lane_of, LANES - 1)
    rank = slot - base
    nhalf = topk // LANES
    for r0 in range(0, tq, sub):
        rows = slice(r0, r0 + sub)
        lanes_h = [lane_of[rows, i * LANES:(i + 1) * LANES] for i in range(nhalf)]
        ranks_h = [rank[rows, i * LANES:(i + 1) * LANES] for i in range(nhalf)]

        def tile_body(c, carry):
            carry = list(carry)
            for u in range(tpc):
                off = pl.multiple_of(c * ts + u * LANES, LANES)
                carry[0] = carry[0] + key_ref[r0:r0 + sub, pl.ds(off, LANES)]
                for i in range(nhalf):
                    got = jnp.take_along_axis(carry[0], lanes_h[i], axis=1)
                    carry[1 + i] = carry[1 + i] + (got <= ranks_h[i]).astype(jnp.int32)
            return tuple(carry)

        res = lax.fori_loop(0, nchunk, tile_body, (jnp.zeros((sub, LANES), jnp.int32),) * (1 + nhalf))
        pos = jnp.concatenate([res[1 + i] * LANES + lanes_h[i] for i in range(nhalf)], 1)
        slot_sub = lax.broadcasted_iota(jnp.int32, (sub, topk), 1)
        sel_ref[rows, :] = jnp.where(slot_sub < total[rows], pos, -1)


def dsa_select(qi, small, kx, *, batch, seq):
    n = qi.shape[0]
    topk = min(TOPK_MAX, seq // 4)
    tq = min(128, seq)
    ts = min(512, seq)
    qpb = seq // tq
    return pl.pallas_call(
        functools.partial(_dsa_select_body, tq=tq, ts=ts, topk=topk, seq=seq),
        grid=(batch, qpb),
        in_specs=[pl.BlockSpec((tq, qi.shape[1]), lambda b, i: (b * qpb + i, 0)),
                  pl.BlockSpec((tq, LANES), lambda b, i: (b * qpb + i, 0)),
                  pl.BlockSpec((seq, LANES), lambda b, i: (b, 0))],
        out_specs=pl.BlockSpec((tq, topk), lambda b, i: (b * qpb + i, 0)),
        out_shape=jax.ShapeDtypeStruct((n, topk), jnp.int32),
        scratch_shapes=[pltpu.VMEM((tq, seq), jnp.int32), pltpu.VMEM((tq, 1), jnp.int32)],
        compiler_params=_cparams("parallel", "arbitrary"),
        name="dsa_select",
    )(qi, small, kx)


def _dsa_attn_body(sel_s, selv_ref, q2_ref, tbl_ref, o_ref, *stages, tqa, topk):
    heads = DSA_HEADS
    scale = DSA_HEAD_DIM ** -0.5
    lt = LAT_TILES

    def gather(i, buf):
        base = i * topk
        for j in range(topk):
            stages[buf][pl.ds(j * lt, lt), :] = tbl_ref[pl.ds(sel_s[base + j], lt), :]

    def attend(i0, bufs):
        qs = range(len(bufs))
        pk = [jnp.concatenate([stages[b][pl.ds(t, topk, stride=lt), :] for t in range(lt)], 1) for b in bufs]
        kv = [jnp.concatenate([lax.bitcast_convert_type(x << 16, F32),
                               lax.bitcast_convert_type(x & jnp.uint32(0xFFFF0000), F32)], 1).astype(BF16)
              for x in pk]
        rows = [pl.ds(pl.multiple_of((i0 + a) * heads, heads), heads) for a in qs]
        s = [lax.dot_general(q2_ref[rows[a], :], kv[a], (((1,), (1,)), ((), ())), preferred_element_type=F32) * scale
             for a in qs]
        s = [jnp.where(selv_ref[pl.ds(i0 + a, 1), :] >= 0, s[a], -jnp.inf) for a in qs]
        e = [jnp.exp(s[a] - jnp.max(s[a], -1, keepdims=True)) for a in qs]
        p = [(e[a] / jnp.sum(e[a], -1, keepdims=True)).astype(BF16) for a in qs]
        o = [jnp.dot(p[a], kv[a][:, :DSA_KV_RANK], preferred_element_type=F32) for a in qs]
        for a in qs:
            o_ref[rows[a], :] = o[a].astype(o_ref.dtype)

    nq = len(stages) // 2
    first, second = tuple(range(nq)), tuple(range(nq, 2 * nq))
    for a in first:
        gather(a, a)

    def body(pi, carry):
        i0 = 2 * nq * pi
        for a in first:
            gather(i0 + nq + a, second[a])
        attend(i0, first)
        for a in first:
            gather(jnp.minimum(i0 + 2 * nq + a, tqa - 1), a)
        attend(i0 + nq, second)
        return carry

    lax.fori_loop(0, tqa // (2 * nq), body, 0)


def dsa_attend(sel, q2, table, *, batch, seq):
    n, topk = sel.shape
    tqa = min(128, seq)
    qpb = seq // tqa
    heads = DSA_HEADS
    return pl.pallas_call(
        functools.partial(_dsa_attn_body, tqa=tqa, topk=topk),
        grid=(batch, qpb),
        in_specs=[pl.BlockSpec((tqa * topk,), lambda b, i: (b * qpb + i,), memory_space=pltpu.SMEM),
                  pl.BlockSpec((tqa, topk), lambda b, i: (b * qpb + i, 0)),
                  pl.BlockSpec((tqa * heads, LAT_W), lambda b, i: (b * qpb + i, 0)),
                  pl.BlockSpec((seq * LAT_TILES, LANES), lambda b, i: (b, 0), pipeline_mode=pl.Buffered(1))],
        out_specs=pl.BlockSpec((tqa * heads, DSA_KV_RANK), lambda b, i: (b * qpb + i, 0)),
        out_shape=jax.ShapeDtypeStruct((n * heads, DSA_KV_RANK), BF16),
        scratch_shapes=[pltpu.VMEM((topk * LAT_TILES, LANES), jnp.uint32)] * (2 * DSA_QUERIES_PER_ATTEND),
        compiler_params=_cparams("parallel", "arbitrary"),
        name="dsa_attend",
    )((jnp.maximum(sel, 0) * LAT_TILES).reshape(-1), sel, q2, table.reshape(n * LAT_TILES, LANES))


def _head_mm_body(a_ref, w_ref, o_ref):
    o_ref[0] = jnp.dot(a_ref[...].astype(BF16), w_ref[0], preferred_element_type=F32).astype(o_ref.dtype)


def head_mm_out(a, w, *, tm=512):
    n = a.shape[0]
    heads, k, m = w.shape
    tm = min(tm, n)
    return pl.pallas_call(
        _head_mm_body,
        grid=(n // tm, heads),
        in_specs=[pl.BlockSpec((tm, k), lambda i, h: (i, h)), pl.BlockSpec((1, k, m), lambda i, h: (h, 0, 0))],
        out_specs=pl.BlockSpec((1, tm, m), lambda i, h: (h, i, 0)),
        out_shape=jax.ShapeDtypeStruct((heads, n, m), BF16),
        compiler_params=_cparams("parallel", "parallel"),
        name="head_mm_out",
    )(a, w)


def _head_mm_in_body(a_ref, w_ref, o_ref):
    o_ref[...] = jnp.dot(a_ref[0], w_ref[0], preferred_element_type=F32).astype(o_ref.dtype)


def head_mm_in(a, w, *, tm=512):
    heads, n, k = a.shape
    m = w.shape[2]
    tm = min(tm, n)
    return pl.pallas_call(
        _head_mm_in_body,
        grid=(n // tm, heads),
        in_specs=[pl.BlockSpec((1, tm, k), lambda i, h: (h, i, 0)), pl.BlockSpec((1, k, m), lambda i, h: (h, 0, 0))],
        out_specs=pl.BlockSpec((tm, m), lambda i, h: (i, h)),
        out_shape=jax.ShapeDtypeStruct((n, heads * m), BF16),
        compiler_params=_cparams("parallel", "parallel"),
        name="head_mm_in",
    )(a, w)


def dsa_layer(hbf, positions, p, *, batch, seq):
    n = hbf.shape[0]
    heads = DSA_HEADS
    q = mm(hbf, p["w_q"])
    ckv = mm(hbf, p["w_ckv"])
    qidx = mm(hbf, p["w_qidx"])
    small = mm(hbf, p["w_small"])
    qr, table, qi, small_o, kx = rowwise(
        _dsa_prep_fn, [q, ckv, qidx, small, positions.reshape(n, 1)], [],
        [p["kv_norm"], p["idx_ln_g"], p["idx_ln_b"], _dsa_patterns()],
        [(q.shape[1], BF16), (LAT_PACK, jnp.uint32), (qidx.shape[1], BF16), (LANES, F32), (LANES, BF16)],
        seq=seq, tm=128, name="dsa_prep")
    sel = dsa_select(qi, small_o, kx, batch=batch, seq=seq)
    q2 = head_mm_out(qr, p["w_q2"])
    q2 = q2.transpose(1, 0, 2).reshape(n * heads, LAT_W)
    o_lat = dsa_attend(sel, q2, table, batch=batch, seq=seq)
    o_lat = o_lat.reshape(n, heads, DSA_KV_RANK).transpose(1, 0, 2)
    out = head_mm_in(o_lat, p["w_uv"])
    return mm(out, p["w_o"])


def _s5_tables(p):
    hp = lax.Precision.HIGHEST
    lc, hg = S5_CHUNK, S5_GROUP
    lr, li = p["lam_re"], p["lam_im"]
    g, ps = lr.shape
    dt = jnp.exp(p["log_dt"])[:, None]
    mag = jnp.exp(lr * dt)
    ab_re, ab_im = mag * jnp.cos(li * dt), mag * jnp.sin(li * dt)
    den = lr * lr + li * li
    nr, ni = ab_re - 1.0, ab_im
    coef_re = (nr * lr + ni * li) / den
    coef_im = (ni * lr - nr * li) / den
    bb_re = coef_re[..., None] * p["b_re"] - coef_im[..., None] * p["b_im"]
    bb_im = coef_re[..., None] * p["b_im"] + coef_im[..., None] * p["b_re"]
    tau = jnp.arange(lc + 1, dtype=F32)[:, None, None]
    pmag = jnp.exp(lr * dt * tau)
    pr, pi = pmag * jnp.cos(li * dt * tau), pmag * jnp.sin(li * dt * tau)
    cr, ci = p["c_re"][None], p["c_im"][None]
    car = cr * pr[:, :, None, :] - ci * pi[:, :, None, :]
    cai = cr * pi[:, :, None, :] + ci * pr[:, :, None, :]
    kern = (jnp.einsum('tghp,gpk->tghk', car[:lc], bb_re, precision=hp)
            - jnp.einsum('tghp,gpk->tghk', cai[:lc], bb_im, precision=hp))
    kz = jnp.concatenate([kern, jnp.zeros((1,) + kern.shape[1:], F32)], 0)
    s_i = np.arange(lc)[:, None]
    t_i = np.arange(lc)[None, :]
    m = kz[np.where(t_i >= s_i, t_i - s_i, lc)]
    m = m.transpose(2, 0, 4, 1, 3).reshape(g, lc * hg, lc * hg)
    skip = jnp.tile(p["d"].reshape(g, 1, hg), (1, lc, 1)).reshape(g, 1, lc * hg)
    m = m + skip * jnp.eye(lc * hg, dtype=F32)[None]
    prs, pis = pr[lc - 1 - np.arange(lc)], pi[lc - 1 - np.arange(lc)]
    wre = prs[..., None] * bb_re[None] - pis[..., None] * bb_im[None]
    wim = prs[..., None] * bb_im[None] + pis[..., None] * bb_re[None]
    wre = wre.transpose(1, 0, 3, 2).reshape(g, lc * hg, ps)
    wim = wim.transpose(1, 0, 3, 2).reshape(g, lc * hg, ps)
    vre = car[1:].transpose(1, 3, 0, 2).reshape(g, ps, lc * hg)
    vim = (-cai[1:]).transpose(1, 3, 0, 2).reshape(g, ps, lc * hg)
    gu = S5_UNIT
    eye_u = jnp.eye(gu, dtype=F32)

    def unit(z, row_t, col_t):
        r, c = z.shape[1:]
        zz = z.reshape(g // gu, gu, r, 1, c) * eye_u[None, :, None, :, None]
        if row_t:
            zz = zz.reshape(g // gu, gu, lc, hg, gu, c).transpose(0, 2, 1, 3, 4, 5).reshape(g // gu, gu * r, gu, c)
        else:
            zz = zz.reshape(g // gu, gu * r, gu, c)
        if col_t:
            zz = zz.reshape(g // gu, gu * r, gu, lc, hg).transpose(0, 1, 3, 2, 4)
        return zz.reshape(g // gu, gu * r, gu * c).astype(BF16)

    are = pr[lc].reshape(g // gu, 1, gu * ps)
    aim = pi[lc].reshape(g // gu, 1, gu * ps)
    return (unit(m, True, True), unit(wre, True, False), unit(wim, True, False),
            unit(vre, False, True), unit(vim, False, True), are, aim)


def _gelu_tanh(y):
    return 0.5 * y * (1.0 + jnp.tanh(math.sqrt(2.0 / math.pi) * (y + 0.044715 * (y * y * y))))


def _s5_body(u_ref, m_ref, wre_ref, wim_ref, vre_ref, vim_ref, are_ref, aim_ref, z_ref, xre, xim, sre, sim, *, pairs):
    nc = u_ref.shape[1]
    for q in range(pairs):
        u = u_ref[q]
        xre[q] = jnp.dot(u, wre_ref[q], preferred_element_type=F32)
        xim[q] = jnp.dot(u, wim_ref[q], preferred_element_type=F32)
    ar = [are_ref[q] for q in range(pairs)]
    ai = [aim_ref[q] for q in range(pairs)]

    def step(c, carry):
        new = []
        row = pl.ds(c, 1)
        for q in range(pairs):
            re, im = carry[2 * q], carry[2 * q + 1]
            sre[q, row, :] = re
            sim[q, row, :] = im
            new.append(ar[q] * re - ai[q] * im + xre[q, row, :])
            new.append(ar[q] * im + ai[q] * re + xim[q, row, :])
        return tuple(new)

    zero = jnp.zeros((1, are_ref.shape[-1]), F32)
    lax.fori_loop(0, nc, step, (zero,) * (2 * pairs), unroll=8)
    rb = min(256, nc)
    for q in range(pairs):
        def out_block(i, carry, q=q):
            rs = pl.ds(pl.multiple_of(i * rb, rb), rb)
            y = (jnp.dot(u_ref[q, rs, :], m_ref[q], preferred_element_type=F32)
                 + jnp.dot(sre[q, rs, :].astype(BF16), vre_ref[q], preferred_element_type=F32)
                 + jnp.dot(sim[q, rs, :].astype(BF16), vim_ref[q], preferred_element_type=F32))
            z_ref[q, rs, :] = _gelu_tanh(y).astype(z_ref.dtype)
            return carry

        lax.fori_loop(0, nc // rb, out_block, 0)


def s5_layer(hbf, p, *, batch, seq):
    n, D = hbf.shape
    lc, hg = S5_CHUNK, S5_GROUP
    gu = S5_UNIT
    g2 = D // hg // gu
    pw = gu * lc * hg
    tabs = _s5_tables(p)
    nc = seq // lc
    u = hbf.reshape(n // lc, lc, g2, gu * hg).transpose(2, 0, 1, 3).reshape(g2, n // lc, pw)
    pairs = 1
    sw = gu * S5_STATE
    wspec = lambda r, c: pl.BlockSpec((pairs, r, c), lambda i, b: (i, 0, 0), pipeline_mode=pl.Buffered(1))
    z = pl.pallas_call(
        functools.partial(_s5_body, pairs=pairs),
        grid=(g2 // pairs, batch),
        in_specs=[pl.BlockSpec((pairs, nc, pw), lambda i, b: (i, b, 0)),
                  wspec(pw, pw), wspec(pw, sw), wspec(pw, sw), wspec(sw, pw), wspec(sw, pw),
                  wspec(1, sw), wspec(1, sw)],
        out_specs=pl.BlockSpec((pairs, nc, pw), lambda i, b: (i, b, 0)),
        out_shape=jax.ShapeDtypeStruct((g2, n // lc, pw), BF16),
        scratch_shapes=[pltpu.VMEM((pairs, nc, sw), F32)] * 4,
        compiler_params=_cparams("parallel", "parallel"),
        name="s5_scan",
    )(u, *tabs)
    z = z.reshape(g2, n // lc, lc, gu * hg).transpose(1, 2, 0, 3).reshape(n, D)
    zz = mm(z, p["w_glu"])
    return rowwise(lambda a: (a[:, :D] * jax.nn.sigmoid(a[:, D:]),), [zz], [], [], [(D, F32)], seq=seq, tm=256,
                   name="s5_glu")[0]


def _modulate(x, sc, sh, dtype, *, seq):
    d = x.shape[1]
    return rowwise(lambda xv, s, h: (xv * (1.0 + s) + h,), [x], [sc, sh], [], [(d, dtype)], seq=seq, tm=256,
                   name="modulate")[0]


def kernel(x, c, positions, ada_w, ada_b, ln_g, ln_b, router_w, router_bias, moe_w_gate, moe_w_up, moe_w_down,
           rwkv_mu, rwkv_w_rkv, rwkv_w_o, rwkv_w0, rwkv_w1, rwkv_w2, rwkv_a0, rwkv_a1, rwkv_a2, rwkv_g1, rwkv_g2,
           rwkv_k_k, rwkv_k_a, rwkv_r_k, rwkv_gn_g, rwkv_gn_b, rwkv_v0, rwkv_v1, rwkv_v2, dsa_w_in, dsa_kv_norm,
           dsa_w_uk, dsa_w_uv, dsa_idx_ln_g, dsa_idx_ln_b, dsa_w_o, s5_lam_re, s5_lam_im, s5_log_dt, s5_b_re,
           s5_b_im, s5_c_re, s5_c_im, s5_d, s5_w_glu):
    batch, seq, d = x.shape
    depth = ada_w.shape[0]
    n = batch * seq
    xf = x.reshape(n, d)
    mod = ada_mod(c, ada_w, ada_b)[:, :batch].reshape(depth, batch, 1, 6, d)
    v_first = None
    for i in range(depth):
        kind, j = i % N_MIXERS, i // N_MIXERS
        sh1, sc1, g1, sh2, sc2, g2 = (mod[i, :, :, m] for m in range(6))
        if kind == 0:
            p = dict(mu=rwkv_mu[j], w_r=rwkv_w_rkv[j, 0].astype(BF16), w_k=rwkv_w_rkv[j, 1].astype(BF16),
                     w_v=rwkv_w_rkv[j, 2].astype(BF16), w_o=rwkv_w_o[j].astype(BF16),
                     w0=rwkv_w0[j], w1=_wpad(rwkv_w1[j]), w2=_kpad(rwkv_w2[j]),
                     a0=rwkv_a0[j], a1=_wpad(rwkv_a1[j]), a2=_kpad(rwkv_a2[j]),
                     g1=_wpad(rwkv_g1[j]), g2=_kpad(rwkv_g2[j]),
                     k_k=rwkv_k_k[j], k_a=rwkv_k_a[j], r_k=rwkv_r_k[j], gn_g=rwkv_gn_g[j], gn_b=rwkv_gn_b[j],
                     v0=None)
            if j > 0:
                p.update(v0=rwkv_v0[j - 1], v1=_wpad(rwkv_v1[j - 1]), v2=_kpad(rwkv_v2[j - 1]))
            y, v_first = rwkv_layer(xf, sc1, sh1, v_first, p, batch=batch, seq=seq)
        elif kind == 1:
            w_in = dsa_w_in[j]
            hq = DSA_HEADS * DSA_HEAD_DIM
            o1 = hq + DSA_KV_RANK
            o2 = o1 + DSA_ROPE_DIM
            o3 = o2 + IDX_HEADS * IDX_DIM
            o4 = o3 + IDX_DIM
            gap = jnp.zeros((d, SM_R0 - SM_W0 - IDX_HEADS), F32)
            w_small = jnp.concatenate([w_in[:, o3:o4], w_in[:, o4:], gap, w_in[:, o1:o2]], axis=1)
            rope_pass = jnp.zeros((DSA_ROPE_DIM, LAT_W), F32).at[
                jnp.arange(DSA_ROPE_DIM), DSA_KV_RANK + SM_R0 + jnp.arange(DSA_ROPE_DIM)].set(1.0)
            w_uk_pad = jnp.pad(dsa_w_uk[j], ((0, 0), (0, 0), (0, LAT_W - DSA_KV_RANK)))
            w_q2 = jnp.concatenate([jnp.broadcast_to(rope_pass, (DSA_HEADS,) + rope_pass.shape), w_uk_pad], axis=1)
            lane_pad = lambda v: jnp.pad(v, (0, LANES - v.shape[0])).reshape(1, LANES)
            p = dict(w_q=w_in[:, :hq].astype(BF16), w_ckv=w_in[:, hq:o1].astype(BF16),
                     w_qidx=w_in[:, o2:o3].astype(BF16), w_small=w_small.astype(BF16),
                     kv_norm=dsa_kv_norm[j].reshape(1, -1), w_q2=w_q2.astype(BF16), w_uv=dsa_w_uv[j].astype(BF16),
                     idx_ln_g=lane_pad(dsa_idx_ln_g[j]), idx_ln_b=lane_pad(dsa_idx_ln_b[j]),
                     w_o=dsa_w_o[j].astype(BF16))
            hbf = _modulate(xf, sc1, sh1, BF16, seq=seq)
            y = dsa_layer(hbf, positions, p, batch=batch, seq=seq)
        else:
            p = dict(lam_re=s5_lam_re[j], lam_im=s5_lam_im[j], log_dt=s5_log_dt[j], b_re=s5_b_re[j],
                     b_im=s5_b_im[j], c_re=s5_c_re[j], c_im=s5_c_im[j], d=s5_d[j], w_glu=s5_w_glu[j].astype(BF16))
            hbf = _modulate(xf, sc1, sh1, BF16, seq=seq)
            y = s5_layer(hbf, p, batch=batch, seq=seq)
        xf, hbf = post_norm_mix(xf, y, g1, sc2, sh2, ln_g[i, 0:1], ln_b[i, 0:1], seq=seq)
        y0, y1, gates = moe_ffn(xf, hbf, sc2, sh2, router_w, router_bias, moe_w_gate[i].astype(BF16),
                                moe_w_up[i].astype(BF16), moe_w_down[i].astype(BF16), seq=seq)
        xf = post_norm_moe(xf, y0, y1, gates, g2, ln_g[i, 1:2], ln_b[i, 1:2], seq=seq)
    return xf.reshape(batch, seq, d)
```
